```python
import math
import jax, jax.numpy as jnp
from jax import lax
import numpy as np

D_MODEL = 1024
BATCH = 32
SEQ = 2048
DEPTH = 1

GRID_W = 64
CTX_LEN = 256
N_MOD = 6
S5_WIDTH = 1024
S5_GROUP_CH = 16
S5_GROUPS = S5_WIDTH // S5_GROUP_CH
S5_STATE = 64
S5_DT_MIN = 1e-3
S5_DT_MAX = 1e-1
S5_MAX_RE = -1e-4
HEAD_DIM = 64
N_Q_HEADS = 16
N_KV_HEADS = 4
GQ = N_Q_HEADS // N_KV_HEADS
ATTN_WIDTH = N_Q_HEADS * HEAD_DIM
KV_WIDTH = N_KV_HEADS * HEAD_DIM
ROPE_AXIS_DIM = HEAD_DIM // 2
ROPE_THETA = 10000.0
Q_BLOCK = 128
IN_SPLITS = (S5_WIDTH, S5_WIDTH + ATTN_WIDTH, S5_WIDTH + ATTN_WIDTH + KV_WIDTH,
             S5_WIDTH + ATTN_WIDTH + 2 * KV_WIDTH, S5_WIDTH + ATTN_WIDTH + 2 * KV_WIDTH + D_MODEL)
IN_WIDTH = S5_WIDTH + ATTN_WIDTH + 2 * KV_WIDTH + 2 * D_MODEL
N_EXPERTS = 256
EXPERT_HIDDEN = 256
TOP_K = 8
N_EXPERT_GROUPS = 8
EXPERTS_PER_GROUP = N_EXPERTS // N_EXPERT_GROUPS
TOPK_GROUPS = 4
ROUTED_SCALE = 2.5
SHARED_HIDDEN = 256
MOE_BLOCK = 128
LN_EPS = 1e-5
MOD_EPS = 1e-6
RMS_EPS = 1e-6

kernel_name = 'hybrid_s5_gqa_moe_prefix_dit_block'


def _layer_norm(x, g, b):
    x32 = x.astype(jnp.float32)
    mu = jnp.mean(x32, -1, keepdims=True)
    var = jnp.mean(jnp.square(x32 - mu), -1, keepdims=True)
    y = (x32 - mu) * lax.rsqrt(var + LN_EPS) * g.astype(jnp.float32) + b.astype(jnp.float32)
    return y.astype(x.dtype)


def _modulate(x, shift, scale):
    x32 = x.astype(jnp.float32)
    mu = jnp.mean(x32, -1, keepdims=True)
    var = jnp.mean(jnp.square(x32 - mu), -1, keepdims=True)
    xn = ((x32 - mu) * lax.rsqrt(var + MOD_EPS)).astype(x.dtype)
    return xn * (1.0 + scale) + shift


def _rms_norm(t, g):
    t32 = t.astype(jnp.float32)
    y = t32 * lax.rsqrt(jnp.mean(t32 * t32, -1, keepdims=True) + RMS_EPS) * g.astype(jnp.float32)
    return y.astype(t.dtype)


def _axial_rope_tables(rows):
    inv_freq = ROPE_THETA ** (-jnp.arange(0, ROPE_AXIS_DIM, 2, dtype=jnp.float32) / ROPE_AXIS_DIM)
    pos_row = jnp.repeat(jnp.arange(rows, dtype=jnp.float32), GRID_W)
    pos_col = jnp.tile(jnp.arange(GRID_W, dtype=jnp.float32), rows)
    ang_r = pos_row[:, None] * inv_freq
    ang_c = pos_col[:, None] * inv_freq
    return (jnp.cos(ang_r), jnp.sin(ang_r), jnp.cos(ang_c), jnp.sin(ang_c))


def _rotate(t, cos, sin):
    t1, t2 = jnp.split(t, 2, axis=-1)
    cos = cos[:, None, :].astype(t.dtype)
    sin = sin[:, None, :].astype(t.dtype)
    return jnp.concatenate([t1 * cos - t2 * sin, t1 * sin + t2 * cos], axis=-1)


def _rope_2d(t, rope):
    cos_r, sin_r, cos_c, sin_c = rope
    t_row, t_col = jnp.split(t, 2, axis=-1)
    return jnp.concatenate([_rotate(t_row, cos_r, sin_r), _rotate(t_col, cos_c, sin_c)], axis=-1)


def _scan_combine(e1, e2):
    a1, b1 = e1
    a2, b2 = e2
    return a2 * a1, a2 * b1 + b2


def _diag_scan(lam_bar, bu, reverse, h0=None):
    if h0 is not None:
        edge = -1 if reverse else 0
        bu = bu.at[:, edge].add(lam_bar * h0)
    a = jnp.broadcast_to(lam_bar, bu.shape)
    _, h = lax.associative_scan(_scan_combine, (a, bu), reverse=reverse, axis=1)
    return h


def _s5_discretise(lam_re, lam_im, log_dt, b_re, b_im):
    lam = lax.complex(jnp.minimum(lam_re.astype(jnp.float32), S5_MAX_RE), lam_im.astype(jnp.float32))
    dt = jnp.exp(log_dt.astype(jnp.float32))[:, None]
    lam_bar = jnp.exp(lam * dt)
    b = lax.complex(b_re.astype(jnp.float32), b_im.astype(jnp.float32))
    b_bar = ((lam_bar - 1.0) / lam)[:, :, None] * b
    return lam_bar, b_bar


def _s5_readout(h, c_mat):
    y = jnp.real(jnp.einsum('blgp,gcp->blgc', h, c_mat))
    return y.reshape(h.shape[0], h.shape[1], S5_WIDTH)


def _half_glu(y, w_glu):
    z = jax.nn.gelu(y)
    return z * jax.nn.sigmoid(z @ w_glu.astype(jnp.float32))


def _s5_branch(s_lat, s_ctx, lam_re, lam_im, log_dt, b_re, b_im, c_re, c_im, d_skip, w_glu, keep_ctx):
    def groups(s):
        return s.astype(jnp.float32).reshape(s.shape[0], s.shape[1], S5_GROUPS, S5_GROUP_CH).astype(jnp.complex64)
    ul, uc = groups(s_lat), groups(s_ctx)
    d32 = d_skip.astype(jnp.float32)
    y_lat = s_lat.astype(jnp.float32) * d32
    y_ctx = s_ctx.astype(jnp.float32) * d32 if keep_ctx else None
    for direction, reverse in enumerate((False, True)):
        lam_bar, b_bar = _s5_discretise(lam_re[direction], lam_im[direction], log_dt[direction],
                                        b_re[direction], b_im[direction])
        c_mat = lax.complex(c_re[direction].astype(jnp.float32), c_im[direction].astype(jnp.float32))
        h_ctx = _diag_scan(lam_bar, jnp.einsum('blgc,gpc->blgp', uc, b_bar), reverse)
        h_ctx_final = h_ctx[:, 0] if reverse else h_ctx[:, -1]
        h_lat = _diag_scan(lam_bar, jnp.einsum('blgc,gpc->blgp', ul, b_bar), reverse, h_ctx_final)
        y_lat = y_lat + _s5_readout(h_lat, c_mat)
        if keep_ctx:
            y_ctx = y_ctx + _s5_readout(h_ctx, c_mat)
    out_lat = _half_glu(y_lat, w_glu).astype(s_lat.dtype)
    out_ctx = _half_glu(y_ctx, w_glu).astype(s_ctx.dtype) if keep_ctx else None
    return out_lat, out_ctx


def _block_attention(q, k, v):
    B, Lq = q.shape[0], q.shape[1]
    nb = Lq // Q_BLOCK
    qb = q.reshape(B, nb, Q_BLOCK, N_KV_HEADS, GQ, HEAD_DIM).swapaxes(0, 1)
    scale = HEAD_DIM ** -0.5

    def one_block(qi):
        s = jnp.einsum('bqhgd,bkhd->bhgqk', qi, k).astype(jnp.float32) * scale
        p = jax.nn.softmax(s, axis=-1).astype(v.dtype)
        return jnp.einsum('bhgqk,bkhd->bqhgd', p, v)

    o = lax.map(one_block, qb)
    return o.swapaxes(0, 1).reshape(B, Lq, ATTN_WIDTH)


def _gqa_branch(q_lat, k_lat, v_lat, q_ctx, k_ctx, v_ctx, q_norm_g, k_norm_g, rope, keep_ctx):
    def heads(t, n):
        return t.reshape(t.shape[0], t.shape[1], n, HEAD_DIM)
    ql = _rope_2d(_rms_norm(heads(q_lat, N_Q_HEADS), q_norm_g), rope)
    kl = _rope_2d(_rms_norm(heads(k_lat, N_KV_HEADS), k_norm_g), rope)
    kc = _rms_norm(heads(k_ctx, N_KV_HEADS), k_norm_g)
    vl, vc = heads(v_lat, N_KV_HEADS), heads(v_ctx, N_KV_HEADS)
    k_all = jnp.concatenate([kc, kl], axis=1)
    v_all = jnp.concatenate([vc, vl], axis=1)
    out_lat = _block_attention(ql, k_all, v_all)
    out_ctx = None
    if keep_ctx:
        qc = _rms_norm(heads(q_ctx, N_Q_HEADS), q_norm_g)
        out_ctx = _block_attention(qc, kc, vc)
    return out_lat, out_ctx


def _mixer(u_lat, u_ctx, w_in, s5_params, q_norm_g, k_norm_g, w_branch_ssm, w_branch_attn, w_out, rope, keep_ctx):
    s_lat, q_lat, k_lat, v_lat, gs_lat, ga_lat = jnp.split(u_lat @ w_in, IN_SPLITS, axis=-1)
    s_ctx, q_ctx, k_ctx, v_ctx, gs_ctx, ga_ctx = jnp.split(u_ctx @ w_in, IN_SPLITS, axis=-1)
    ssm_lat, ssm_ctx = _s5_branch(s_lat, s_ctx, *s5_params, keep_ctx)
    attn_lat, attn_ctx = _gqa_branch(q_lat, k_lat, v_lat, q_ctx, k_ctx, v_ctx, q_norm_g, k_norm_g, rope, keep_ctx)

    def merge(ssm, attn, gs, ga):
        merged = jax.nn.sigmoid(gs) * (ssm @ w_branch_ssm) + jax.nn.sigmoid(ga) * (attn @ w_branch_attn)
        return merged @ w_out

    y_lat = merge(ssm_lat, attn_lat, gs_lat, ga_lat)
    y_ctx = merge(ssm_ctx, attn_ctx, gs_ctx, ga_ctx) if keep_ctx else None
    return y_lat, y_ctx


def _routed_experts(ht, idx, gates, w_gate, w_up, w_down):
    T, D = ht.shape
    M = T * TOP_K
    flat_e = idx.reshape(M)
    order = jnp.argsort(flat_e)
    sorted_e = flat_e[order]
    counts = jnp.bincount(flat_e, length=N_EXPERTS)
    padded = (counts + MOE_BLOCK - 1) // MOE_BLOCK * MOE_BLOCK
    pad_end = jnp.cumsum(padded)
    pad_start = pad_end - padded
    grp_start = jnp.cumsum(counts) - counts
    dest = pad_start[sorted_e] + jnp.arange(M, dtype=pad_start.dtype) - grp_start[sorted_e]
    n_blocks = (M + N_EXPERTS * (MOE_BLOCK - 1) + MOE_BLOCK - 1) // MOE_BLOCK
    n_slots = n_blocks * MOE_BLOCK
    slot_token = jnp.full((n_slots,), T, jnp.int32).at[dest].set((order // TOP_K).astype(jnp.int32))
    slot_gate = jnp.zeros((n_slots,), jnp.float32).at[dest].set(gates.reshape(M)[order])
    block_start = jnp.arange(n_blocks, dtype=pad_end.dtype) * MOE_BLOCK
    block_expert = jnp.minimum(jnp.searchsorted(pad_end, block_start, side='right'), N_EXPERTS - 1)
    h_pad = jnp.concatenate([ht, jnp.zeros((1, D), ht.dtype)], axis=0)

    def expert_block(args):
        tok, gate, e = args
        xb = h_pad[tok]
        hid = jax.nn.silu(xb @ w_gate[e]) * (xb @ w_up[e])
        return (hid @ w_down[e]) * gate[:, None].astype(ht.dtype)

    out = lax.map(expert_block, (slot_token.reshape(n_blocks, MOE_BLOCK),
                                 slot_gate.reshape(n_blocks, MOE_BLOCK), block_expert))
    return jax.ops.segment_sum(out.reshape(n_slots, D), slot_token, num_segments=T + 1)[:T]


def _moe_ffn(h, w_router, router_bias, w_exp_gate, w_exp_up, w_exp_down, w_sh_gate, w_sh_up, w_sh_down):
    shape = h.shape
    ht = h.reshape(-1, shape[-1])
    T = ht.shape[0]
    scores = jax.nn.sigmoid((ht @ w_router).astype(jnp.float32))
    biased = scores + router_bias.astype(jnp.float32)
    grouped = biased.reshape(T, N_EXPERT_GROUPS, EXPERTS_PER_GROUP)
    group_score = jnp.sum(lax.top_k(grouped, 2)[0], axis=-1)
    _, top_groups = lax.top_k(group_score, TOPK_GROUPS)
    group_keep = jnp.sum(jax.nn.one_hot(top_groups, N_EXPERT_GROUPS, dtype=jnp.float32), axis=1) > 0
    expert_keep = jnp.repeat(group_keep, EXPERTS_PER_GROUP, axis=1)
    _, idx = lax.top_k(jnp.where(expert_keep, biased, -jnp.inf), TOP_K)
    sel = jnp.take_along_axis(scores, idx, axis=-1)
    gates = sel / jnp.sum(sel, axis=-1, keepdims=True) * ROUTED_SCALE
    routed = _routed_experts(ht, idx, gates, w_exp_gate, w_exp_up, w_exp_down)
    shared = (jax.nn.silu(ht @ w_sh_gate) * (ht @ w_sh_up)) @ w_sh_down
    return (routed + shared).reshape(shape)


def setup_inputs(seed: int = 0) -> dict:
    key = jax.random.key(seed)
    ks = jax.random.split(key, 40)
    f32 = jnp.float32

    def nrm(k, shape, scale):
        return jax.random.normal(k, shape, f32) * scale

    beta = (8.0 * DEPTH) ** -0.25
    n_idx = jnp.arange(S5_STATE, dtype=f32)
    return {
        'x': nrm(ks[0], (BATCH, SEQ, D_MODEL), 1.0),
        'c': nrm(ks[1], (BATCH, D_MODEL), 1.0),
        'ctx': nrm(ks[2], (BATCH, CTX_LEN, D_MODEL), 1.0),
        'c_ctx': nrm(ks[3], (D_MODEL,), 1.0),
        'w_mod': nrm(ks[4], (DEPTH, D_MODEL, N_MOD * D_MODEL), 0.5 * D_MODEL ** -0.5),
        'b_mod': nrm(ks[5], (DEPTH, N_MOD * D_MODEL), 0.02),
        'w_in': nrm(ks[6], (DEPTH, D_MODEL, IN_WIDTH), D_MODEL ** -0.5),
        's5_lam_re': -0.5 + nrm(ks[7], (DEPTH, 2, S5_GROUPS, S5_STATE), 0.01),
        's5_lam_im': jnp.pi * n_idx + nrm(ks[8], (DEPTH, 2, S5_GROUPS, S5_STATE), 0.01),
        's5_log_dt': jax.random.uniform(ks[9], (DEPTH, 2, S5_GROUPS), f32,
                                        minval=math.log(S5_DT_MIN), maxval=math.log(S5_DT_MAX)),
        's5_b_re': nrm(ks[10], (DEPTH, 2, S5_GROUPS, S5_STATE, S5_GROUP_CH), (2.0 * S5_GROUP_CH) ** -0.5),
        's5_b_im': nrm(ks[11], (DEPTH, 2, S5_GROUPS, S5_STATE, S5_GROUP_CH), (2.0 * S5_GROUP_CH) ** -0.5),
        's5_c_re': nrm(ks[12], (DEPTH, 2, S5_GROUPS, S5_GROUP_CH, S5_STATE), (2.0 * S5_STATE) ** -0.5),
        's5_c_im': nrm(ks[13], (DEPTH, 2, S5_GROUPS, S5_GROUP_CH, S5_STATE), (2.0 * S5_STATE) ** -0.5),
        's5_d': nrm(ks[14], (DEPTH, S5_WIDTH), 0.5),
        'w_glu': nrm(ks[15], (DEPTH, S5_WIDTH, S5_WIDTH), S5_WIDTH ** -0.5),
        'q_norm_g': 1.0 + nrm(ks[16], (DEPTH, HEAD_DIM), 0.02),
        'k_norm_g': 1.0 + nrm(ks[17], (DEPTH, HEAD_DIM), 0.02),
        'w_branch_ssm': nrm(ks[18], (DEPTH, S5_WIDTH, D_MODEL), S5_WIDTH ** -0.5),
        'w_branch_attn': nrm(ks[19], (DEPTH, ATTN_WIDTH, D_MODEL), ATTN_WIDTH ** -0.5),
        'w_out': nrm(ks[20], (DEPTH, D_MODEL, D_MODEL), beta * D_MODEL ** -0.5),
        'ln1_g': 1.0 + nrm(ks[21], (DEPTH, D_MODEL), 0.02),
        'ln1_b': nrm(ks[22], (DEPTH, D_MODEL), 0.02),
        'w_router': nrm(ks[23], (DEPTH, D_MODEL, N_EXPERTS), D_MODEL ** -0.5),
        'router_bias': nrm(ks[24], (DEPTH, N_EXPERTS), 0.01),
        'w_exp_gate': nrm(ks[25], (DEPTH, N_EXPERTS, D_MODEL, EXPERT_HIDDEN), D_MODEL ** -0.5),
        'w_exp_up': nrm(ks[26], (DEPTH, N_EXPERTS, D_MODEL, EXPERT_HIDDEN), D_MODEL ** -0.5),
        'w_exp_down': nrm(ks[27], (DEPTH, N_EXPERTS, EXPERT_HIDDEN, D_MODEL), beta * EXPERT_HIDDEN ** -0.5),
        'w_sh_gate': nrm(ks[28], (DEPTH, D_MODEL, SHARED_HIDDEN), D_MODEL ** -0.5),
        'w_sh_up': nrm(ks[29], (DEPTH, D_MODEL, SHARED_HIDDEN), D_MODEL ** -0.5),
        'w_sh_down': nrm(ks[30], (DEPTH, SHARED_HIDDEN, D_MODEL), beta * SHARED_HIDDEN ** -0.5),
        'ln2_g': 1.0 + nrm(ks[31], (DEPTH, D_MODEL), 0.02),
        'ln2_b': nrm(ks[32], (DEPTH, D_MODEL), 0.02),
    }


def reference(x, c, ctx, c_ctx, w_mod, b_mod, w_in, s5_lam_re, s5_lam_im, s5_log_dt, s5_b_re, s5_b_im,
              s5_c_re, s5_c_im, s5_d, w_glu, q_norm_g, k_norm_g, w_branch_ssm, w_branch_attn, w_out,
              ln1_g, ln1_b, w_router, router_bias, w_exp_gate, w_exp_up, w_exp_down,
              w_sh_gate, w_sh_up, w_sh_down, ln2_g, ln2_b):
    alpha = (2.0 * DEPTH) ** 0.25
    rows = x.shape[1] // GRID_W
    rope = _axial_rope_tables(rows)
    x_lat, x_ctx = x, ctx
    for layer in range(DEPTH):
        keep_ctx = layer < DEPTH - 1
        mod = jax.nn.silu(c) @ w_mod[layer] + b_mod[layer]
        mod_c = jax.nn.silu(c_ctx) @ w_mod[layer] + b_mod[layer]
        sh1, sc1, g1, sh2, sc2, g2 = jnp.split(mod[:, None, :], N_MOD, axis=-1)
        csh1, csc1, cg1, csh2, csc2, cg2 = jnp.split(mod_c, N_MOD, axis=-1)
        s5_params = (s5_lam_re[layer], s5_lam_im[layer], s5_log_dt[layer], s5_b_re[layer], s5_b_im[layer],
                     s5_c_re[layer], s5_c_im[layer], s5_d[layer], w_glu[layer])
        moe_params = (w_router[layer], router_bias[layer], w_exp_gate[layer], w_exp_up[layer], w_exp_down[layer],
                      w_sh_gate[layer], w_sh_up[layer], w_sh_down[layer])
        y_lat, y_ctx = _mixer(_modulate(x_lat, sh1, sc1), _modulate(x_ctx, csh1, csc1), w_in[layer], s5_params,
                              q_norm_g[layer], k_norm_g[layer], w_branch_ssm[layer], w_branch_attn[layer],
                              w_out[layer], rope, keep_ctx)
        x_lat = _layer_norm(alpha * x_lat + g1 * y_lat, ln1_g[layer], ln1_b[layer])
        f_lat = _moe_ffn(_modulate(x_lat, sh2, sc2), *moe_params)
        x_lat = _layer_norm(alpha * x_lat + g2 * f_lat, ln2_g[layer], ln2_b[layer])
        if keep_ctx:
            x_ctx = _layer_norm(alpha * x_ctx + cg1 * y_ctx, ln1_g[layer], ln1_b[layer])
            f_ctx = _moe_ffn(_modulate(x_ctx, csh2, csc2), *moe_params)
            x_ctx = _layer_norm(alpha * x_ctx + cg2 * f_ctx, ln2_g[layer], ln2_b[layer])
    return x_lat
```

```python
import functools
import math

import jax
import jax.numpy as jnp
from jax import lax
from jax.experimental import pallas as pl
from jax.experimental.pallas import tpu as pltpu

F32 = jnp.float32
BF16 = jnp.bfloat16

GRID_W = 64
HEAD_DIM = 64
N_KV_HEADS = 4
S5_GROUP_CH = 16
S5_MAX_RE = -1e-4
ROPE_THETA = 10000.0
TOP_K = 8
N_EXPERT_GROUPS = 8
TOPK_GROUPS = 4
ROUTED_SCALE = 2.5
LN_EPS = 1e-5
MOD_EPS = 1e-6
RMS_EPS = 1e-6
N_MOD = 6

LANES = 128
SUBLANES = 8
VMEM_LIMIT_BYTES = 56 * 1024 * 1024

S5_CHUNK = 16
S5_COLS = S5_CHUNK * S5_GROUP_CH
MOE_BLOCK = 256
COMBINE_TOKENS = 64


def _params(sem):
    return pltpu.CompilerParams(dimension_semantics=sem, vmem_limit_bytes=VMEM_LIMIT_BYTES)


def _const_spec(shape):
    nd = len(shape)
    return pl.BlockSpec(shape, lambda *_: (0,) * nd, pipeline_mode=pl.Buffered(1))


def _dot(a, b):
    return jnp.dot(a, b, preferred_element_type=F32)


def _norm_rows(x, eps):
    mu = jnp.mean(x, axis=-1, keepdims=True)
    xc = x - mu
    var = jnp.mean(xc * xc, axis=-1, keepdims=True)
    return xc * lax.rsqrt(var + eps)


def _silu(x):
    return x * jax.nn.sigmoid(x)


def _gelu_tanh(x):
    return 0.5 * x * (1.0 + jnp.tanh(math.sqrt(2.0 / math.pi) * (x + 0.044715 * (x * x * x))))


def _mod_kernel(c_ref, w_ref, b_ref, o_ref):
    o_ref[...] = _dot(_silu(c_ref[...]), w_ref[...]) + b_ref[...]


def _mod_call(c_all, w_mod, b_mod):
    rows, d = c_all.shape
    n = w_mod.shape[1]
    tn = d
    return pl.pallas_call(
        _mod_kernel,
        out_shape=jax.ShapeDtypeStruct((rows, n), F32),
        grid=(n // tn,),
        in_specs=[pl.BlockSpec((rows, d), lambda j: (0, 0)),
                  pl.BlockSpec((d, tn), lambda j: (0, j)),
                  pl.BlockSpec((1, tn), lambda j: (0, j))],
        out_specs=pl.BlockSpec((rows, tn), lambda j: (0, j)),
        compiler_params=_params(("arbitrary",)),
        name="mod",
    )(c_all, w_mod, b_mod.reshape(1, n))


def _swap16(t):
    width = t.shape[1]
    lane = lax.broadcasted_iota(jnp.int32, t.shape, 1)
    first = (lane & 16) == 0
    return jnp.where(first, pltpu.roll(t, width - 16, 1), pltpu.roll(t, 16, 1))


def _rms_rope(t, bd_ref, ta, tb):
    msq = _dot((t * t).astype(BF16), bd_ref[...])
    return lax.rsqrt(msq + RMS_EPS) * (t * ta + _swap16(t) * tb)


def _store_padded_heads(t, o_ref):
    rows = t.shape[0]
    lane = lax.broadcasted_iota(jnp.int32, (rows, LANES), 1)
    lo = lane < HEAD_DIM
    for j in range(t.shape[1] // LANES):
        slab = t[:, j * LANES:(j + 1) * LANES]
        a = jnp.where(lo, slab, 0.0)
        b = jnp.where(lo, 0.0, slab)
        pieces = (a, pltpu.roll(a, HEAD_DIM, 1), pltpu.roll(b, HEAD_DIM, 1), b)
        for p, piece in enumerate(pieces):
            c0 = (4 * j + p) * LANES
            o_ref[:, c0:c0 + LANES] = piece.astype(o_ref.dtype)


def _inproj_kernel(x_ref, sh_ref, sc_ref, w_ref, bd_ref, tka_ref, tkb_ref, *rest, d, kvw, has_q):
    if has_q:
        tqa_ref, tqb_ref, s_ref, k_ref, v_ref, q_ref, gs_ref, ga_ref = rest
    else:
        s_ref, k_ref, v_ref = rest
    u = (_norm_rows(x_ref[...], MOD_EPS) * (1.0 + sc_ref[...]) + sh_ref[...]).astype(BF16)
    col = 0
    s_ref[...] = _dot(u, w_ref[:, col:col + d]).astype(s_ref.dtype)
    col += d
    if has_q:
        for c in range(d // kvw):
            q = _dot(u, w_ref[:, col + c * kvw:col + (c + 1) * kvw])
            q_ref[:, c * kvw:(c + 1) * kvw] = _rms_rope(q, bd_ref, tqa_ref[...], tqb_ref[...]).astype(q_ref.dtype)
        col += d
    k = _dot(u, w_ref[:, col:col + kvw])
    _store_padded_heads(_rms_rope(k, bd_ref, tka_ref[...], tkb_ref[...]), k_ref)
    col += kvw
    _store_padded_heads(_dot(u, w_ref[:, col:col + kvw]), v_ref)
    col += kvw
    if has_q:
        gs_ref[...] = jax.nn.sigmoid(_dot(u, w_ref[:, col:col + d])).astype(gs_ref.dtype)
        col += d
        ga_ref[...] = jax.nn.sigmoid(_dot(u, w_ref[:, col:col + d])).astype(ga_ref.dtype)


def _inproj_call(x, shift, scale, w, bd, tka, tkb, tqa=None, tqb=None, *, tm):
    bsz, length, d = x.shape
    has_q = tqa is not None
    kvw = N_KV_HEADS * HEAD_DIM
    per_batch = shift.shape[0] > 1
    tab_rows = tka.shape[0]
    tab_blk = tm if tab_rows > 1 else 1
    mod_spec = pl.BlockSpec((None, 1, d), (lambda b, i: (b, 0, 0)) if per_batch else (lambda b, i: (0, 0, 0)))
    tab_spec = pl.BlockSpec((tab_blk, kvw), (lambda b, i: (i, 0)) if tab_rows > 1 else (lambda b, i: (0, 0)))
    row_spec = lambda width: pl.BlockSpec((None, tm, width), lambda b, i: (b, i, 0))
    in_specs = [row_spec(d), mod_spec, mod_spec, _const_spec(w.shape), _const_spec(bd.shape), tab_spec, tab_spec]
    args = [x, shift, scale, w, bd, tka, tkb]
    widths = [d, 4 * kvw, 4 * kvw]
    if has_q:
        in_specs += [tab_spec, tab_spec]
        args += [tqa, tqb]
        widths += [d, d, d]
    return pl.pallas_call(
        functools.partial(_inproj_kernel, d=d, kvw=kvw, has_q=has_q),
        out_shape=[jax.ShapeDtypeStruct((bsz, length, wd), BF16) for wd in widths],
        grid=(bsz, length // tm),
        in_specs=in_specs,
        out_specs=[row_spec(wd) for wd in widths],
        compiler_params=_params(("parallel", "parallel")),
        name="inproj_lat" if has_q else "inproj_ctx",
    )(*args)


def _rope_tables(seq, gain, scale):
    half = HEAD_DIM // 2
    inv_freq = ROPE_THETA ** (-jnp.arange(0, half, 2, dtype=F32) / half)
    t = jnp.arange(seq, dtype=jnp.int32)
    pos = jnp.stack([(t // GRID_W).astype(F32), (t % GRID_W).astype(F32)], axis=1)
    dim = jnp.arange(HEAD_DIM)
    axis = dim // half
    second = ((dim % half) // (half // 2)) == 1
    freq = inv_freq[dim % (half // 2)]
    ang = pos[:, axis] * freq[None, :]
    partner = jnp.where(second, dim - half // 2, dim + half // 2)
    g = gain.astype(F32)
    ta = jnp.cos(ang) * g[None, :] * scale
    tb = jnp.sin(ang) * jnp.where(second, 1.0, -1.0)[None, :] * g[partner][None, :] * scale
    return jnp.tile(ta, (1, N_KV_HEADS)), jnp.tile(tb, (1, N_KV_HEADS))


def _attn_kernel(q_ref, kc_ref, kl_ref, vc_ref, vl_ref, o_ref):
    nt = (((1,), (1,)), ((), ()))
    rows = q_ref.shape[0]
    lane = lax.broadcasted_iota(jnp.int32, (rows, LANES), 1)
    for j in range(q_ref.shape[1] // LANES):
        qs = q_ref[:, j * LANES:(j + 1) * LANES]
        acc = jnp.zeros((rows, LANES), F32)
        inv = []
        for half in range(2):
            cols = slice(half * LANES, (half + 1) * LANES)
            s_c = lax.dot_general(qs, kc_ref[:, cols], nt, preferred_element_type=F32)
            s_l = lax.dot_general(qs, kl_ref[:, cols], nt, preferred_element_type=F32)
            m = jnp.maximum(jnp.max(s_c, axis=-1, keepdims=True), jnp.max(s_l, axis=-1, keepdims=True))
            e_c = jnp.exp(s_c - m)
            e_l = jnp.exp(s_l - m)
            inv.append(1.0 / (jnp.sum(e_c, axis=-1, keepdims=True) + jnp.sum(e_l, axis=-1, keepdims=True)))
            acc = acc + _dot(e_c.astype(BF16), vc_ref[:, cols]) + _dot(e_l.astype(BF16), vl_ref[:, cols])
        o_ref[:, j * LANES:(j + 1) * LANES] = (acc * jnp.where(lane < HEAD_DIM, inv[0], inv[1])).astype(o_ref.dtype)


def _attn_call(q, kc, kl, vc, vl, *, tq):
    bsz, seq, d = q.shape
    ctx = kc.shape[1]
    gw = d // N_KV_HEADS
    q_spec = pl.BlockSpec((None, tq, gw), lambda b, h, i: (b, i, h))
    kv_spec = lambda length: pl.BlockSpec((None, length, 2 * LANES), lambda b, h, i: (b, 0, h))
    return pl.pallas_call(
        _attn_kernel,
        out_shape=jax.ShapeDtypeStruct((bsz, seq, d), BF16),
        grid=(bsz, N_KV_HEADS, seq // tq),
        in_specs=[q_spec, kv_spec(ctx), kv_spec(seq), kv_spec(ctx), kv_spec(seq)],
        out_specs=q_spec,
        compiler_params=_params(("parallel", "parallel", "arbitrary")),
        name="attention",
    )(q, kc, kl, vc, vl)


def _s5_tables(lam_re, lam_im, log_dt, b_re, b_im, c_re, c_im):
    n = S5_CHUNK
    lam = lax.complex(jnp.minimum(lam_re.astype(F32), S5_MAX_RE), lam_im.astype(F32))
    lam_dt = lam * jnp.exp(log_dt.astype(F32))[..., None]
    b_bar = ((jnp.exp(lam_dt) - 1.0) / lam)[..., None] * lax.complex(b_re.astype(F32), b_im.astype(F32))
    c_mat = lax.complex(c_re.astype(F32), c_im.astype(F32))
    pw = jnp.exp(lam_dt[None] * jnp.arange(n + 1, dtype=F32)[:, None, None, None])
    kern = jnp.real(jnp.einsum('dgcp,jdgp,dgpe->djgce', c_mat, pw[:n], b_bar))
    s_idx = jnp.arange(n)[:, None]
    t_idx = jnp.arange(n)[None, :]

    def toeplitz(k, lag):
        g = k[jnp.clip(lag, 0, n - 1)]
        g = jnp.where((lag >= 0)[:, :, None, None, None], g, 0.0)
        return g.transpose(2, 0, 4, 1, 3)

    toep = toeplitz(kern[0], t_idx - s_idx) + toeplitz(kern[1], s_idx - t_idx)
    groups = toep.shape[0]
    toep = toep.reshape(groups, S5_COLS, S5_COLS)
    wis_f = jnp.einsum('sgp,gpe->gsep', pw[n - 1 - jnp.arange(n), 0], b_bar[0]).reshape(groups, S5_COLS, -1)
    wis_r = jnp.einsum('sgp,gpe->gsep', pw[jnp.arange(n), 1], b_bar[1]).reshape(groups, S5_COLS, -1)
    w1 = jnp.concatenate([toep, jnp.real(wis_f), jnp.real(wis_r), jnp.imag(wis_f), jnp.imag(wis_r)], axis=-1)
    m_f = jnp.einsum('gcp,tgp->gptc', c_mat[0], pw[1 + jnp.arange(n), 0]).reshape(groups, -1, S5_COLS)
    m_r = jnp.einsum('gcp,tgp->gptc', c_mat[1], pw[n - jnp.arange(n), 1]).reshape(groups, -1, S5_COLS)
    wso = jnp.concatenate([jnp.real(m_f), jnp.real(m_r), -jnp.imag(m_f), -jnp.imag(m_r)], axis=1)
    lam_n = jnp.concatenate([pw[n, 0], pw[n, 1]], axis=-1)
    lam_tab = jnp.stack([jnp.real(lam_n), jnp.imag(lam_n)], axis=1)
    return w1.astype(BF16), wso.astype(BF16), lam_tab


def _s5_kernel(u_ref, w1_ref, wso_ref, lam_ref, y_ref, a_ref, xf_re, xf_im, xr_re, xr_im, *, nb, nc_ctx, nc):
    cols = S5_COLS
    half = lam_ref.shape[1] // 2
    a_ref[...] = _dot(u_ref[...], w1_ref[...])
    l_re = lam_ref[0:1, :]
    l_im = lam_ref[1:2, :]
    lo = lax.broadcasted_iota(jnp.int32, (nb, 2 * half), 1) < half

    def step(i, carry):
        x_re, x_im = carry
        i_rev = jnp.where(i < nc_ctx, nc_ctx - 1 - i, nc - 1 + nc_ctx - i)
        rf = pl.ds(pl.multiple_of(i * nb, nb), nb)
        rr = pl.ds(pl.multiple_of(i_rev * nb, nb), nb)
        xf_re[rf, :] = x_re
        xf_im[rf, :] = x_im
        xr_re[rr, :] = x_re
        xr_im[rr, :] = x_im
        s_re = jnp.where(lo, a_ref[rf, cols:cols + 2 * half], a_ref[rr, cols:cols + 2 * half])
        s_im = jnp.where(lo, a_ref[rf, cols + 2 * half:cols + 4 * half], a_ref[rr, cols + 2 * half:cols + 4 * half])
        return l_re * x_re - l_im * x_im + s_re, l_re * x_im + l_im * x_re + s_im

    zero = jnp.zeros((nb, 2 * half), F32)
    lax.fori_loop(0, nc, step, (zero, zero))
    r0 = nc_ctx * nb
    rows = (nc - nc_ctx) * nb
    lo_all = lax.broadcasted_iota(jnp.int32, (rows, 2 * half), 1) < half
    y = a_ref[r0:, 0:cols]
    for k, (f_ref, r_ref) in enumerate(((xf_re, xr_re), (xf_im, xr_im))):
        st = jnp.where(lo_all, f_ref[r0:, :], r_ref[r0:, :])
        hi = st.astype(BF16)
        lo_part = (st - hi.astype(F32)).astype(BF16)
        w = wso_ref[k * 2 * half:(k + 1) * 2 * half, :]
        y = y + _dot(hi, w) + _dot(lo_part, w)
    y_ref[...] = y


def _s5_call(u, w1, wso, lam_tab, *, nb, nc_ctx):
    groups, rows, cols = u.shape
    nc = rows // nb
    out_rows = (nc - nc_ctx) * nb
    st = lam_tab.shape[2]
    return pl.pallas_call(
        functools.partial(_s5_kernel, nb=nb, nc_ctx=nc_ctx, nc=nc),
        out_shape=jax.ShapeDtypeStruct((groups, out_rows, cols), F32),
        grid=(groups,),
        in_specs=[pl.BlockSpec((None, rows, cols), lambda g: (g, 0, 0)),
                  pl.BlockSpec((None,) + w1.shape[1:], lambda g: (g, 0, 0)),
                  pl.BlockSpec((None,) + wso.shape[1:], lambda g: (g, 0, 0)),
                  pl.BlockSpec((None, 2, st), lambda g: (g, 0, 0))],
        out_specs=pl.BlockSpec((None, out_rows, cols), lambda g: (g, 0, 0)),
        scratch_shapes=[pltpu.VMEM((rows, w1.shape[2]), F32)] + [pltpu.VMEM((rows, st), F32)] * 4,
        compiler_params=_params(("parallel",)),
        name="s5",
    )(u, w1, wso, lam_tab)


def _to_group_major(s_ctx, s_lat):
    s_all = jnp.concatenate([s_ctx, s_lat], axis=1)
    bsz, length, width = s_all.shape
    groups = width // S5_GROUP_CH
    nc = length // S5_CHUNK
    t = s_all.reshape(bsz, nc, S5_CHUNK, groups, S5_GROUP_CH).transpose(3, 1, 0, 2, 4)
    return t.reshape(groups, nc * bsz, S5_COLS)


def _to_token_major(y, bsz):
    groups, rows, _ = y.shape
    nc = rows // bsz
    t = y.reshape(groups, nc, bsz, S5_CHUNK, S5_GROUP_CH).transpose(2, 1, 3, 0, 4)
    return t.reshape(bsz, nc * S5_CHUNK, groups * S5_GROUP_CH)


def _mixout_kernel(y_ref, s_ref, at_ref, gs_ref, ga_ref, x_ref, g1_ref, sh2_ref, sc2_ref, dsk_ref,
                   wglu_ref, wbs_ref, wba_ref, wout_ref, lng_ref, lnb_ref, wrs_ref, wsd_ref,
                   x1_ref, h_ref, shared_ref, scores_ref, *, alpha, n_exp):
    y = y_ref[...] + s_ref[...].astype(F32) * dsk_ref[...]
    z = _gelu_tanh(y)
    ssm = z * jax.nn.sigmoid(_dot(z.astype(BF16), wglu_ref[...]))
    merged = (gs_ref[...].astype(F32) * _dot(ssm.astype(BF16), wbs_ref[...])
              + ga_ref[...].astype(F32) * _dot(at_ref[...], wba_ref[...]))
    y_mix = _dot(merged.astype(BF16), wout_ref[...])
    x1 = _norm_rows(alpha * x_ref[...] + g1_ref[...] * y_mix, LN_EPS) * lng_ref[...] + lnb_ref[...]
    x1_ref[...] = x1
    h = _norm_rows(x1, MOD_EPS) * (1.0 + sc2_ref[...]) + sh2_ref[...]
    h_ref[...] = h
    rs = _dot(h.astype(BF16), wrs_ref[...])
    scores_ref[...] = jax.nn.sigmoid(rs[:, :n_exp])
    sh_hidden = wsd_ref.shape[0]
    hid = _silu(rs[:, n_exp:n_exp + sh_hidden]) * rs[:, n_exp + sh_hidden:]
    shared_ref[...] = _dot(hid.astype(BF16), wsd_ref[...])


def _mixout_call(y, s, attn, gs, ga, x, g1, sh2, sc2, dsk, wglu, wbs, wba, wout, lng, lnb, wrs, wsd,
                 *, tm, alpha, n_exp):
    bsz, seq, d = x.shape
    row = lambda width: pl.BlockSpec((None, tm, width), lambda b, i: (b, i, 0))
    mod = pl.BlockSpec((None, 1, d), lambda b, i: (b, 0, 0))
    consts = [dsk, wglu, wbs, wba, wout, lng, lnb, wrs, wsd]
    return pl.pallas_call(
        functools.partial(_mixout_kernel, alpha=alpha, n_exp=n_exp),
        out_shape=[jax.ShapeDtypeStruct((bsz, seq, d), F32)] * 3 + [jax.ShapeDtypeStruct((bsz, seq, n_exp), F32)],
        grid=(bsz, seq // tm),
        in_specs=[row(d)] * 6 + [mod] * 3 + [_const_spec(a.shape) for a in consts],
        out_specs=[row(d)] * 3 + [row(n_exp)],
        compiler_params=_params(("parallel", "parallel")),
        name="mixout",
    )(y, s, attn, gs, ga, x, g1, sh2, sc2, *consts)


def _route(scores, router_bias, n_exp):
    tokens = scores.shape[0]
    per_group = n_exp // N_EXPERT_GROUPS
    biased = scores + router_bias.astype(F32)
    grouped = biased.reshape(tokens, N_EXPERT_GROUPS, per_group)
    group_score = jnp.sum(lax.top_k(grouped, 2)[0], axis=-1)
    _, top_groups = lax.top_k(group_score, TOPK_GROUPS)
    group_keep = jnp.sum(jax.nn.one_hot(top_groups, N_EXPERT_GROUPS, dtype=F32), axis=1) > 0
    expert_keep = jnp.repeat(group_keep, per_group, axis=1)
    _, idx = lax.top_k(jnp.where(expert_keep, biased, -jnp.inf), TOP_K)
    sel = jnp.take_along_axis(scores, idx, axis=-1)
    gates = sel / jnp.sum(sel, axis=-1, keepdims=True) * ROUTED_SCALE

    m = tokens * TOP_K
    bm = MOE_BLOCK
    flat_e = idx.reshape(m).astype(jnp.int32)
    order = jnp.argsort(flat_e).astype(jnp.int32)
    sorted_e = flat_e[order]
    counts = jnp.bincount(flat_e, length=n_exp).astype(jnp.int32)
    padded = (counts + bm - 1) // bm * bm
    pad_end = jnp.cumsum(padded)
    pad_start = pad_end - padded
    grp_start = jnp.cumsum(counts) - counts
    dest = pad_start[sorted_e] + jnp.arange(m, dtype=jnp.int32) - grp_start[sorted_e]
    n_blocks = (m + n_exp * (bm - 1) + bm - 1) // bm
    n_slots = n_blocks * bm
    slot_token = jnp.zeros((n_slots,), jnp.int32).at[dest].set(order // TOP_K)
    slot_gate = jnp.zeros((n_slots,), F32).at[dest].set(gates.reshape(m)[order])
    block_start = jnp.arange(n_blocks, dtype=jnp.int32) * bm
    block_expert = jnp.minimum(jnp.searchsorted(pad_end, block_start, side='right'), n_exp - 1).astype(jnp.int32)
    n_active = (pad_end[-1] // bm).astype(jnp.int32).reshape(1)
    pos = jnp.zeros((m,), jnp.int32).at[order].set(dest)
    return (slot_token.reshape(n_blocks, 1, bm), slot_gate.reshape(n_blocks, 1, bm), block_expert, n_active,
            pos.reshape(tokens, TOP_K))


def _row_gather_start(idx_ref, n_rows, src_hbm, dst_ref, sem):
    def body(r, carry):
        pltpu.make_async_copy(src_hbm.at[pl.ds(idx_ref[0, r], 1), :], dst_ref.at[pl.ds(r, 1), :], sem).start()
        return carry
    lax.fori_loop(0, n_rows, body, 0, unroll=8)


def _row_gather_wait(n_rows, src_hbm, dst_ref, sem):
    pltpu.make_async_copy(src_hbm.at[pl.ds(0, n_rows), :], dst_ref, sem).wait()


def _experts_kernel(be_ref, nact_ref, tok_ref, tok_next_ref, gate_ref, h_hbm, wg_ref, wu_ref, wd_ref,
                    y_ref, xbuf, sems):
    i = pl.program_id(0)
    nact = nact_ref[0]
    bm = xbuf.shape[1]
    slot = i % 2

    @pl.when(i == 0)
    def _():
        _row_gather_start(tok_ref, bm, h_hbm, xbuf.at[0], sems.at[0])

    @pl.when(i + 1 < nact)
    def _():
        _row_gather_start(tok_next_ref, bm, h_hbm, xbuf.at[1 - slot], sems.at[1 - slot])

    @pl.when(i < nact)
    def _():
        _row_gather_wait(bm, h_hbm, xbuf.at[slot], sems.at[slot])
        xb = xbuf[slot].astype(BF16)
        hid = _silu(_dot(xb, wg_ref[...])) * _dot(xb, wu_ref[...])
        r_id = lax.broadcasted_iota(jnp.int32, (bm, bm), 0)
        c_id = lax.broadcasted_iota(jnp.int32, (bm, bm), 1)
        gcol = jnp.sum(jnp.where(r_id == c_id, gate_ref[...], 0.0), axis=1, keepdims=True)
        y_ref[...] = _dot(hid.astype(BF16), wd_ref[...]) * gcol

    @pl.when(i >= nact)
    def _():
        y_ref[...] = jnp.zeros_like(y_ref)


def _experts_call(block_expert, n_active, slot_token, slot_gate, h, wg, wu, wd):
    n_blocks, _, bm = slot_token.shape
    tokens, d = h.shape
    hidden = wg.shape[2]
    last = n_blocks - 1
    grid_spec = pltpu.PrefetchScalarGridSpec(
        num_scalar_prefetch=2,
        grid=(n_blocks,),
        in_specs=[
            pl.BlockSpec((None, 1, bm), lambda i, be, na: (i, 0, 0), memory_space=pltpu.SMEM),
            pl.BlockSpec((None, 1, bm), lambda i, be, na: (jnp.minimum(i + 1, last), 0, 0), memory_space=pltpu.SMEM),
            pl.BlockSpec((None, 1, bm), lambda i, be, na: (i, 0, 0)),
            pl.BlockSpec(memory_space=pl.ANY),
            pl.BlockSpec((None, d, hidden), lambda i, be, na: (be[i], 0, 0)),
            pl.BlockSpec((None, d, hidden), lambda i, be, na: (be[i], 0, 0)),
            pl.BlockSpec((None, hidden, d), lambda i, be, na: (be[i], 0, 0)),
        ],
        out_specs=pl.BlockSpec((bm, d), lambda i, be, na: (i, 0)),
        scratch_shapes=[pltpu.VMEM((2, bm, d), F32), pltpu.SemaphoreType.DMA((2,))],
    )
    return pl.pallas_call(
        _experts_kernel,
        out_shape=jax.ShapeDtypeStruct((n_blocks * bm, d), F32),
        grid_spec=grid_spec,
        compiler_params=_params(("arbitrary",)),
        name="experts",
    )(block_expert, n_active, slot_token, slot_token, slot_gate, h, wg, wu, wd)


def _combine_kernel(pos_ref, pos_next_ref, y_hbm, shared_ref, x1_ref, g2_ref, lng_ref, lnb_ref, o_ref,
                    buf, sems, *, alpha):
    i = pl.program_id(0)
    n = pl.num_programs(0)
    rows = buf.shape[1]
    tc = o_ref.shape[0]
    slot = i % 2

    @pl.when(i == 0)
    def _():
        _row_gather_start(pos_ref, rows, y_hbm, buf.at[0], sems.at[0])

    @pl.when(i + 1 < n)
    def _():
        _row_gather_start(pos_next_ref, rows, y_hbm, buf.at[1 - slot], sems.at[1 - slot])

    _row_gather_wait(rows, y_hbm, buf.at[slot], sems.at[slot])
    f = shared_ref[...]
    for k in range(rows // tc):
        f = f + buf[slot, k * tc:(k + 1) * tc, :]
    o_ref[...] = _norm_rows(alpha * x1_ref[...] + g2_ref[...] * f, LN_EPS) * lng_ref[...] + lnb_ref[...]


def _combine_call(pos_tiles, y_slots, shared, x1, g2, lng, lnb, *, alpha, tiles_per_batch):
    n_tiles, _, rows = pos_tiles.shape
    tokens, d = x1.shape
    tc = tokens // n_tiles
    last = n_tiles - 1
    row = pl.BlockSpec((tc, d), lambda i: (i, 0))
    return pl.pallas_call(
        functools.partial(_combine_kernel, alpha=alpha),
        out_shape=jax.ShapeDtypeStruct((tokens, d), F32),
        grid=(n_tiles,),
        in_specs=[
            pl.BlockSpec((None, 1, rows), lambda i: (i, 0, 0), memory_space=pltpu.SMEM),
            pl.BlockSpec((None, 1, rows), lambda i: (jnp.minimum(i + 1, last), 0, 0), memory_space=pltpu.SMEM),
            pl.BlockSpec(memory_space=pl.ANY),
            row, row,
            pl.BlockSpec((None, 1, d), lambda i: (i // tiles_per_batch, 0, 0)),
            _const_spec(lng.shape), _const_spec(lnb.shape),
        ],
        out_specs=row,
        scratch_shapes=[pltpu.VMEM((2, rows, d), F32), pltpu.SemaphoreType.DMA((2,))],
        compiler_params=_params(("arbitrary",)),
        name="combine",
    )(pos_tiles, pos_tiles, y_slots, shared, x1, g2, lng, lnb)


def kernel(x, c, ctx, c_ctx, w_mod, b_mod, w_in, s5_lam_re, s5_lam_im, s5_log_dt, s5_b_re, s5_b_im, s5_c_re, s5_c_im, s5_d, w_glu, q_norm_g, k_norm_g, w_branch_ssm, w_branch_attn, w_out, ln1_g, ln1_b, w_router, router_bias, w_exp_gate, w_exp_up, w_exp_down, w_sh_gate, w_sh_up, w_sh_down, ln2_g, ln2_b):
    depth = w_mod.shape[0]
    assert depth == 1, "single-layer block: context outputs are never needed"
    bsz, seq, d = x.shape
    ctx_len = ctx.shape[1]
    kvw = N_KV_HEADS * HEAD_DIM
    n_exp = w_router.shape[2]
    alpha = (2.0 * depth) ** 0.25
    assert seq % GRID_W == 0 and seq % S5_CHUNK == 0 and ctx_len % S5_CHUNK == 0 and bsz % SUBLANES == 0
    lay = 0

    pad = (-(bsz + 1)) % SUBLANES
    c_all = jnp.concatenate([c, c_ctx[None, :], jnp.zeros((pad, d), F32)], axis=0)
    mod = _mod_call(c_all, w_mod[lay], b_mod[lay])
    mod_lat = mod[:bsz].reshape(bsz, N_MOD, 1, d)
    sh1, sc1, g1, sh2, sc2, g2 = (mod_lat[:, k] for k in range(N_MOD))
    mod_ctx = mod[bsz].reshape(N_MOD, 1, 1, d)

    w_in_l = w_in[lay].astype(BF16)
    w_ctx = jnp.concatenate([w_in_l[:, :d], w_in_l[:, 2 * d:2 * d + 2 * kvw]], axis=1)
    head_id = jnp.arange(kvw) // HEAD_DIM
    bd = jnp.where(head_id[:, None] == head_id[None, :], 1.0 / HEAD_DIM, 0.0).astype(BF16)
    tqa, tqb = _rope_tables(seq, q_norm_g[lay], HEAD_DIM ** -0.5)
    tka, tkb = _rope_tables(seq, k_norm_g[lay], 1.0)
    tca = jnp.tile(k_norm_g[lay].astype(F32), N_KV_HEADS)[None, :]
    tm_lat = min(512, seq)
    s_lat, k_lat, v_lat, q_lat, gs_lat, ga_lat = _inproj_call(x, sh1, sc1, w_in_l, bd, tka, tkb, tqa, tqb, tm=tm_lat)
    s_ctx, k_ctx, v_ctx = _inproj_call(ctx, mod_ctx[0], mod_ctx[1], w_ctx, bd, tca, jnp.zeros_like(tca),
                                       tm=min(256, ctx_len))

    w1, wso, lam_tab = _s5_tables(s5_lam_re[lay], s5_lam_im[lay], s5_log_dt[lay], s5_b_re[lay], s5_b_im[lay],
                                  s5_c_re[lay], s5_c_im[lay])
    y_groups = _s5_call(_to_group_major(s_ctx, s_lat), w1, wso, lam_tab, nb=bsz, nc_ctx=ctx_len // S5_CHUNK)
    y_s5 = _to_token_major(y_groups, bsz)

    attn = _attn_call(q_lat, k_ctx, k_lat, v_ctx, v_lat, tq=min(256, seq))

    row = lambda v: v.astype(F32).reshape(1, -1)
    wrs = jnp.concatenate([w_router[lay], w_sh_gate[lay], w_sh_up[lay]], axis=1).astype(BF16)
    x1, h, shared, scores = _mixout_call(
        y_s5, s_lat, attn, gs_lat, ga_lat, x, g1, sh2, sc2, row(s5_d[lay]),
        w_glu[lay].astype(BF16), w_branch_ssm[lay].astype(BF16), w_branch_attn[lay].astype(BF16),
        w_out[lay].astype(BF16), row(ln1_g[lay]), row(ln1_b[lay]), wrs, w_sh_down[lay].astype(BF16),
        tm=min(256, seq), alpha=alpha, n_exp=n_exp)

    tokens = bsz * seq
    slot_token, slot_gate, block_expert, n_active, pos = _route(scores.reshape(tokens, n_exp), router_bias[lay], n_exp)
    y_slots = _experts_call(block_expert, n_active, slot_token, slot_gate, h.reshape(tokens, d),
                            w_exp_gate[lay].astype(BF16), w_exp_up[lay].astype(BF16), w_exp_down[lay].astype(BF16))

    tc = min(COMBINE_TOKENS, seq)
    n_tiles = tokens // tc
    pos_tiles = pos.reshape(n_tiles, tc, TOP_K).transpose(0, 2, 1).reshape(n_tiles, 1, TOP_K * tc)
    out = _combine_call(pos_tiles, y_slots, shared.reshape(tokens, d), x1.reshape(tokens, d), g2,
                        row(ln2_g[lay]), row(ln2_b[lay]), alpha=alpha, tiles_per_batch=seq // tc)
    return out.reshape(bsz, seq, d)
```

```python
import functools
import math

import jax
import jax.numpy as jnp
from jax import lax
from jax.experimental import pallas as pl
from jax.experimental.pallas import tpu as pltpu

F32 = jnp.float32
BF16 = jnp.bfloat16

GRID_W = 64
HEAD_DIM = 64
N_KV_HEADS = 4
S5_GROUP_CH = 16
S5_MAX_RE = -1e-4
ROPE_THETA = 10000.0
TOP_K = 8
N_EXPERT_GROUPS = 8
TOPK_GROUPS = 4
ROUTED_SCALE = 2.5
LN_EPS = 1e-5
MOD_EPS = 1e-6
RMS_EPS = 1e-6
N_MOD = 6

LANES = 128
SUBLANES = 8
VMEM_LIMIT_BYTES = 56 * 1024 * 1024

S5_CHUNK = 16
S5_COLS = S5_CHUNK * S5_GROUP_CH
MOE_BLOCK = 256
DISPATCH_TOKENS = 64
COMBINE_TOKENS = 64


def _params(sem):
    return pltpu.CompilerParams(dimension_semantics=sem, vmem_limit_bytes=VMEM_LIMIT_BYTES)


def _const_spec(shape):
    nd = len(shape)
    return pl.BlockSpec(shape, lambda *_: (0,) * nd, pipeline_mode=pl.Buffered(1))


def _dot(a, b):
    return jnp.dot(a, b, preferred_element_type=F32)


def _norm_rows(x, eps):
    mu = jnp.mean(x, axis=-1, keepdims=True)
    xc = x - mu
    var = jnp.mean(xc * xc, axis=-1, keepdims=True)
    return xc * lax.rsqrt(var + eps)


def _silu(x):
    return x * jax.nn.sigmoid(x)


def _gelu_tanh(x):
    return 0.5 * x * (1.0 + jnp.tanh(math.sqrt(2.0 / math.pi) * (x + 0.044715 * (x * x * x))))


def _pack_bf16_pairs(v):
    n = v.shape[1] // 2
    bits = lax.bitcast_convert_type(v.astype(BF16).astype(F32), jnp.uint32)
    return bits[:, :n] | (bits[:, n:] >> 16)


def _unpack_bf16_pairs(w):
    hi = lax.bitcast_convert_type(w & jnp.uint32(0xFFFF0000), F32)
    lo = lax.bitcast_convert_type(w << 16, F32)
    return hi, lo


def _mod_kernel(c_ref, w_ref, b_ref, o_ref):
    o_ref[...] = _dot(_silu(c_ref[...]), w_ref[...]) + b_ref[...]


def _mod_call(c_all, w_mod, b_mod):
    rows, d = c_all.shape
    n = w_mod.shape[1]
    tn = d
    return pl.pallas_call(
        _mod_kernel,
        out_shape=jax.ShapeDtypeStruct((rows, n), F32),
        grid=(n // tn,),
        in_specs=[pl.BlockSpec((rows, d), lambda j: (0, 0)),
                  pl.BlockSpec((d, tn), lambda j: (0, j)),
                  pl.BlockSpec((1, tn), lambda j: (0, j))],
        out_specs=pl.BlockSpec((rows, tn), lambda j: (0, j)),
        compiler_params=_params(("arbitrary",)),
        name="mod",
    )(c_all, w_mod, b_mod.reshape(1, n))


def _swap16(t):
    width = t.shape[1]
    lane = lax.broadcasted_iota(jnp.int32, t.shape, 1)
    first = (lane & 16) == 0
    return jnp.where(first, pltpu.roll(t, width - 16, 1), pltpu.roll(t, 16, 1))


def _rms_rope(t, bd_ref, ta, tb):
    msq = _dot((t * t).astype(BF16), bd_ref[...])
    return lax.rsqrt(msq + RMS_EPS) * (t * ta + _swap16(t) * tb)


def _store_padded_heads(t, o_ref):
    rows = t.shape[0]
    lane = lax.broadcasted_iota(jnp.int32, (rows, LANES), 1)
    lo = lane < HEAD_DIM
    for j in range(t.shape[1] // LANES):
        slab = t[:, j * LANES:(j + 1) * LANES]
        a = jnp.where(lo, slab, 0.0)
        b = jnp.where(lo, 0.0, slab)
        pieces = (a, pltpu.roll(a, HEAD_DIM, 1), pltpu.roll(b, HEAD_DIM, 1), b)
        for p, piece in enumerate(pieces):
            c0 = (4 * j + p) * LANES
            o_ref[:, c0:c0 + LANES] = piece.astype(o_ref.dtype)


def _inproj_kernel(x_ref, sh_ref, sc_ref, w_ref, bd_ref, tka_ref, tkb_ref, *rest, d, kvw, has_q):
    if has_q:
        tqa_ref, tqb_ref, s_ref, k_ref, v_ref, q_ref, gs_ref, ga_ref = rest
    else:
        s_ref, k_ref, v_ref = rest
    u = (_norm_rows(x_ref[...], MOD_EPS) * (1.0 + sc_ref[...]) + sh_ref[...]).astype(BF16)
    col = 0
    s_ref[...] = _dot(u, w_ref[:, col:col + d]).astype(s_ref.dtype)
    col += d
    if has_q:
        for c in range(d // kvw):
            q = _dot(u, w_ref[:, col + c * kvw:col + (c + 1) * kvw])
            q_ref[:, c * kvw:(c + 1) * kvw] = _rms_rope(q, bd_ref, tqa_ref[...], tqb_ref[...]).astype(q_ref.dtype)
        col += d
    k = _dot(u, w_ref[:, col:col + kvw])
    _store_padded_heads(_rms_rope(k, bd_ref, tka_ref[...], tkb_ref[...]), k_ref)
    col += kvw
    _store_padded_heads(_dot(u, w_ref[:, col:col + kvw]), v_ref)
    col += kvw
    if has_q:
        gs_ref[...] = jax.nn.sigmoid(_dot(u, w_ref[:, col:col + d])).astype(gs_ref.dtype)
        col += d
        ga_ref[...] = jax.nn.sigmoid(_dot(u, w_ref[:, col:col + d])).astype(ga_ref.dtype)


def _inproj_call(x, shift, scale, w, bd, tka, tkb, tqa=None, tqb=None, *, tm):
    bsz, length, d = x.shape
    has_q = tqa is not None
    kvw = N_KV_HEADS * HEAD_DIM
    per_batch = shift.shape[0] > 1
    tab_rows = tka.shape[0]
    tab_blk = tm if tab_rows > 1 else 1
    mod_spec = pl.BlockSpec((None, 1, d), (lambda b, i: (b, 0, 0)) if per_batch else (lambda b, i: (0, 0, 0)))
    tab_spec = pl.BlockSpec((tab_blk, kvw), (lambda b, i: (i, 0)) if tab_rows > 1 else (lambda b, i: (0, 0)))
    row_spec = lambda width: pl.BlockSpec((None, tm, width), lambda b, i: (b, i, 0))
    in_specs = [row_spec(d), mod_spec, mod_spec, _const_spec(w.shape), _const_spec(bd.shape), tab_spec, tab_spec]
    args = [x, shift, scale, w, bd, tka, tkb]
    widths = [d, 4 * kvw, 4 * kvw]
    if has_q:
        in_specs += [tab_spec, tab_spec]
        args += [tqa, tqb]
        widths += [d, d, d]
    return pl.pallas_call(
        functools.partial(_inproj_kernel, d=d, kvw=kvw, has_q=has_q),
        out_shape=[jax.ShapeDtypeStruct((bsz, length, wd), BF16) for wd in widths],
        grid=(bsz, length // tm),
        in_specs=in_specs,
        out_specs=[row_spec(wd) for wd in widths],
        compiler_params=_params(("parallel", "parallel")),
        name="inproj_lat" if has_q else "inproj_ctx",
    )(*args)


def _rope_tables(seq, gain, scale):
    half = HEAD_DIM // 2
    inv_freq = ROPE_THETA ** (-jnp.arange(0, half, 2, dtype=F32) / half)
    t = jnp.arange(seq, dtype=jnp.int32)
    pos = jnp.stack([(t // GRID_W).astype(F32), (t % GRID_W).astype(F32)], axis=1)
    dim = jnp.arange(HEAD_DIM)
    axis = dim // half
    second = ((dim % half) // (half // 2)) == 1
    freq = inv_freq[dim % (half // 2)]
    ang = pos[:, axis] * freq[None, :]
    partner = jnp.where(second, dim - half // 2, dim + half // 2)
    g = gain.astype(F32)
    ta = jnp.cos(ang) * g[None, :] * scale
    tb = jnp.sin(ang) * jnp.where(second, 1.0, -1.0)[None, :] * g[partner][None, :] * scale
    return jnp.tile(ta, (1, N_KV_HEADS)), jnp.tile(tb, (1, N_KV_HEADS))


def _attn_kernel(q_ref, kc_ref, kl_ref, vc_ref, vl_ref, o_ref):
    nt = (((1,), (1,)), ((), ()))
    rows = q_ref.shape[0]
    lane = lax.broadcasted_iota(jnp.int32, (rows, LANES), 1)
    for j in range(q_ref.shape[1] // LANES):
        qs = q_ref[:, j * LANES:(j + 1) * LANES]
        acc = jnp.zeros((rows, LANES), F32)
        inv = []
        for half in range(2):
            cols = slice(half * LANES, (half + 1) * LANES)
            s_c = lax.dot_general(qs, kc_ref[:, cols], nt, preferred_element_type=F32)
            s_l = lax.dot_general(qs, kl_ref[:, cols], nt, preferred_element_type=F32)
            m = jnp.maximum(jnp.max(s_c, axis=-1, keepdims=True), jnp.max(s_l, axis=-1, keepdims=True))
            e_c = jnp.exp(s_c - m)
            e_l = jnp.exp(s_l - m)
            inv.append(1.0 / (jnp.sum(e_c, axis=-1, keepdims=True) + jnp.sum(e_l, axis=-1, keepdims=True)))
            acc = acc + _dot(e_c.astype(BF16), vc_ref[:, cols]) + _dot(e_l.astype(BF16), vl_ref[:, cols])
        o_ref[:, j * LANES:(j + 1) * LANES] = (acc * jnp.where(lane < HEAD_DIM, inv[0], inv[1])).astype(o_ref.dtype)


def _attn_call(q, kc, kl, vc, vl, *, tq):
    bsz, seq, d = q.shape
    ctx = kc.shape[1]
    gw = d // N_KV_HEADS
    q_spec = pl.BlockSpec((None, tq, gw), lambda b, h, i: (b, i, h))
    kv_spec = lambda length: pl.BlockSpec((None, length, 2 * LANES), lambda b, h, i: (b, 0, h))
    return pl.pallas_call(
        _attn_kernel,
        out_shape=jax.ShapeDtypeStruct((bsz, seq, d), BF16),
        grid=(bsz, N_KV_HEADS, seq // tq),
        in_specs=[q_spec, kv_spec(ctx), kv_spec(seq), kv_spec(ctx), kv_spec(seq)],
        out_specs=q_spec,
        compiler_params=_params(("parallel", "parallel", "arbitrary")),
        name="attention",
    )(q, kc, kl, vc, vl)


def _s5_tables(lam_re, lam_im, log_dt, b_re, b_im, c_re, c_im):
    n = S5_CHUNK
    lam = lax.complex(jnp.minimum(lam_re.astype(F32), S5_MAX_RE), lam_im.astype(F32))
    lam_dt = lam * jnp.exp(log_dt.astype(F32))[..., None]
    b_bar = ((jnp.exp(lam_dt) - 1.0) / lam)[..., None] * lax.complex(b_re.astype(F32), b_im.astype(F32))
    c_mat = lax.complex(c_re.astype(F32), c_im.astype(F32))
    pw = jnp.exp(lam_dt[None] * jnp.arange(n + 1, dtype=F32)[:, None, None, None])
    kern = jnp.real(jnp.einsum('dgcp,jdgp,dgpe->djgce', c_mat, pw[:n], b_bar))
    s_idx = jnp.arange(n)[:, None]
    t_idx = jnp.arange(n)[None, :]

    def toeplitz(k, lag):
        g = k[jnp.clip(lag, 0, n - 1)]
        g = jnp.where((lag >= 0)[:, :, None, None, None], g, 0.0)
        return g.transpose(2, 0, 4, 1, 3)

    toep = toeplitz(kern[0], t_idx - s_idx) + toeplitz(kern[1], s_idx - t_idx)
    groups = toep.shape[0]
    toep = toep.reshape(groups, S5_COLS, S5_COLS)
    wis_f = jnp.einsum('sgp,gpe->gsep', pw[n - 1 - jnp.arange(n), 0], b_bar[0]).reshape(groups, S5_COLS, -1)
    wis_r = jnp.einsum('sgp,gpe->gsep', pw[jnp.arange(n), 1], b_bar[1]).reshape(groups, S5_COLS, -1)
    w1 = jnp.concatenate([toep, jnp.real(wis_f), jnp.real(wis_r), jnp.imag(wis_f), jnp.imag(wis_r)], axis=-1)
    m_f = jnp.einsum('gcp,tgp->gptc', c_mat[0], pw[1 + jnp.arange(n), 0]).reshape(groups, -1, S5_COLS)
    m_r = jnp.einsum('gcp,tgp->gptc', c_mat[1], pw[n - jnp.arange(n), 1]).reshape(groups, -1, S5_COLS)
    wso = jnp.concatenate([jnp.real(m_f), jnp.real(m_r), -jnp.imag(m_f), -jnp.imag(m_r)], axis=1)
    lam_n = jnp.concatenate([pw[n, 0], pw[n, 1]], axis=-1)
    lam_tab = jnp.stack([jnp.real(lam_n), jnp.imag(lam_n)], axis=1)
    return w1.astype(BF16), wso.astype(BF16), lam_tab


def _s5_kernel(u_ref, w1_ref, wso_ref, lam_ref, y_ref, a_ref, xf_re, xf_im, xr_re, xr_im, *, nb, nc_ctx, nc):
    cols = S5_COLS
    half = lam_ref.shape[1] // 2
    a_ref[...] = _dot(u_ref[...], w1_ref[...])
    l_re = lam_ref[0:1, :]
    l_im = lam_ref[1:2, :]
    lo = lax.broadcasted_iota(jnp.int32, (nb, 2 * half), 1) < half

    def step(i, carry):
        x_re, x_im = carry
        i_rev = jnp.where(i < nc_ctx, nc_ctx - 1 - i, nc - 1 + nc_ctx - i)
        rf = pl.ds(pl.multiple_of(i * nb, nb), nb)
        rr = pl.ds(pl.multiple_of(i_rev * nb, nb), nb)
        xf_re[rf, :] = x_re
        xf_im[rf, :] = x_im
        xr_re[rr, :] = x_re
        xr_im[rr, :] = x_im
        s_re = jnp.where(lo, a_ref[rf, cols:cols + 2 * half], a_ref[rr, cols:cols + 2 * half])
        s_im = jnp.where(lo, a_ref[rf, cols + 2 * half:cols + 4 * half], a_ref[rr, cols + 2 * half:cols + 4 * half])
        return l_re * x_re - l_im * x_im + s_re, l_re * x_im + l_im * x_re + s_im

    zero = jnp.zeros((nb, 2 * half), F32)
    lax.fori_loop(0, nc, step, (zero, zero))
    r0 = nc_ctx * nb
    rows = (nc - nc_ctx) * nb
    lo_all = lax.broadcasted_iota(jnp.int32, (rows, 2 * half), 1) < half
    y = a_ref[r0:, 0:cols]
    for k, (f_ref, r_ref) in enumerate(((xf_re, xr_re), (xf_im, xr_im))):
        st = jnp.where(lo_all, f_ref[r0:, :], r_ref[r0:, :])
        hi = st.astype(BF16)
        lo_part = (st - hi.astype(F32)).astype(BF16)
        w = wso_ref[k * 2 * half:(k + 1) * 2 * half, :]
        y = y + _dot(hi, w) + _dot(lo_part, w)
    y_ref[...] = y


def _s5_call(u, w1, wso, lam_tab, *, nb, nc_ctx):
    groups, rows, cols = u.shape
    nc = rows // nb
    out_rows = (nc - nc_ctx) * nb
    st = lam_tab.shape[2]
    return pl.pallas_call(
        functools.partial(_s5_kernel, nb=nb, nc_ctx=nc_ctx, nc=nc),
        out_shape=jax.ShapeDtypeStruct((groups, out_rows, cols), F32),
        grid=(groups,),
        in_specs=[pl.BlockSpec((None, rows, cols), lambda g: (g, 0, 0)),
                  pl.BlockSpec((None,) + w1.shape[1:], lambda g: (g, 0, 0)),
                  pl.BlockSpec((None,) + wso.shape[1:], lambda g: (g, 0, 0)),
                  pl.BlockSpec((None, 2, st), lambda g: (g, 0, 0))],
        out_specs=pl.BlockSpec((None, out_rows, cols), lambda g: (g, 0, 0)),
        scratch_shapes=[pltpu.VMEM((rows, w1.shape[2]), F32)] + [pltpu.VMEM((rows, st), F32)] * 4,
        compiler_params=_params(("parallel",)),
        name="s5",
    )(u, w1, wso, lam_tab)


def _to_group_major(s_ctx, s_lat):
    s_all = jnp.concatenate([s_ctx, s_lat], axis=1)
    bsz, length, width = s_all.shape
    groups = width // S5_GROUP_CH
    nc = length // S5_CHUNK
    t = s_all.reshape(bsz, nc, S5_CHUNK, groups, S5_GROUP_CH).transpose(3, 1, 0, 2, 4)
    return t.reshape(groups, nc * bsz, S5_COLS)


def _to_token_major(y, bsz):
    groups, rows, _ = y.shape
    nc = rows // bsz
    t = y.reshape(groups, nc, bsz, S5_CHUNK, S5_GROUP_CH).transpose(2, 1, 3, 0, 4)
    return t.reshape(bsz, nc * S5_CHUNK, groups * S5_GROUP_CH)


def _mixout_kernel(y_ref, s_ref, at_ref, gs_ref, ga_ref, x_ref, g1_ref, sh2_ref, sc2_ref, dsk_ref,
                   wglu_ref, wbs_ref, wba_ref, wout_ref, lng_ref, lnb_ref, wrs_ref, wsd_ref,
                   x1_ref, h_ref, shared_ref, scores_ref, *, alpha, n_exp):
    y = y_ref[...] + s_ref[...].astype(F32) * dsk_ref[...]
    z = _gelu_tanh(y)
    ssm = z * jax.nn.sigmoid(_dot(z.astype(BF16), wglu_ref[...]))
    merged = (gs_ref[...].astype(F32) * _dot(ssm.astype(BF16), wbs_ref[...])
              + ga_ref[...].astype(F32) * _dot(at_ref[...], wba_ref[...]))
    y_mix = _dot(merged.astype(BF16), wout_ref[...])
    x1 = _norm_rows(alpha * x_ref[...] + g1_ref[...] * y_mix, LN_EPS) * lng_ref[...] + lnb_ref[...]
    x1_ref[...] = x1
    h = _norm_rows(x1, MOD_EPS) * (1.0 + sc2_ref[...]) + sh2_ref[...]
    h_ref[...] = _pack_bf16_pairs(h)
    rs = _dot(h.astype(BF16), wrs_ref[...])
    scores_ref[...] = jax.nn.sigmoid(rs[:, :n_exp])
    sh_hidden = wsd_ref.shape[0]
    hid = _silu(rs[:, n_exp:n_exp + sh_hidden]) * rs[:, n_exp + sh_hidden:]
    shared_ref[...] = _dot(hid.astype(BF16), wsd_ref[...])


def _mixout_call(y, s, attn, gs, ga, x, g1, sh2, sc2, dsk, wglu, wbs, wba, wout, lng, lnb, wrs, wsd,
                 *, tm, alpha, n_exp):
    bsz, seq, d = x.shape
    row = lambda width: pl.BlockSpec((None, tm, width), lambda b, i: (b, i, 0))
    mod = pl.BlockSpec((None, 1, d), lambda b, i: (b, 0, 0))
    consts = [dsk, wglu, wbs, wba, wout, lng, lnb, wrs, wsd]
    return pl.pallas_call(
        functools.partial(_mixout_kernel, alpha=alpha, n_exp=n_exp),
        out_shape=[jax.ShapeDtypeStruct((bsz, seq, d), F32), jax.ShapeDtypeStruct((bsz, seq, d // 2), jnp.uint32),
                   jax.ShapeDtypeStruct((bsz, seq, d), F32), jax.ShapeDtypeStruct((bsz, seq, n_exp), F32)],
        grid=(bsz, seq // tm),
        in_specs=[row(d)] * 6 + [mod] * 3 + [_const_spec(a.shape) for a in consts],
        out_specs=[row(d), row(d // 2), row(d), row(n_exp)],
        compiler_params=_params(("parallel", "parallel")),
        name="mixout",
    )(y, s, attn, gs, ga, x, g1, sh2, sc2, *consts)


def _segment_allreduce(x, lane, width, op):
    total = x.shape[1]
    d = 1
    while d < width:
        partner = jnp.where((lane & d) == 0, pltpu.roll(x, total - d, 1), pltpu.roll(x, d, 1))
        x = op(x, partner)
        d *= 2
    return x


def _route_kernel(s_ref, bias_ref, ltri_ref, idx_ref, gate_ref, rank_ref, counts_ref, base_ref):
    @pl.when(pl.program_id(0) == 0)
    def _():
        base_ref[...] = jnp.zeros_like(base_ref)

    s = s_ref[...]
    tm, n_exp = s.shape
    per_group = n_exp // N_EXPERT_GROUPS
    neg = -jnp.inf
    b = s + bias_ref[...]
    lane = lax.broadcasted_iota(jnp.int32, (tm, n_exp), 1)
    lane_f = lane.astype(F32)
    m1 = _segment_allreduce(b, lane, per_group, jnp.maximum)
    is1 = b == m1
    n1 = _segment_allreduce(jnp.where(is1, 1.0, 0.0), lane, per_group, jnp.add)
    m2 = _segment_allreduce(jnp.where(is1, neg, b), lane, per_group, jnp.maximum)
    gscore = m1 + jnp.where(n1 >= 2.0, m1, m2)
    gid = lane // per_group
    beaten = jnp.zeros((tm, n_exp), F32)
    for j in range(1, N_EXPERT_GROUPS):
        other = pltpu.roll(gscore, j * per_group, 1)
        oid = (gid - j) & (N_EXPERT_GROUPS - 1)
        beaten = beaten + jnp.where((other > gscore) | ((other == gscore) & (oid < gid)), 1.0, 0.0)
    masked = jnp.where(beaten < float(TOPK_GROUPS), b, neg)
    lane_k = lax.broadcasted_iota(jnp.int32, (tm, TOP_K), 1)
    idx = jnp.zeros((tm, TOP_K), F32)
    val = jnp.zeros((tm, TOP_K), F32)
    chosen = jnp.zeros((tm, n_exp), F32)
    for k in range(TOP_K):
        m = jnp.max(masked, axis=1, keepdims=True)
        first = jnp.min(jnp.where(masked == m, lane_f, float(n_exp)), axis=1, keepdims=True)
        sel = lane_f == first
        idx = jnp.where(lane_k == k, first, idx)
        val = jnp.where(lane_k == k, jnp.sum(jnp.where(sel, s, 0.0), axis=1, keepdims=True), val)
        chosen = jnp.where(sel, 1.0, chosen)
        masked = jnp.where(sel, neg, masked)
    before = _dot(ltri_ref[...], chosen.astype(BF16)) + base_ref[...]
    rank = jnp.zeros((tm, TOP_K), F32)
    for k in range(TOP_K):
        sel = lane_f == idx[:, k:k + 1]
        rank = jnp.where(lane_k == k, jnp.sum(jnp.where(sel, before, 0.0), axis=1, keepdims=True), rank)
    idx_ref[...] = idx.astype(jnp.int32)
    rank_ref[...] = rank.astype(jnp.int32)
    gate_ref[...] = val / jnp.sum(val, axis=1, keepdims=True) * ROUTED_SCALE
    base_ref[...] = base_ref[...] + jnp.sum(chosen, axis=0, keepdims=True)
    counts_ref[...] = base_ref[...]


def _route_call(scores, bias, *, tm):
    tokens, n_exp = scores.shape
    r_id = lax.broadcasted_iota(jnp.int32, (tm, tm), 0)
    c_id = lax.broadcasted_iota(jnp.int32, (tm, tm), 1)
    ltri = (c_id < r_id).astype(BF16)
    small = pl.BlockSpec((tm, TOP_K), lambda i: (i, 0))
    return pl.pallas_call(
        _route_kernel,
        out_shape=[jax.ShapeDtypeStruct((tokens, TOP_K), jnp.int32), jax.ShapeDtypeStruct((tokens, TOP_K), F32),
                   jax.ShapeDtypeStruct((tokens, TOP_K), jnp.int32), jax.ShapeDtypeStruct((1, n_exp), F32)],
        grid=(tokens // tm,),
        in_specs=[pl.BlockSpec((tm, n_exp), lambda i: (i, 0)), _const_spec((1, n_exp)), _const_spec((tm, tm))],
        out_specs=[small, small, small, pl.BlockSpec((1, n_exp), lambda i: (0, 0))],
        scratch_shapes=[pltpu.VMEM((1, n_exp), F32)],
        compiler_params=_params(("arbitrary",)),
        name="route",
    )(scores, bias.astype(F32).reshape(1, n_exp), ltri)


def _slot_kernel(idx_ref, rank_ref, start_ref, pos_ref):
    idx = idx_ref[...]
    tm = idx.shape[0]
    n_exp = start_ref.shape[1]
    lane = lax.broadcasted_iota(jnp.int32, (tm, n_exp), 1)
    lane_k = lax.broadcasted_iota(jnp.int32, (tm, TOP_K), 1)
    pos = rank_ref[...]
    for k in range(TOP_K):
        start = jnp.sum(jnp.where(lane == idx[:, k:k + 1], start_ref[...], 0.0), axis=1, keepdims=True)
        pos = pos + jnp.where(lane_k == k, start.astype(jnp.int32), 0)
    pos_ref[...] = pos


def _slot_call(idx, rank, pad_start, *, tm):
    tokens = idx.shape[0]
    n_exp = pad_start.shape[0]
    small = pl.BlockSpec((tm, TOP_K), lambda i: (i, 0))
    return pl.pallas_call(
        _slot_kernel,
        out_shape=jax.ShapeDtypeStruct((tokens, TOP_K), jnp.int32),
        grid=(tokens // tm,),
        in_specs=[small, small, _const_spec((1, n_exp))],
        out_specs=small,
        compiler_params=_params(("parallel",)),
        name="slots",
    )(idx, rank, pad_start.astype(F32).reshape(1, n_exp))


def _dispatch_kernel(cnt_ref, start_ref, padded_ref, npad_ref, pos_ref, h_hbm, xs_hbm, zrow, sem, zsem):
    i = pl.program_id(0)
    n = pl.num_programs(0)
    td = pos_ref.shape[1] // TOP_K
    n_exp = cnt_ref.shape[0]

    def zero_copy(slot):
        return pltpu.make_async_copy(zrow, xs_hbm.at[pl.ds(slot, 1), :], zsem)

    @pl.when(i == 0)
    def _():
        zrow[...] = jnp.zeros_like(zrow)

        def per_expert(e, carry):
            def one(r, c2):
                zero_copy(start_ref[e] + r).start()
                return c2
            return lax.fori_loop(cnt_ref[e], padded_ref[e], one, carry)
        lax.fori_loop(0, n_exp, per_expert, 0)

    def per_token(t, carry):
        for k in range(TOP_K):
            pltpu.make_async_copy(h_hbm.at[pl.ds(i * td + t, 1), :],
                                  xs_hbm.at[pl.ds(pos_ref[0, t * TOP_K + k], 1), :], sem).start()
        return carry
    lax.fori_loop(0, td, per_token, 0)

    def wait_step():
        rows = td * TOP_K
        pltpu.make_async_copy(h_hbm.at[pl.ds(0, rows), :], xs_hbm.at[pl.ds(0, rows), :], sem).wait()

    @pl.when(i > 0)
    def _():
        wait_step()

    @pl.when(i == n - 1)
    def _():
        wait_step()

        def one(r, c2):
            zero_copy(0).wait()
            return c2
        lax.fori_loop(0, npad_ref[0], one, 0)


def _dispatch_call(counts, pad_start, padded, n_pad, pos_flat, hp, n_slots):
    n_tiles, _, rows = pos_flat.shape
    width = hp.shape[1]
    grid_spec = pltpu.PrefetchScalarGridSpec(
        num_scalar_prefetch=4,
        grid=(n_tiles,),
        in_specs=[pl.BlockSpec((None, 1, rows), lambda i, *_: (i, 0, 0), memory_space=pltpu.SMEM),
                  pl.BlockSpec(memory_space=pl.ANY)],
        out_specs=pl.BlockSpec(memory_space=pl.ANY),
        scratch_shapes=[pltpu.VMEM((1, width), hp.dtype), pltpu.SemaphoreType.DMA, pltpu.SemaphoreType.DMA],
    )
    return pl.pallas_call(
        _dispatch_kernel,
        out_shape=jax.ShapeDtypeStruct((n_slots, width), hp.dtype),
        grid_spec=grid_spec,
        compiler_params=_params(("arbitrary",)),
        name="dispatch",
    )(counts, pad_start, padded, n_pad, pos_flat, hp)


def _experts_kernel(be_ref, nact_ref, x_ref, wg_ref, wu_ref, wd_ref, y_ref):
    @pl.when(pl.program_id(0) < nact_ref[0])
    def _():
        half = x_ref.shape[1]
        x_hi, x_lo = (v.astype(BF16) for v in _unpack_bf16_pairs(x_ref[...]))
        hg = _dot(x_hi, wg_ref[:half, :]) + _dot(x_lo, wg_ref[half:, :])
        hu = _dot(x_hi, wu_ref[:half, :]) + _dot(x_lo, wu_ref[half:, :])
        y_ref[...] = _pack_bf16_pairs(_dot((_silu(hg) * hu).astype(BF16), wd_ref[...]))


def _experts_call(block_expert, n_active, xs, wg, wu, wd):
    n_blocks = block_expert.shape[0]
    n_slots, half = xs.shape
    bm = n_slots // n_blocks
    d, hidden = wg.shape[1:]
    blk = lambda i, be, na: (jnp.minimum(i, na[0] - 1), 0)
    wsel = lambda i, be, na: (be[jnp.minimum(i, na[0] - 1)], 0, 0)
    grid_spec = pltpu.PrefetchScalarGridSpec(
        num_scalar_prefetch=2,
        grid=(n_blocks,),
        in_specs=[pl.BlockSpec((bm, half), blk),
                  pl.BlockSpec((None, d, hidden), wsel),
                  pl.BlockSpec((None, d, hidden), wsel),
                  pl.BlockSpec((None, hidden, d), wsel)],
        out_specs=pl.BlockSpec((bm, half), blk),
    )
    return pl.pallas_call(
        _experts_kernel,
        out_shape=jax.ShapeDtypeStruct((n_slots, half), xs.dtype),
        grid_spec=grid_spec,
        compiler_params=_params(("arbitrary",)),
        name="experts",
    )(block_expert, n_active, xs, wg, wu, wd)


def _row_gather_start(idx_ref, n_rows, src_hbm, dst_ref, sem):
    def body(r, carry):
        pltpu.make_async_copy(src_hbm.at[pl.ds(idx_ref[0, r], 1), :], dst_ref.at[pl.ds(r, 1), :], sem).start()
        return carry
    lax.fori_loop(0, n_rows, body, 0, unroll=8)


def _row_gather_wait(n_rows, src_hbm, dst_ref, sem):
    pltpu.make_async_copy(src_hbm.at[pl.ds(0, n_rows), :], dst_ref, sem).wait()


def _combine_kernel(pos_ref, pos_next_ref, y_hbm, gate_ref, shared_ref, x1_ref, g2_ref, lng_ref, lnb_ref, o_ref,
                    buf, sems, *, alpha):
    i = pl.program_id(0)
    n = pl.num_programs(0)
    rows = buf.shape[1]
    tc = o_ref.shape[0]
    slot = i % 2

    @pl.when(i == 0)
    def _():
        _row_gather_start(pos_ref, rows, y_hbm, buf.at[0], sems.at[0])

    @pl.when(i + 1 < n)
    def _():
        _row_gather_start(pos_next_ref, rows, y_hbm, buf.at[1 - slot], sems.at[1 - slot])

    _row_gather_wait(rows, y_hbm, buf.at[slot], sems.at[slot])
    half = buf.shape[2]
    f_hi = shared_ref[:, :half]
    f_lo = shared_ref[:, half:]
    for k in range(rows // tc):
        y_hi, y_lo = _unpack_bf16_pairs(buf[slot, k * tc:(k + 1) * tc, :])
        g = gate_ref[:, k:k + 1]
        f_hi = f_hi + g * y_hi
        f_lo = f_lo + g * y_lo
    f = jnp.concatenate([f_hi, f_lo], axis=1)
    o_ref[...] = _norm_rows(alpha * x1_ref[...] + g2_ref[...] * f, LN_EPS) * lng_ref[...] + lnb_ref[...]


def _combine_call(pos_tiles, y_slots, gates, shared, x1, g2, lng, lnb, *, alpha, tiles_per_batch):
    n_tiles, _, rows = pos_tiles.shape
    tokens, d = x1.shape
    tc = tokens // n_tiles
    last = n_tiles - 1
    row = pl.BlockSpec((tc, d), lambda i: (i, 0))
    return pl.pallas_call(
        functools.partial(_combine_kernel, alpha=alpha),
        out_shape=jax.ShapeDtypeStruct((tokens, d), F32),
        grid=(n_tiles,),
        in_specs=[
            pl.BlockSpec((None, 1, rows), lambda i: (i, 0, 0), memory_space=pltpu.SMEM),
            pl.BlockSpec((None, 1, rows), lambda i: (jnp.minimum(i + 1, last), 0, 0), memory_space=pltpu.SMEM),
            pl.BlockSpec(memory_space=pl.ANY),
            pl.BlockSpec((tc, TOP_K), lambda i: (i, 0)),
            row, row,
            pl.BlockSpec((None, 1, d), lambda i: (i // tiles_per_batch, 0, 0)),
            _const_spec(lng.shape), _const_spec(lnb.shape),
        ],
        out_specs=row,
        scratch_shapes=[pltpu.VMEM((2, rows, y_slots.shape[1]), y_slots.dtype), pltpu.SemaphoreType.DMA((2,))],
        compiler_params=_params(("arbitrary",)),
        name="combine",
    )(pos_tiles, pos_tiles, y_slots, gates, shared, x1, g2, lng, lnb)


def kernel(x, c, ctx, c_ctx, w_mod, b_mod, w_in, s5_lam_re, s5_lam_im, s5_log_dt, s5_b_re, s5_b_im, s5_c_re, s5_c_im, s5_d, w_glu, q_norm_g, k_norm_g, w_branch_ssm, w_branch_attn, w_out, ln1_g, ln1_b, w_router, router_bias, w_exp_gate, w_exp_up, w_exp_down, w_sh_gate, w_sh_up, w_sh_down, ln2_g, ln2_b):
    depth = w_mod.shape[0]
    assert depth == 1, "single-layer block: context outputs are never needed"
    bsz, seq, d = x.shape
    ctx_len = ctx.shape[1]
    kvw = N_KV_HEADS * HEAD_DIM
    n_exp = w_router.shape[2]
    alpha = (2.0 * depth) ** 0.25
    assert seq % GRID_W == 0 and seq % S5_CHUNK == 0 and ctx_len % S5_CHUNK == 0 and bsz % SUBLANES == 0
    lay = 0

    pad = (-(bsz + 1)) % SUBLANES
    c_all = jnp.concatenate([c, c_ctx[None, :], jnp.zeros((pad, d), F32)], axis=0)
    mod = _mod_call(c_all, w_mod[lay], b_mod[lay])
    mod_lat = mod[:bsz].reshape(bsz, N_MOD, 1, d)
    sh1, sc1, g1, sh2, sc2, g2 = (mod_lat[:, k] for k in range(N_MOD))
    mod_ctx = mod[bsz].reshape(N_MOD, 1, 1, d)

    w_in_l = w_in[lay].astype(BF16)
    w_ctx = jnp.concatenate([w_in_l[:, :d], w_in_l[:, 2 * d:2 * d + 2 * kvw]], axis=1)
    head_id = jnp.arange(kvw) // HEAD_DIM
    bd = jnp.where(head_id[:, None] == head_id[None, :], 1.0 / HEAD_DIM, 0.0).astype(BF16)
    tqa, tqb = _rope_tables(seq, q_norm_g[lay], HEAD_DIM ** -0.5)
    tka, tkb = _rope_tables(seq, k_norm_g[lay], 1.0)
    tca = jnp.tile(k_norm_g[lay].astype(F32), N_KV_HEADS)[None, :]
    tm_lat = min(512, seq)
    s_lat, k_lat, v_lat, q_lat, gs_lat, ga_lat = _inproj_call(x, sh1, sc1, w_in_l, bd, tka, tkb, tqa, tqb, tm=tm_lat)
    s_ctx, k_ctx, v_ctx = _inproj_call(ctx, mod_ctx[0], mod_ctx[1], w_ctx, bd, tca, jnp.zeros_like(tca),
                                       tm=min(256, ctx_len))

    w1, wso, lam_tab = _s5_tables(s5_lam_re[lay], s5_lam_im[lay], s5_log_dt[lay], s5_b_re[lay], s5_b_im[lay],
                                  s5_c_re[lay], s5_c_im[lay])
    y_groups = _s5_call(_to_group_major(s_ctx, s_lat), w1, wso, lam_tab, nb=bsz, nc_ctx=ctx_len // S5_CHUNK)
    y_s5 = _to_token_major(y_groups, bsz)

    attn = _attn_call(q_lat, k_ctx, k_lat, v_ctx, v_lat, tq=min(256, seq))

    row = lambda v: v.astype(F32).reshape(1, -1)
    wrs = jnp.concatenate([w_router[lay], w_sh_gate[lay], w_sh_up[lay]], axis=1).astype(BF16)
    x1, hp, shared, scores = _mixout_call(
        y_s5, s_lat, attn, gs_lat, ga_lat, x, g1, sh2, sc2, row(s5_d[lay]),
        w_glu[lay].astype(BF16), w_branch_ssm[lay].astype(BF16), w_branch_attn[lay].astype(BF16),
        w_out[lay].astype(BF16), row(ln1_g[lay]), row(ln1_b[lay]), wrs, w_sh_down[lay].astype(BF16),
        tm=min(256, seq), alpha=alpha, n_exp=n_exp)

    tokens = bsz * seq
    bm = MOE_BLOCK
    idx, gates, rank, counts_f = _route_call(scores.reshape(tokens, n_exp), router_bias[lay], tm=min(256, tokens))
    counts = counts_f.reshape(n_exp).astype(jnp.int32)
    padded = (counts + bm - 1) // bm * bm
    pad_end = jnp.cumsum(padded)
    pad_start = pad_end - padded
    n_blocks = (tokens * TOP_K + n_exp * (bm - 1) + bm - 1) // bm
    block_start = jnp.arange(n_blocks, dtype=jnp.int32) * bm
    block_expert = jnp.minimum(jnp.sum((pad_end[None, :] <= block_start[:, None]).astype(jnp.int32), axis=1), n_exp - 1)
    n_active = (pad_end[-1] // bm).reshape(1)
    n_pad = jnp.sum(padded - counts).reshape(1)
    pos = _slot_call(idx, rank, pad_start, tm=min(512, tokens))
    td = min(DISPATCH_TOKENS, tokens)
    xs = _dispatch_call(counts, pad_start, padded, n_pad, pos.reshape(tokens // td, 1, td * TOP_K),
                        hp.reshape(tokens, d // 2), n_blocks * bm)
    y_slots = _experts_call(block_expert, n_active, xs, w_exp_gate[lay].astype(BF16), w_exp_up[lay].astype(BF16),
                            w_exp_down[lay].astype(BF16))

    tc = min(COMBINE_TOKENS, seq)
    n_tiles = tokens // tc
    pos_tiles = pos.reshape(n_tiles, tc, TOP_K).transpose(0, 2, 1).reshape(n_tiles, 1, TOP_K * tc)
    out = _combine_call(pos_tiles, y_slots, gates, shared.reshape(tokens, d), x1.reshape(tokens, d), g2,
                        row(ln2_g[lay]), row(ln2_b[lay]), alpha=alpha, tiles_per_batch=seq // tc)
    return out.reshape(bsz, seq, d)
```

```python
import functools
import math

import jax
import jax.numpy as jnp
from jax import lax
from jax.experimental import pallas as pl
from jax.experimental.pallas import tpu as pltpu

F32 = jnp.float32
BF16 = jnp.bfloat16

GRID_W = 64
HEAD_DIM = 64
N_KV_HEADS = 4
S5_GROUP_CH = 16
S5_MAX_RE = -1e-4
ROPE_THETA = 10000.0
TOP_K = 8
N_EXPERT_GROUPS = 8
TOPK_GROUPS = 4
ROUTED_SCALE = 2.5
LN_EPS = 1e-5
MOD_EPS = 1e-6
RMS_EPS = 1e-6
N_MOD = 6

LANES = 128
SUBLANES = 8
VMEM_LIMIT_BYTES = 56 * 1024 * 1024

S5_CHUNK = 16
S5_COLS = S5_CHUNK * S5_GROUP_CH
MOE_BLOCK = 512
DISPATCH_TOKENS = 64
DISPATCH_BUFFERS = 3
COMBINE_TOKENS = 64


def _params(sem):
    return pltpu.CompilerParams(dimension_semantics=sem, vmem_limit_bytes=VMEM_LIMIT_BYTES)


def _const_spec(shape):
    nd = len(shape)
    return pl.BlockSpec(shape, lambda *_: (0,) * nd, pipeline_mode=pl.Buffered(1))


def _dot(a, b):
    return jnp.dot(a, b, preferred_element_type=F32)


def _norm_rows(x, eps):
    mu = jnp.mean(x, axis=-1, keepdims=True)
    xc = x - mu
    var = jnp.mean(xc * xc, axis=-1, keepdims=True)
    return xc * lax.rsqrt(var + eps)


def _silu(x):
    return x * jax.nn.sigmoid(x)


def _gelu_tanh(x):
    return 0.5 * x * (1.0 + jnp.tanh(math.sqrt(2.0 / math.pi) * (x + 0.044715 * (x * x * x))))


def _pack_bf16_pairs(v):
    n = v.shape[1] // 2
    bits = lax.bitcast_convert_type(v.astype(BF16).astype(F32), jnp.uint32)
    return bits[:, :n] | (bits[:, n:] >> 16)


def _unpack_bf16_pairs(w):
    hi = lax.bitcast_convert_type(w & jnp.uint32(0xFFFF0000), F32)
    lo = lax.bitcast_convert_type(w << 16, F32)
    return hi, lo


def _mod_kernel(c_ref, w_ref, b_ref, o_ref):
    o_ref[...] = _dot(_silu(c_ref[...]), w_ref[...]) + b_ref[...]


def _mod_call(c_all, w_mod, b_mod):
    rows, d = c_all.shape
    n = w_mod.shape[1]
    tn = d
    return pl.pallas_call(
        _mod_kernel,
        out_shape=jax.ShapeDtypeStruct((rows, n), F32),
        grid=(n // tn,),
        in_specs=[pl.BlockSpec((rows, d), lambda j: (0, 0)),
                  pl.BlockSpec((d, tn), lambda j: (0, j)),
                  pl.BlockSpec((1, tn), lambda j: (0, j))],
        out_specs=pl.BlockSpec((rows, tn), lambda j: (0, j)),
        compiler_params=_params(("arbitrary",)),
        name="mod",
    )(c_all, w_mod, b_mod.reshape(1, n))


def _swap16(t):
    width = t.shape[1]
    lane = lax.broadcasted_iota(jnp.int32, t.shape, 1)
    first = (lane & 16) == 0
    return jnp.where(first, pltpu.roll(t, width - 16, 1), pltpu.roll(t, 16, 1))


def _rms_rope(t, bd_ref, ta, tb):
    msq = _dot((t * t).astype(BF16), bd_ref[...])
    return lax.rsqrt(msq + RMS_EPS) * (t * ta + _swap16(t) * tb)


def _store_padded_heads(t, o_ref):
    rows = t.shape[0]
    lane = lax.broadcasted_iota(jnp.int32, (rows, LANES), 1)
    lo = lane < HEAD_DIM
    for j in range(t.shape[1] // LANES):
        slab = t[:, j * LANES:(j + 1) * LANES]
        a = jnp.where(lo, slab, 0.0)
        b = jnp.where(lo, 0.0, slab)
        pieces = (a, pltpu.roll(a, HEAD_DIM, 1), pltpu.roll(b, HEAD_DIM, 1), b)
        for p, piece in enumerate(pieces):
            c0 = (4 * j + p) * LANES
            o_ref[:, c0:c0 + LANES] = piece.astype(o_ref.dtype)


def _inproj_kernel(x_ref, sh_ref, sc_ref, w_ref, bd_ref, tka_ref, tkb_ref, *rest, d, kvw, has_q):
    if has_q:
        tqa_ref, tqb_ref, s_ref, k_ref, v_ref, q_ref, gs_ref, ga_ref = rest
    else:
        s_ref, k_ref, v_ref = rest
    u = (_norm_rows(x_ref[...], MOD_EPS) * (1.0 + sc_ref[...]) + sh_ref[...]).astype(BF16)
    col = 0
    s_ref[...] = _dot(u, w_ref[:, col:col + d]).astype(s_ref.dtype)
    col += d
    if has_q:
        for c in range(d // kvw):
            q = _dot(u, w_ref[:, col + c * kvw:col + (c + 1) * kvw])
            q_ref[:, c * kvw:(c + 1) * kvw] = _rms_rope(q, bd_ref, tqa_ref[...], tqb_ref[...]).astype(q_ref.dtype)
        col += d
    k = _dot(u, w_ref[:, col:col + kvw])
    _store_padded_heads(_rms_rope(k, bd_ref, tka_ref[...], tkb_ref[...]), k_ref)
    col += kvw
    _store_padded_heads(_dot(u, w_ref[:, col:col + kvw]), v_ref)
    col += kvw
    if has_q:
        gs_ref[...] = jax.nn.sigmoid(_dot(u, w_ref[:, col:col + d])).astype(gs_ref.dtype)
        col += d
        ga_ref[...] = jax.nn.sigmoid(_dot(u, w_ref[:, col:col + d])).astype(ga_ref.dtype)


def _inproj_call(x, shift, scale, w, bd, tka, tkb, tqa=None, tqb=None, *, tm):
    bsz, length, d = x.shape
    has_q = tqa is not None
    kvw = N_KV_HEADS * HEAD_DIM
    per_batch = shift.shape[0] > 1
    tab_rows = tka.shape[0]
    tab_blk = tm if tab_rows > 1 else 1
    mod_spec = pl.BlockSpec((None, 1, d), (lambda b, i: (b, 0, 0)) if per_batch else (lambda b, i: (0, 0, 0)))
    tab_spec = pl.BlockSpec((tab_blk, kvw), (lambda b, i: (i, 0)) if tab_rows > 1 else (lambda b, i: (0, 0)))
    row_spec = lambda width: pl.BlockSpec((None, tm, width), lambda b, i: (b, i, 0))
    in_specs = [row_spec(d), mod_spec, mod_spec, _const_spec(w.shape), _const_spec(bd.shape), tab_spec, tab_spec]
    args = [x, shift, scale, w, bd, tka, tkb]
    widths = [d, 4 * kvw, 4 * kvw]
    if has_q:
        in_specs += [tab_spec, tab_spec]
        args += [tqa, tqb]
        widths += [d, d, d]
    return pl.pallas_call(
        functools.partial(_inproj_kernel, d=d, kvw=kvw, has_q=has_q),
        out_shape=[jax.ShapeDtypeStruct((bsz, length, wd), BF16) for wd in widths],
        grid=(bsz, length // tm),
        in_specs=in_specs,
        out_specs=[row_spec(wd) for wd in widths],
        compiler_params=_params(("parallel", "parallel")),
        name="inproj_lat" if has_q else "inproj_ctx",
    )(*args)


def _rope_tables(seq, gain, scale):
    half = HEAD_DIM // 2
    inv_freq = ROPE_THETA ** (-jnp.arange(0, half, 2, dtype=F32) / half)
    t = jnp.arange(seq, dtype=jnp.int32)
    pos = jnp.stack([(t // GRID_W).astype(F32), (t % GRID_W).astype(F32)], axis=1)
    dim = jnp.arange(HEAD_DIM)
    axis = dim // half
    second = ((dim % half) // (half // 2)) == 1
    freq = inv_freq[dim % (half // 2)]
    ang = pos[:, axis] * freq[None, :]
    partner = jnp.where(second, dim - half // 2, dim + half // 2)
    g = gain.astype(F32)
    ta = jnp.cos(ang) * g[None, :] * scale
    tb = jnp.sin(ang) * jnp.where(second, 1.0, -1.0)[None, :] * g[partner][None, :] * scale
    return jnp.tile(ta, (1, N_KV_HEADS)), jnp.tile(tb, (1, N_KV_HEADS))


def _attn_kernel(q_ref, kc_ref, kl_ref, vc_ref, vl_ref, o_ref):
    nt = (((1,), (1,)), ((), ()))
    rows = q_ref.shape[0]
    lane = lax.broadcasted_iota(jnp.int32, (rows, LANES), 1)
    for j in range(q_ref.shape[1] // LANES):
        qs = q_ref[:, j * LANES:(j + 1) * LANES]
        acc = jnp.zeros((rows, LANES), F32)
        inv = []
        for half in range(2):
            cols = slice(half * LANES, (half + 1) * LANES)
            s_c = lax.dot_general(qs, kc_ref[:, cols], nt, preferred_element_type=F32)
            s_l = lax.dot_general(qs, kl_ref[:, cols], nt, preferred_element_type=F32)
            m = jnp.maximum(jnp.max(s_c, axis=-1, keepdims=True), jnp.max(s_l, axis=-1, keepdims=True))
            e_c = jnp.exp(s_c - m)
            e_l = jnp.exp(s_l - m)
            inv.append(1.0 / (jnp.sum(e_c, axis=-1, keepdims=True) + jnp.sum(e_l, axis=-1, keepdims=True)))
            acc = acc + _dot(e_c.astype(BF16), vc_ref[:, cols]) + _dot(e_l.astype(BF16), vl_ref[:, cols])
        o_ref[:, j * LANES:(j + 1) * LANES] = (acc * jnp.where(lane < HEAD_DIM, inv[0], inv[1])).astype(o_ref.dtype)


def _attn_call(q, kc, kl, vc, vl, *, tq):
    bsz, seq, d = q.shape
    ctx = kc.shape[1]
    gw = d // N_KV_HEADS
    q_spec = pl.BlockSpec((None, tq, gw), lambda b, h, i: (b, i, h))
    kv_spec = lambda length: pl.BlockSpec((None, length, 2 * LANES), lambda b, h, i: (b, 0, h))
    return pl.pallas_call(
        _attn_kernel,
        out_shape=jax.ShapeDtypeStruct((bsz, seq, d), BF16),
        grid=(bsz, N_KV_HEADS, seq // tq),
        in_specs=[q_spec, kv_spec(ctx), kv_spec(seq), kv_spec(ctx), kv_spec(seq)],
        out_specs=q_spec,
        compiler_params=_params(("parallel", "parallel", "arbitrary")),
        name="attention",
    )(q, kc, kl, vc, vl)


def _s5_tables(lam_re, lam_im, log_dt, b_re, b_im, c_re, c_im):
    n = S5_CHUNK
    lam = lax.complex(jnp.minimum(lam_re.astype(F32), S5_MAX_RE), lam_im.astype(F32))
    lam_dt = lam * jnp.exp(log_dt.astype(F32))[..., None]
    b_bar = ((jnp.exp(lam_dt) - 1.0) / lam)[..., None] * lax.complex(b_re.astype(F32), b_im.astype(F32))
    c_mat = lax.complex(c_re.astype(F32), c_im.astype(F32))
    pw = jnp.exp(lam_dt[None] * jnp.arange(n + 1, dtype=F32)[:, None, None, None])
    kern = jnp.real(jnp.einsum('dgcp,jdgp,dgpe->djgce', c_mat, pw[:n], b_bar))
    s_idx = jnp.arange(n)[:, None]
    t_idx = jnp.arange(n)[None, :]

    def toeplitz(k, lag):
        g = k[jnp.clip(lag, 0, n - 1)]
        g = jnp.where((lag >= 0)[:, :, None, None, None], g, 0.0)
        return g.transpose(2, 0, 4, 1, 3)

    toep = toeplitz(kern[0], t_idx - s_idx) + toeplitz(kern[1], s_idx - t_idx)
    groups = toep.shape[0]
    toep = toep.reshape(groups, S5_COLS, S5_COLS)
    wis_f = jnp.einsum('sgp,gpe->gsep', pw[n - 1 - jnp.arange(n), 0], b_bar[0]).reshape(groups, S5_COLS, -1)
    wis_r = jnp.einsum('sgp,gpe->gsep', pw[jnp.arange(n), 1], b_bar[1]).reshape(groups, S5_COLS, -1)
    w1 = jnp.concatenate([toep, jnp.real(wis_f), jnp.real(wis_r), jnp.imag(wis_f), jnp.imag(wis_r)], axis=-1)
    m_f = jnp.einsum('gcp,tgp->gptc', c_mat[0], pw[1 + jnp.arange(n), 0]).reshape(groups, -1, S5_COLS)
    m_r = jnp.einsum('gcp,tgp->gptc', c_mat[1], pw[n - jnp.arange(n), 1]).reshape(groups, -1, S5_COLS)
    wso = jnp.concatenate([jnp.real(m_f), jnp.real(m_r), -jnp.imag(m_f), -jnp.imag(m_r)], axis=1)
    lam_n = jnp.concatenate([pw[n, 0], pw[n, 1]], axis=-1)
    lam_tab = jnp.stack([jnp.real(lam_n), jnp.imag(lam_n)], axis=1)
    return w1.astype(BF16), wso.astype(BF16), lam_tab


def _s5_kernel(u_ref, w1_ref, wso_ref, lam_ref, y_ref, a_ref, xf_re, xf_im, xr_re, xr_im, *, nb, nc_ctx, nc):
    cols = S5_COLS
    half = lam_ref.shape[1] // 2
    a_ref[...] = _dot(u_ref[...], w1_ref[...])
    l_re = lam_ref[0:1, :]
    l_im = lam_ref[1:2, :]
    lo = lax.broadcasted_iota(jnp.int32, (nb, 2 * half), 1) < half

    def step(i, carry):
        x_re, x_im = carry
        i_rev = jnp.where(i < nc_ctx, nc_ctx - 1 - i, nc - 1 + nc_ctx - i)
        rf = pl.ds(pl.multiple_of(i * nb, nb), nb)
        rr = pl.ds(pl.multiple_of(i_rev * nb, nb), nb)
        xf_re[rf, :] = x_re
        xf_im[rf, :] = x_im
        xr_re[rr, :] = x_re
        xr_im[rr, :] = x_im
        s_re = jnp.where(lo, a_ref[rf, cols:cols + 2 * half], a_ref[rr, cols:cols + 2 * half])
        s_im = jnp.where(lo, a_ref[rf, cols + 2 * half:cols + 4 * half], a_ref[rr, cols + 2 * half:cols + 4 * half])
        return l_re * x_re - l_im * x_im + s_re, l_re * x_im + l_im * x_re + s_im

    zero = jnp.zeros((nb, 2 * half), F32)
    lax.fori_loop(0, nc, step, (zero, zero))
    r0 = nc_ctx * nb
    rows = (nc - nc_ctx) * nb
    lo_all = lax.broadcasted_iota(jnp.int32, (rows, 2 * half), 1) < half
    y = a_ref[r0:, 0:cols]
    for k, (f_ref, r_ref) in enumerate(((xf_re, xr_re), (xf_im, xr_im))):
        st = jnp.where(lo_all, f_ref[r0:, :], r_ref[r0:, :])
        hi = st.astype(BF16)
        lo_part = (st - hi.astype(F32)).astype(BF16)
        w = wso_ref[k * 2 * half:(k + 1) * 2 * half, :]
        y = y + _dot(hi, w) + _dot(lo_part, w)
    y_ref[...] = y


def _s5_call(u, w1, wso, lam_tab, *, nb, nc_ctx):
    groups, rows, cols = u.shape
    nc = rows // nb
    out_rows = (nc - nc_ctx) * nb
    st = lam_tab.shape[2]
    return pl.pallas_call(
        functools.partial(_s5_kernel, nb=nb, nc_ctx=nc_ctx, nc=nc),
        out_shape=jax.ShapeDtypeStruct((groups, out_rows, cols), F32),
        grid=(groups,),
        in_specs=[pl.BlockSpec((None, rows, cols), lambda g: (g, 0, 0)),
                  pl.BlockSpec((None,) + w1.shape[1:], lambda g: (g, 0, 0)),
                  pl.BlockSpec((None,) + wso.shape[1:], lambda g: (g, 0, 0)),
                  pl.BlockSpec((None, 2, st), lambda g: (g, 0, 0))],
        out_specs=pl.BlockSpec((None, out_rows, cols), lambda g: (g, 0, 0)),
        scratch_shapes=[pltpu.VMEM((rows, w1.shape[2]), F32)] + [pltpu.VMEM((rows, st), F32)] * 4,
        compiler_params=_params(("parallel",)),
        name="s5",
    )(u, w1, wso, lam_tab)


def _to_group_major(s_ctx, s_lat):
    s_all = jnp.concatenate([s_ctx, s_lat], axis=1)
    bsz, length, width = s_all.shape
    groups = width // S5_GROUP_CH
    nc = length // S5_CHUNK
    t = s_all.reshape(bsz, nc, S5_CHUNK, groups, S5_GROUP_CH).transpose(3, 1, 0, 2, 4)
    return t.reshape(groups, nc * bsz, S5_COLS)


def _to_token_major(y, bsz):
    groups, rows, _ = y.shape
    nc = rows // bsz
    t = y.reshape(groups, nc, bsz, S5_CHUNK, S5_GROUP_CH).transpose(2, 1, 3, 0, 4)
    return t.reshape(bsz, nc * S5_CHUNK, groups * S5_GROUP_CH)


def _mixout_kernel(y_ref, s_ref, at_ref, gs_ref, ga_ref, x_ref, g1_ref, sh2_ref, sc2_ref, dsk_ref,
                   wglu_ref, wbs_ref, wba_ref, wout_ref, lng_ref, lnb_ref, wrs_ref, wsd_ref,
                   x1_ref, h_ref, shared_ref, scores_ref, *, alpha, n_exp):
    y = y_ref[...] + s_ref[...].astype(F32) * dsk_ref[...]
    z = _gelu_tanh(y)
    ssm = z * jax.nn.sigmoid(_dot(z.astype(BF16), wglu_ref[...]))
    merged = (gs_ref[...].astype(F32) * _dot(ssm.astype(BF16), wbs_ref[...])
              + ga_ref[...].astype(F32) * _dot(at_ref[...], wba_ref[...]))
    y_mix = _dot(merged.astype(BF16), wout_ref[...])
    x1 = _norm_rows(alpha * x_ref[...] + g1_ref[...] * y_mix, LN_EPS) * lng_ref[...] + lnb_ref[...]
    x1_ref[...] = x1
    h = _norm_rows(x1, MOD_EPS) * (1.0 + sc2_ref[...]) + sh2_ref[...]
    h_ref[...] = _pack_bf16_pairs(h)
    rs = _dot(h.astype(BF16), wrs_ref[...])
    scores_ref[...] = jax.nn.sigmoid(rs[:, :n_exp])
    sh_hidden = wsd_ref.shape[0]
    hid = _silu(rs[:, n_exp:n_exp + sh_hidden]) * rs[:, n_exp + sh_hidden:]
    shared_ref[...] = _dot(hid.astype(BF16), wsd_ref[...])


def _mixout_call(y, s, attn, gs, ga, x, g1, sh2, sc2, dsk, wglu, wbs, wba, wout, lng, lnb, wrs, wsd,
                 *, tm, alpha, n_exp):
    bsz, seq, d = x.shape
    row = lambda width: pl.BlockSpec((None, tm, width), lambda b, i: (b, i, 0))
    mod = pl.BlockSpec((None, 1, d), lambda b, i: (b, 0, 0))
    consts = [dsk, wglu, wbs, wba, wout, lng, lnb, wrs, wsd]
    return pl.pallas_call(
        functools.partial(_mixout_kernel, alpha=alpha, n_exp=n_exp),
        out_shape=[jax.ShapeDtypeStruct((bsz, seq, d), F32), jax.ShapeDtypeStruct((bsz, seq, d // 2), jnp.uint32),
                   jax.ShapeDtypeStruct((bsz, seq, d), F32), jax.ShapeDtypeStruct((bsz, seq, n_exp), F32)],
        grid=(bsz, seq // tm),
        in_specs=[row(d)] * 6 + [mod] * 3 + [_const_spec(a.shape) for a in consts],
        out_specs=[row(d), row(d // 2), row(d), row(n_exp)],
        compiler_params=_params(("parallel", "parallel")),
        name="mixout",
    )(y, s, attn, gs, ga, x, g1, sh2, sc2, *consts)


def _segment_allreduce(x, lane, width, op):
    total = x.shape[1]
    d = 1
    while d < width:
        partner = jnp.where((lane & d) == 0, pltpu.roll(x, total - d, 1), pltpu.roll(x, d, 1))
        x = op(x, partner)
        d *= 2
    return x


def _route_kernel(s_ref, bias_ref, ltri_ref, idx_ref, gate_ref, rank_ref, counts_ref, base_ref):
    @pl.when(pl.program_id(0) == 0)
    def _():
        base_ref[...] = jnp.zeros_like(base_ref)

    s = s_ref[...]
    tm, n_exp = s.shape
    per_group = n_exp // N_EXPERT_GROUPS
    neg = -jnp.inf
    b = s + bias_ref[...]
    lane = lax.broadcasted_iota(jnp.int32, (tm, n_exp), 1)
    lane_f = lane.astype(F32)
    m1 = _segment_allreduce(b, lane, per_group, jnp.maximum)
    is1 = b == m1
    n1 = _segment_allreduce(jnp.where(is1, 1.0, 0.0), lane, per_group, jnp.add)
    m2 = _segment_allreduce(jnp.where(is1, neg, b), lane, per_group, jnp.maximum)
    gscore = m1 + jnp.where(n1 >= 2.0, m1, m2)
    gid = lane // per_group
    beaten = jnp.zeros((tm, n_exp), F32)
    for j in range(1, N_EXPERT_GROUPS):
        other = pltpu.roll(gscore, j * per_group, 1)
        oid = (gid - j) & (N_EXPERT_GROUPS - 1)
        beaten = beaten + jnp.where((other > gscore) | ((other == gscore) & (oid < gid)), 1.0, 0.0)
    masked = jnp.where(beaten < float(TOPK_GROUPS), b, neg)
    lane_k = lax.broadcasted_iota(jnp.int32, (tm, TOP_K), 1)
    idx = jnp.zeros((tm, TOP_K), F32)
    val = jnp.zeros((tm, TOP_K), F32)
    chosen = jnp.zeros((tm, n_exp), F32)
    for k in range(TOP_K):
        m = jnp.max(masked, axis=1, keepdims=True)
        first = jnp.min(jnp.where(masked == m, lane_f, float(n_exp)), axis=1, keepdims=True)
        sel = lane_f == first
        idx = jnp.where(lane_k == k, first, idx)
        val = jnp.where(lane_k == k, jnp.sum(jnp.where(sel, s, 0.0), axis=1, keepdims=True), val)
        chosen = jnp.where(sel, 1.0, chosen)
        masked = jnp.where(sel, neg, masked)
    before = _dot(ltri_ref[...], chosen.astype(BF16)) + base_ref[...]
    rank = jnp.zeros((tm, TOP_K), F32)
    for k in range(TOP_K):
        sel = lane_f == idx[:, k:k + 1]
        rank = jnp.where(lane_k == k, jnp.sum(jnp.where(sel, before, 0.0), axis=1, keepdims=True), rank)
    idx_ref[...] = idx.astype(jnp.int32)
    rank_ref[...] = rank.astype(jnp.int32)
    gate_ref[...] = val / jnp.sum(val, axis=1, keepdims=True) * ROUTED_SCALE
    base_ref[...] = base_ref[...] + jnp.sum(chosen, axis=0, keepdims=True)
    counts_ref[...] = base_ref[...]


def _route_call(scores, bias, *, tm):
    tokens, n_exp = scores.shape
    r_id = lax.broadcasted_iota(jnp.int32, (tm, tm), 0)
    c_id = lax.broadcasted_iota(jnp.int32, (tm, tm), 1)
    ltri = (c_id < r_id).astype(BF16)
    small = pl.BlockSpec((tm, TOP_K), lambda i: (i, 0))
    return pl.pallas_call(
        _route_kernel,
        out_shape=[jax.ShapeDtypeStruct((tokens, TOP_K), jnp.int32), jax.ShapeDtypeStruct((tokens, TOP_K), F32),
                   jax.ShapeDtypeStruct((tokens, TOP_K), jnp.int32), jax.ShapeDtypeStruct((1, n_exp), F32)],
        grid=(tokens // tm,),
        in_specs=[pl.BlockSpec((tm, n_exp), lambda i: (i, 0)), _const_spec((1, n_exp)), _const_spec((tm, tm))],
        out_specs=[small, small, small, pl.BlockSpec((1, n_exp), lambda i: (0, 0))],
        scratch_shapes=[pltpu.VMEM((1, n_exp), F32)],
        compiler_params=_params(("arbitrary",)),
        name="route",
    )(scores, bias.astype(F32).reshape(1, n_exp), ltri)


def _slot_kernel(idx_ref, rank_ref, start_ref, pos_ref):
    idx = idx_ref[...]
    tm = idx.shape[0]
    n_exp = start_ref.shape[1]
    lane = lax.broadcasted_iota(jnp.int32, (tm, n_exp), 1)
    lane_k = lax.broadcasted_iota(jnp.int32, (tm, TOP_K), 1)
    pos = rank_ref[...]
    for k in range(TOP_K):
        start = jnp.sum(jnp.where(lane == idx[:, k:k + 1], start_ref[...], 0.0), axis=1, keepdims=True)
        pos = pos + jnp.where(lane_k == k, start.astype(jnp.int32), 0)
    pos_ref[...] = pos


def _slot_call(idx, rank, pad_start, *, tm):
    tokens = idx.shape[0]
    n_exp = pad_start.shape[0]
    small = pl.BlockSpec((tm, TOP_K), lambda i: (i, 0))
    return pl.pallas_call(
        _slot_kernel,
        out_shape=jax.ShapeDtypeStruct((tokens, TOP_K), jnp.int32),
        grid=(tokens // tm,),
        in_specs=[small, small, _const_spec((1, n_exp))],
        out_specs=small,
        compiler_params=_params(("parallel",)),
        name="slots",
    )(idx, rank, pad_start.astype(F32).reshape(1, n_exp))


def _dispatch_kernel(cnt_ref, start_ref, padded_ref, npad_ref, pos_ref, h_hbm, xs_hbm, hbuf, zrow, lsem, ssem, zsem):
    i = pl.program_id(0)
    n = pl.num_programs(0)
    nbuf, td, _ = hbuf.shape
    n_exp = cnt_ref.shape[0]
    slot = i % nbuf

    def zero_copy(row):
        return pltpu.make_async_copy(zrow, xs_hbm.at[pl.ds(row, 1), :], zsem)

    def load(step, s):
        return pltpu.make_async_copy(h_hbm.at[pl.ds(step * td, td), :], hbuf.at[s], lsem.at[s])

    def wait_scatters(s):
        rows = td * TOP_K
        pltpu.make_async_copy(h_hbm.at[pl.ds(0, rows), :], xs_hbm.at[pl.ds(0, rows), :], ssem.at[s]).wait()

    @pl.when(i == 0)
    def _():
        zrow[...] = jnp.zeros_like(zrow)

        def per_expert(e, carry):
            def one(r, c2):
                zero_copy(start_ref[e] + r).start()
                return c2
            return lax.fori_loop(cnt_ref[e], padded_ref[e], one, carry)
        lax.fori_loop(0, n_exp, per_expert, 0)
        load(0, 0).start()

    @pl.when(i >= nbuf - 1)
    def _():
        wait_scatters((i + 1) % nbuf)

    @pl.when(i + 1 < n)
    def _():
        load(i + 1, (i + 1) % nbuf).start()

    load(i, slot).wait()

    def per_token(t, carry):
        for k in range(TOP_K):
            pltpu.make_async_copy(hbuf.at[slot, pl.ds(t, 1), :],
                                  xs_hbm.at[pl.ds(pos_ref[0, t * TOP_K + k], 1), :], ssem.at[slot]).start()
        return carry
    lax.fori_loop(0, td, per_token, 0)

    @pl.when(i == n - 1)
    def _():
        for back in range(nbuf - 1):
            @pl.when(i >= back)
            def _():
                wait_scatters((i - back) % nbuf)

        def one(r, c2):
            zero_copy(0).wait()
            return c2
        lax.fori_loop(0, npad_ref[0], one, 0)


def _dispatch_call(counts, pad_start, padded, n_pad, pos_flat, hp, n_slots):
    n_tiles, _, rows = pos_flat.shape
    width = hp.shape[1]
    grid_spec = pltpu.PrefetchScalarGridSpec(
        num_scalar_prefetch=4,
        grid=(n_tiles,),
        in_specs=[pl.BlockSpec((None, 1, rows), lambda i, *_: (i, 0, 0), memory_space=pltpu.SMEM),
                  pl.BlockSpec(memory_space=pl.ANY)],
        out_specs=pl.BlockSpec(memory_space=pl.ANY),
        scratch_shapes=[pltpu.VMEM((DISPATCH_BUFFERS, rows // TOP_K, width), hp.dtype), pltpu.VMEM((1, width), hp.dtype),
                        pltpu.SemaphoreType.DMA((DISPATCH_BUFFERS,)), pltpu.SemaphoreType.DMA((DISPATCH_BUFFERS,)),
                        pltpu.SemaphoreType.DMA],
    )
    return pl.pallas_call(
        _dispatch_kernel,
        out_shape=jax.ShapeDtypeStruct((n_slots, width), hp.dtype),
        grid_spec=grid_spec,
        compiler_params=_params(("arbitrary",)),
        name="dispatch",
    )(counts, pad_start, padded, n_pad, pos_flat, hp)


def _experts_kernel(be_ref, nact_ref, x_ref, wg_ref, wu_ref, wd_ref, y_ref):
    @pl.when(pl.program_id(0) < nact_ref[0])
    def _():
        half = x_ref.shape[1]
        x_hi, x_lo = (v.astype(BF16) for v in _unpack_bf16_pairs(x_ref[...]))
        hg = _dot(x_hi, wg_ref[:half, :]) + _dot(x_lo, wg_ref[half:, :])
        hu = _dot(x_hi, wu_ref[:half, :]) + _dot(x_lo, wu_ref[half:, :])
        y_ref[...] = _pack_bf16_pairs(_dot((_silu(hg) * hu).astype(BF16), wd_ref[...]))


def _experts_call(block_expert, n_active, xs, wg, wu, wd):
    n_blocks = block_expert.shape[0]
    n_slots, half = xs.shape
    bm = n_slots // n_blocks
    d, hidden = wg.shape[1:]
    blk = lambda i, be, na: (jnp.minimum(i, na[0] - 1), 0)
    wsel = lambda i, be, na: (be[jnp.minimum(i, na[0] - 1)], 0, 0)
    grid_spec = pltpu.PrefetchScalarGridSpec(
        num_scalar_prefetch=2,
        grid=(n_blocks,),
        in_specs=[pl.BlockSpec((bm, half), blk),
                  pl.BlockSpec((None, d, hidden), wsel),
                  pl.BlockSpec((None, d, hidden), wsel),
                  pl.BlockSpec((None, hidden, d), wsel)],
        out_specs=pl.BlockSpec((bm, half), blk),
    )
    return pl.pallas_call(
        _experts_kernel,
        out_shape=jax.ShapeDtypeStruct((n_slots, half), xs.dtype),
        grid_spec=grid_spec,
        compiler_params=_params(("arbitrary",)),
        name="experts",
    )(block_expert, n_active, xs, wg, wu, wd)


def _row_gather_start(idx_ref, n_rows, src_hbm, dst_ref, sem):
    def body(r, carry):
        pltpu.make_async_copy(src_hbm.at[pl.ds(idx_ref[0, r], 1), :], dst_ref.at[pl.ds(r, 1), :], sem).start()
        return carry
    lax.fori_loop(0, n_rows, body, 0, unroll=8)


def _row_gather_wait(n_rows, src_hbm, dst_ref, sem):
    pltpu.make_async_copy(src_hbm.at[pl.ds(0, n_rows), :], dst_ref, sem).wait()


def _combine_kernel(pos_ref, pos_next_ref, y_hbm, gate_ref, shared_ref, x1_ref, g2_ref, lng_ref, lnb_ref, o_ref,
                    buf, sems, *, alpha):
    i = pl.program_id(0)
    n = pl.num_programs(0)
    rows = buf.shape[1]
    tc = o_ref.shape[0]
    slot = i % 2

    @pl.when(i == 0)
    def _():
        _row_gather_start(pos_ref, rows, y_hbm, buf.at[0], sems.at[0])

    @pl.when(i + 1 < n)
    def _():
        _row_gather_start(pos_next_ref, rows, y_hbm, buf.at[1 - slot], sems.at[1 - slot])

    _row_gather_wait(rows, y_hbm, buf.at[slot], sems.at[slot])
    half = buf.shape[2]
    f_hi = shared_ref[:, :half]
    f_lo = shared_ref[:, half:]
    for k in range(rows // tc):
        y_hi, y_lo = _unpack_bf16_pairs(buf[slot, k * tc:(k + 1) * tc, :])
        g = gate_ref[:, k:k + 1]
        f_hi = f_hi + g * y_hi
        f_lo = f_lo + g * y_lo
    f = jnp.concatenate([f_hi, f_lo], axis=1)
    o_ref[...] = _norm_rows(alpha * x1_ref[...] + g2_ref[...] * f, LN_EPS) * lng_ref[...] + lnb_ref[...]


def _combine_call(pos_tiles, y_slots, gates, shared, x1, g2, lng, lnb, *, alpha, tiles_per_batch):
    n_tiles, _, rows = pos_tiles.shape
    tokens, d = x1.shape
    tc = tokens // n_tiles
    last = n_tiles - 1
    row = pl.BlockSpec((tc, d), lambda i: (i, 0))
    return pl.pallas_call(
        functools.partial(_combine_kernel, alpha=alpha),
        out_shape=jax.ShapeDtypeStruct((tokens, d), F32),
        grid=(n_tiles,),
        in_specs=[
            pl.BlockSpec((None, 1, rows), lambda i: (i, 0, 0), memory_space=pltpu.SMEM),
            pl.BlockSpec((None, 1, rows), lambda i: (jnp.minimum(i + 1, last), 0, 0), memory_space=pltpu.SMEM),
            pl.BlockSpec(memory_space=pl.ANY),
            pl.BlockSpec((tc, TOP_K), lambda i: (i, 0)),
            row, row,
            pl.BlockSpec((None, 1, d), lambda i: (i // tiles_per_batch, 0, 0)),
            _const_spec(lng.shape), _const_spec(lnb.shape),
        ],
        out_specs=row,
        scratch_shapes=[pltpu.VMEM((2, rows, y_slots.shape[1]), y_slots.dtype), pltpu.SemaphoreType.DMA((2,))],
        compiler_params=_params(("arbitrary",)),
        name="combine",
    )(pos_tiles, pos_tiles, y_slots, gates, shared, x1, g2, lng, lnb)


def kernel(x, c, ctx, c_ctx, w_mod, b_mod, w_in, s5_lam_re, s5_lam_im, s5_log_dt, s5_b_re, s5_b_im, s5_c_re, s5_c_im, s5_d, w_glu, q_norm_g, k_norm_g, w_branch_ssm, w_branch_attn, w_out, ln1_g, ln1_b, w_router, router_bias, w_exp_gate, w_exp_up, w_exp_down, w_sh_gate, w_sh_up, w_sh_down, ln2_g, ln2_b):
    depth = w_mod.shape[0]
    assert depth == 1, "single-layer block: context outputs are never needed"
    bsz, seq, d = x.shape
    ctx_len = ctx.shape[1]
    kvw = N_KV_HEADS * HEAD_DIM
    n_exp = w_router.shape[2]
    alpha = (2.0 * depth) ** 0.25
    assert seq % GRID_W == 0 and seq % S5_CHUNK == 0 and ctx_len % S5_CHUNK == 0 and bsz % SUBLANES == 0
    lay = 0

    pad = (-(bsz + 1)) % SUBLANES
    c_all = jnp.concatenate([c, c_ctx[None, :], jnp.zeros((pad, d), F32)], axis=0)
    mod = _mod_call(c_all, w_mod[lay], b_mod[lay])
    mod_lat = mod[:bsz].reshape(bsz, N_MOD, 1, d)
    sh1, sc1, g1, sh2, sc2, g2 = (mod_lat[:, k] for k in range(N_MOD))
    mod_ctx = mod[bsz].reshape(N_MOD, 1, 1, d)

    w_in_l = w_in[lay].astype(BF16)
    w_ctx = jnp.concatenate([w_in_l[:, :d], w_in_l[:, 2 * d:2 * d + 2 * kvw]], axis=1)
    head_id = jnp.arange(kvw) // HEAD_DIM
    bd = jnp.where(head_id[:, None] == head_id[None, :], 1.0 / HEAD_DIM, 0.0).astype(BF16)
    tqa, tqb = _rope_tables(seq, q_norm_g[lay], HEAD_DIM ** -0.5)
    tka, tkb = _rope_tables(seq, k_norm_g[lay], 1.0)
    tca = jnp.tile(k_norm_g[lay].astype(F32), N_KV_HEADS)[None, :]
    tm_lat = min(512, seq)
    s_lat, k_lat, v_lat, q_lat, gs_lat, ga_lat = _inproj_call(x, sh1, sc1, w_in_l, bd, tka, tkb, tqa, tqb, tm=tm_lat)
    s_ctx, k_ctx, v_ctx = _inproj_call(ctx, mod_ctx[0], mod_ctx[1], w_ctx, bd, tca, jnp.zeros_like(tca),
                                       tm=min(256, ctx_len))

    w1, wso, lam_tab = _s5_tables(s5_lam_re[lay], s5_lam_im[lay], s5_log_dt[lay], s5_b_re[lay], s5_b_im[lay],
                                  s5_c_re[lay], s5_c_im[lay])
    y_groups = _s5_call(_to_group_major(s_ctx, s_lat), w1, wso, lam_tab, nb=bsz, nc_ctx=ctx_len // S5_CHUNK)
    y_s5 = _to_token_major(y_groups, bsz)

    attn = _attn_call(q_lat, k_ctx, k_lat, v_ctx, v_lat, tq=min(512, seq))

    row = lambda v: v.astype(F32).reshape(1, -1)
    wrs = jnp.concatenate([w_router[lay], w_sh_gate[lay], w_sh_up[lay]], axis=1).astype(BF16)
    x1, hp, shared, scores = _mixout_call(
        y_s5, s_lat, attn, gs_lat, ga_lat, x, g1, sh2, sc2, row(s5_d[lay]),
        w_glu[lay].astype(BF16), w_branch_ssm[lay].astype(BF16), w_branch_attn[lay].astype(BF16),
        w_out[lay].astype(BF16), row(ln1_g[lay]), row(ln1_b[lay]), wrs, w_sh_down[lay].astype(BF16),
        tm=min(256, seq), alpha=alpha, n_exp=n_exp)

    tokens = bsz * seq
    bm = MOE_BLOCK
    idx, gates, rank, counts_f = _route_call(scores.reshape(tokens, n_exp), router_bias[lay], tm=min(256, tokens))
    counts = counts_f.reshape(n_exp).astype(jnp.int32)
    padded = (counts + bm - 1) // bm * bm
    pad_end = jnp.cumsum(padded)
    pad_start = pad_end - padded
    n_blocks = (tokens * TOP_K + n_exp * (bm - 1) + bm - 1) // bm
    block_start = jnp.arange(n_blocks, dtype=jnp.int32) * bm
    block_expert = jnp.minimum(jnp.sum((pad_end[None, :] <= block_start[:, None]).astype(jnp.int32), axis=1), n_exp - 1)
    n_active = (pad_end[-1] // bm).reshape(1)
    n_pad = jnp.sum(padded - counts).reshape(1)
    pos = _slot_call(idx, rank, pad_start, tm=min(512, tokens))
    td = min(DISPATCH_TOKENS, tokens)
    xs = _dispatch_call(counts, pad_start, padded, n_pad, pos.reshape(tokens // td, 1, td * TOP_K),
                        hp.reshape(tokens, d // 2), n_blocks * bm)
    y_slots = _experts_call(block_expert, n_active, xs, w_exp_gate[lay].astype(BF16), w_exp_up[lay].astype(BF16),
                            w_exp_down[lay].astype(BF16))

    tc = min(COMBINE_TOKENS, seq)
    n_tiles = tokens // tc
    pos_tiles = pos.reshape(n_tiles, tc, TOP_K).transpose(0, 2, 1).reshape(n_tiles, 1, TOP_K * tc)
    out = _combine_call(pos_tiles, y_slots, gates, shared.reshape(tokens, d), x1.reshape(tokens, d), g2,
                        row(ln2_g[lay]), row(ln2_b[lay]), alpha=alpha, tiles_per_batch=seq // tc)
    return out.reshape(bsz, seq, d)
```

```python
import functools
import math

import jax
import jax.numpy as jnp
from jax import lax
from jax.experimental import pallas as pl
from jax.experimental.pallas import tpu as pltpu

F32 = jnp.float32
BF16 = jnp.bfloat16

GRID_W = 64
HEAD_DIM = 64
N_KV_HEADS = 4
S5_GROUP_CH = 16
S5_MAX_RE = -1e-4
ROPE_THETA = 10000.0
TOP_K = 8
N_EXPERT_GROUPS = 8
TOPK_GROUPS = 4
ROUTED_SCALE = 2.5
LN_EPS = 1e-5
MOD_EPS = 1e-6
RMS_EPS = 1e-6
N_MOD = 6

LANES = 128
SUBLANES = 8
VMEM_LIMIT_BYTES = 56 * 1024 * 1024

S5_CHUNK = 16
S5_COLS = S5_CHUNK * S5_GROUP_CH
MOE_BLOCK = 512
DISPATCH_TOKENS = 128
DISPATCH_BUFFERS = 3
COMBINE_TOKENS = 128


def _params(sem):
    return pltpu.CompilerParams(dimension_semantics=sem, vmem_limit_bytes=VMEM_LIMIT_BYTES)


def _const_spec(shape):
    nd = len(shape)
    return pl.BlockSpec(shape, lambda *_: (0,) * nd, pipeline_mode=pl.Buffered(1))


def _dot(a, b):
    return jnp.dot(a, b, preferred_element_type=F32)


def _norm_rows(x, eps):
    mu = jnp.mean(x, axis=-1, keepdims=True)
    xc = x - mu
    var = jnp.mean(xc * xc, axis=-1, keepdims=True)
    return xc * lax.rsqrt(var + eps)


def _silu(x):
    return x * jax.nn.sigmoid(x)


def _gelu_tanh(x):
    return 0.5 * x * (1.0 + jnp.tanh(math.sqrt(2.0 / math.pi) * (x + 0.044715 * (x * x * x))))


def _pack_bf16_pairs(v):
    n = v.shape[1] // 2
    bits = lax.bitcast_convert_type(v.astype(BF16).astype(F32), jnp.uint32)
    return bits[:, :n] | (bits[:, n:] >> 16)


def _unpack_bf16_pairs(w):
    hi = lax.bitcast_convert_type(w & jnp.uint32(0xFFFF0000), F32)
    lo = lax.bitcast_convert_type(w << 16, F32)
    return hi, lo


def _mod_kernel(c_ref, w_ref, b_ref, o_ref):
    o_ref[...] = _dot(_silu(c_ref[...]), w_ref[...]) + b_ref[...]


def _mod_call(c_all, w_mod, b_mod):
    rows, d = c_all.shape
    n = w_mod.shape[1]
    tn = d
    return pl.pallas_call(
        _mod_kernel,
        out_shape=jax.ShapeDtypeStruct((rows, n), F32),
        grid=(n // tn,),
        in_specs=[pl.BlockSpec((rows, d), lambda j: (0, 0)),
                  pl.BlockSpec((d, tn), lambda j: (0, j)),
                  pl.BlockSpec((1, tn), lambda j: (0, j))],
        out_specs=pl.BlockSpec((rows, tn), lambda j: (0, j)),
        compiler_params=_params(("arbitrary",)),
        name="mod",
    )(c_all, w_mod, b_mod.reshape(1, n))


def _swap16(t):
    width = t.shape[1]
    lane = lax.broadcasted_iota(jnp.int32, t.shape, 1)
    first = (lane & 16) == 0
    return jnp.where(first, pltpu.roll(t, width - 16, 1), pltpu.roll(t, 16, 1))


def _rms_rope(t, bd_ref, ta, tb):
    msq = _dot((t * t).astype(BF16), bd_ref[...])
    return lax.rsqrt(msq + RMS_EPS) * (t * ta + _swap16(t) * tb)


def _store_padded_heads(t, o_ref):
    rows = t.shape[0]
    lane = lax.broadcasted_iota(jnp.int32, (rows, LANES), 1)
    lo = lane < HEAD_DIM
    for j in range(t.shape[1] // LANES):
        slab = t[:, j * LANES:(j + 1) * LANES]
        a = jnp.where(lo, slab, 0.0)
        b = jnp.where(lo, 0.0, slab)
        pieces = (a, pltpu.roll(a, HEAD_DIM, 1), pltpu.roll(b, HEAD_DIM, 1), b)
        for p, piece in enumerate(pieces):
            c0 = (4 * j + p) * LANES
            o_ref[:, c0:c0 + LANES] = piece.astype(o_ref.dtype)


def _inproj_kernel(x_ref, sh_ref, sc_ref, w_ref, bd_ref, tka_ref, tkb_ref, *rest, d, kvw, has_q):
    if has_q:
        tqa_ref, tqb_ref, s_ref, k_ref, v_ref, q_ref, gs_ref, ga_ref = rest
    else:
        s_ref, k_ref, v_ref = rest
    u = (_norm_rows(x_ref[...], MOD_EPS) * (1.0 + sc_ref[...]) + sh_ref[...]).astype(BF16)
    col = 0
    s_ref[...] = _dot(u, w_ref[:, col:col + d]).astype(s_ref.dtype)
    col += d
    if has_q:
        for c in range(d // kvw):
            q = _dot(u, w_ref[:, col + c * kvw:col + (c + 1) * kvw])
            q_ref[:, c * kvw:(c + 1) * kvw] = _rms_rope(q, bd_ref, tqa_ref[...], tqb_ref[...]).astype(q_ref.dtype)
        col += d
    k = _dot(u, w_ref[:, col:col + kvw])
    _store_padded_heads(_rms_rope(k, bd_ref, tka_ref[...], tkb_ref[...]), k_ref)
    col += kvw
    _store_padded_heads(_dot(u, w_ref[:, col:col + kvw]), v_ref)
    col += kvw
    if has_q:
        gs_ref[...] = jax.nn.sigmoid(_dot(u, w_ref[:, col:col + d])).astype(gs_ref.dtype)
        col += d
        ga_ref[...] = jax.nn.sigmoid(_dot(u, w_ref[:, col:col + d])).astype(ga_ref.dtype)


def _inproj_call(x, shift, scale, w, bd, tka, tkb, tqa=None, tqb=None, *, tm):
    bsz, length, d = x.shape
    has_q = tqa is not None
    kvw = N_KV_HEADS * HEAD_DIM
    per_batch = shift.shape[0] > 1
    tab_rows = tka.shape[0]
    tab_blk = tm if tab_rows > 1 else 1
    mod_spec = pl.BlockSpec((None, 1, d), (lambda b, i: (b, 0, 0)) if per_batch else (lambda b, i: (0, 0, 0)))
    tab_spec = pl.BlockSpec((tab_blk, kvw), (lambda b, i: (i, 0)) if tab_rows > 1 else (lambda b, i: (0, 0)))
    row_spec = lambda width: pl.BlockSpec((None, tm, width), lambda b, i: (b, i, 0))
    in_specs = [row_spec(d), mod_spec, mod_spec, _const_spec(w.shape), _const_spec(bd.shape), tab_spec, tab_spec]
    args = [x, shift, scale, w, bd, tka, tkb]
    widths = [d, 4 * kvw, 4 * kvw]
    if has_q:
        in_specs += [tab_spec, tab_spec]
        args += [tqa, tqb]
        widths += [d, d, d]
    return pl.pallas_call(
        functools.partial(_inproj_kernel, d=d, kvw=kvw, has_q=has_q),
        out_shape=[jax.ShapeDtypeStruct((bsz, length, wd), BF16) for wd in widths],
        grid=(bsz, length // tm),
        in_specs=in_specs,
        out_specs=[row_spec(wd) for wd in widths],
        compiler_params=_params(("parallel", "parallel")),
        name="inproj_lat" if has_q else "inproj_ctx",
    )(*args)


def _rope_tables(seq, gain, scale):
    half = HEAD_DIM // 2
    inv_freq = ROPE_THETA ** (-jnp.arange(0, half, 2, dtype=F32) / half)
    t = jnp.arange(seq, dtype=jnp.int32)
    pos = jnp.stack([(t // GRID_W).astype(F32), (t % GRID_W).astype(F32)], axis=1)
    dim = jnp.arange(HEAD_DIM)
    axis = dim // half
    second = ((dim % half) // (half // 2)) == 1
    freq = inv_freq[dim % (half // 2)]
    ang = pos[:, axis] * freq[None, :]
    partner = jnp.where(second, dim - half // 2, dim + half // 2)
    g = gain.astype(F32)
    ta = jnp.cos(ang) * g[None, :] * scale
    tb = jnp.sin(ang) * jnp.where(second, 1.0, -1.0)[None, :] * g[partner][None, :] * scale
    return jnp.tile(ta, (1, N_KV_HEADS)), jnp.tile(tb, (1, N_KV_HEADS))


def _attn_kernel(q_ref, kc_ref, kl_ref, vc_ref, vl_ref, o_ref):
    nt = (((1,), (1,)), ((), ()))
    rows = q_ref.shape[0]
    lane = lax.broadcasted_iota(jnp.int32, (rows, LANES), 1)
    for j in range(q_ref.shape[1] // LANES):
        qs = q_ref[:, j * LANES:(j + 1) * LANES]
        acc = jnp.zeros((rows, LANES), F32)
        inv = []
        for half in range(2):
            cols = slice(half * LANES, (half + 1) * LANES)
            s_c = lax.dot_general(qs, kc_ref[:, cols], nt, preferred_element_type=F32)
            s_l = lax.dot_general(qs, kl_ref[:, cols], nt, preferred_element_type=F32)
            m = jnp.maximum(jnp.max(s_c, axis=-1, keepdims=True), jnp.max(s_l, axis=-1, keepdims=True))
            e_c = jnp.exp(s_c - m)
            e_l = jnp.exp(s_l - m)
            inv.append(1.0 / (jnp.sum(e_c, axis=-1, keepdims=True) + jnp.sum(e_l, axis=-1, keepdims=True)))
            acc = acc + _dot(e_c.astype(BF16), vc_ref[:, cols]) + _dot(e_l.astype(BF16), vl_ref[:, cols])
        o_ref[:, j * LANES:(j + 1) * LANES] = (acc * jnp.where(lane < HEAD_DIM, inv[0], inv[1])).astype(o_ref.dtype)


def _attn_call(q, kc, kl, vc, vl, *, tq):
    bsz, seq, d = q.shape
    ctx = kc.shape[1]
    gw = d // N_KV_HEADS
    q_spec = pl.BlockSpec((None, tq, gw), lambda b, h, i: (b, i, h))
    kv_spec = lambda length: pl.BlockSpec((None, length, 2 * LANES), lambda b, h, i: (b, 0, h))
    return pl.pallas_call(
        _attn_kernel,
        out_shape=jax.ShapeDtypeStruct((bsz, seq, d), BF16),
        grid=(bsz, N_KV_HEADS, seq // tq),
        in_specs=[q_spec, kv_spec(ctx), kv_spec(seq), kv_spec(ctx), kv_spec(seq)],
        out_specs=q_spec,
        compiler_params=_params(("parallel", "parallel", "arbitrary")),
        name="attention",
    )(q, kc, kl, vc, vl)


def _s5_tables(lam_re, lam_im, log_dt, b_re, b_im, c_re, c_im):
    n = S5_CHUNK
    lam = lax.complex(jnp.minimum(lam_re.astype(F32), S5_MAX_RE), lam_im.astype(F32))
    lam_dt = lam * jnp.exp(log_dt.astype(F32))[..., None]
    b_bar = ((jnp.exp(lam_dt) - 1.0) / lam)[..., None] * lax.complex(b_re.astype(F32), b_im.astype(F32))
    c_mat = lax.complex(c_re.astype(F32), c_im.astype(F32))
    pw = jnp.exp(lam_dt[None] * jnp.arange(n + 1, dtype=F32)[:, None, None, None])
    kern = jnp.real(jnp.einsum('dgcp,jdgp,dgpe->djgce', c_mat, pw[:n], b_bar))
    s_idx = jnp.arange(n)[:, None]
    t_idx = jnp.arange(n)[None, :]

    def toeplitz(k, lag):
        g = k[jnp.clip(lag, 0, n - 1)]
        g = jnp.where((lag >= 0)[:, :, None, None, None], g, 0.0)
        return g.transpose(2, 0, 4, 1, 3)

    toep = toeplitz(kern[0], t_idx - s_idx) + toeplitz(kern[1], s_idx - t_idx)
    groups = toep.shape[0]
    toep = toep.reshape(groups, S5_COLS, S5_COLS)
    wis_f = jnp.einsum('sgp,gpe->gsep', pw[n - 1 - jnp.arange(n), 0], b_bar[0]).reshape(groups, S5_COLS, -1)
    wis_r = jnp.einsum('sgp,gpe->gsep', pw[jnp.arange(n), 1], b_bar[1]).reshape(groups, S5_COLS, -1)
    w1 = jnp.concatenate([toep, jnp.real(wis_f), jnp.real(wis_r), jnp.imag(wis_f), jnp.imag(wis_r)], axis=-1)
    m_f = jnp.einsum('gcp,tgp->gptc', c_mat[0], pw[1 + jnp.arange(n), 0]).reshape(groups, -1, S5_COLS)
    m_r = jnp.einsum('gcp,tgp->gptc', c_mat[1], pw[n - jnp.arange(n), 1]).reshape(groups, -1, S5_COLS)
    wso = jnp.concatenate([jnp.real(m_f), jnp.real(m_r), -jnp.imag(m_f), -jnp.imag(m_r)], axis=1)
    lam_n = jnp.concatenate([pw[n, 0], pw[n, 1]], axis=-1)
    lam_tab = jnp.stack([jnp.real(lam_n), jnp.imag(lam_n)], axis=1)
    return w1.astype(BF16), wso.astype(BF16), lam_tab


def _s5_kernel(u_ref, w1_ref, wso_ref, lam_ref, y_ref, a_ref, xf_re, xf_im, xr_re, xr_im, *, nb, nc_ctx, nc):
    cols = S5_COLS
    half = lam_ref.shape[1] // 2
    a_ref[...] = _dot(u_ref[...], w1_ref[...])
    l_re = lam_ref[0:1, :]
    l_im = lam_ref[1:2, :]
    lo = lax.broadcasted_iota(jnp.int32, (nb, 2 * half), 1) < half

    def step(i, carry):
        x_re, x_im = carry
        i_rev = jnp.where(i < nc_ctx, nc_ctx - 1 - i, nc - 1 + nc_ctx - i)
        rf = pl.ds(pl.multiple_of(i * nb, nb), nb)
        rr = pl.ds(pl.multiple_of(i_rev * nb, nb), nb)
        xf_re[rf, :] = x_re
        xf_im[rf, :] = x_im
        xr_re[rr, :] = x_re
        xr_im[rr, :] = x_im
        s_re = jnp.where(lo, a_ref[rf, cols:cols + 2 * half], a_ref[rr, cols:cols + 2 * half])
        s_im = jnp.where(lo, a_ref[rf, cols + 2 * half:cols + 4 * half], a_ref[rr, cols + 2 * half:cols + 4 * half])
        return l_re * x_re - l_im * x_im + s_re, l_re * x_im + l_im * x_re + s_im

    zero = jnp.zeros((nb, 2 * half), F32)
    lax.fori_loop(0, nc, step, (zero, zero))
    r0 = nc_ctx * nb
    rows = (nc - nc_ctx) * nb
    lo_all = lax.broadcasted_iota(jnp.int32, (rows, 2 * half), 1) < half
    y = a_ref[r0:, 0:cols]
    for k, (f_ref, r_ref) in enumerate(((xf_re, xr_re), (xf_im, xr_im))):
        st = jnp.where(lo_all, f_ref[r0:, :], r_ref[r0:, :])
        hi = st.astype(BF16)
        lo_part = (st - hi.astype(F32)).astype(BF16)
        w = wso_ref[k * 2 * half:(k + 1) * 2 * half, :]
        y = y + _dot(hi, w) + _dot(lo_part, w)
    y_ref[...] = y


def _s5_call(u, w1, wso, lam_tab, *, nb, nc_ctx):
    groups, rows, cols = u.shape
    nc = rows // nb
    out_rows = (nc - nc_ctx) * nb
    st = lam_tab.shape[2]
    return pl.pallas_call(
        functools.partial(_s5_kernel, nb=nb, nc_ctx=nc_ctx, nc=nc),
        out_shape=jax.ShapeDtypeStruct((groups, out_rows, cols), F32),
        grid=(groups,),
        in_specs=[pl.BlockSpec((None, rows, cols), lambda g: (g, 0, 0)),
                  pl.BlockSpec((None,) + w1.shape[1:], lambda g: (g, 0, 0)),
                  pl.BlockSpec((None,) + wso.shape[1:], lambda g: (g, 0, 0)),
                  pl.BlockSpec((None, 2, st), lambda g: (g, 0, 0))],
        out_specs=pl.BlockSpec((None, out_rows, cols), lambda g: (g, 0, 0)),
        scratch_shapes=[pltpu.VMEM((rows, w1.shape[2]), F32)] + [pltpu.VMEM((rows, st), F32)] * 4,
        compiler_params=_params(("parallel",)),
        name="s5",
    )(u, w1, wso, lam_tab)


def _to_group_major(s_ctx, s_lat):
    s_all = jnp.concatenate([s_ctx, s_lat], axis=1)
    bsz, length, width = s_all.shape
    groups = width // S5_GROUP_CH
    nc = length // S5_CHUNK
    t = s_all.reshape(bsz, nc, S5_CHUNK, groups, S5_GROUP_CH).transpose(3, 1, 0, 2, 4)
    return t.reshape(groups, nc * bsz, S5_COLS)


def _to_token_major(y, bsz):
    groups, rows, _ = y.shape
    nc = rows // bsz
    t = y.reshape(groups, nc, bsz, S5_CHUNK, S5_GROUP_CH).transpose(2, 1, 3, 0, 4)
    return t.reshape(bsz, nc * S5_CHUNK, groups * S5_GROUP_CH)


def _mixout_kernel(y_ref, s_ref, at_ref, gs_ref, ga_ref, x_ref, g1_ref, sh2_ref, sc2_ref, dsk_ref,
                   wglu_ref, wbs_ref, wba_ref, wout_ref, lng_ref, lnb_ref, wrs_ref, wsd_ref,
                   x1_ref, h_ref, shared_ref, scores_ref, *, alpha, n_exp):
    y = y_ref[...] + s_ref[...].astype(F32) * dsk_ref[...]
    z = _gelu_tanh(y)
    ssm = z * jax.nn.sigmoid(_dot(z.astype(BF16), wglu_ref[...]))
    merged = (gs_ref[...].astype(F32) * _dot(ssm.astype(BF16), wbs_ref[...])
              + ga_ref[...].astype(F32) * _dot(at_ref[...], wba_ref[...]))
    y_mix = _dot(merged.astype(BF16), wout_ref[...])
    x1 = _norm_rows(alpha * x_ref[...] + g1_ref[...] * y_mix, LN_EPS) * lng_ref[...] + lnb_ref[...]
    x1_ref[...] = x1
    h = _norm_rows(x1, MOD_EPS) * (1.0 + sc2_ref[...]) + sh2_ref[...]
    h_ref[...] = _pack_bf16_pairs(h)
    rs = _dot(h.astype(BF16), wrs_ref[...])
    scores_ref[...] = jax.nn.sigmoid(rs[:, :n_exp])
    sh_hidden = wsd_ref.shape[0]
    hid = _silu(rs[:, n_exp:n_exp + sh_hidden]) * rs[:, n_exp + sh_hidden:]
    shared_ref[...] = _dot(hid.astype(BF16), wsd_ref[...])


def _mixout_call(y, s, attn, gs, ga, x, g1, sh2, sc2, dsk, wglu, wbs, wba, wout, lng, lnb, wrs, wsd,
                 *, tm, alpha, n_exp):
    bsz, seq, d = x.shape
    row = lambda width: pl.BlockSpec((None, tm, width), lambda b, i: (b, i, 0))
    mod = pl.BlockSpec((None, 1, d), lambda b, i: (b, 0, 0))
    consts = [dsk, wglu, wbs, wba, wout, lng, lnb, wrs, wsd]
    return pl.pallas_call(
        functools.partial(_mixout_kernel, alpha=alpha, n_exp=n_exp),
        out_shape=[jax.ShapeDtypeStruct((bsz, seq, d), F32), jax.ShapeDtypeStruct((bsz, seq, d // 2), jnp.uint32),
                   jax.ShapeDtypeStruct((bsz, seq, d), F32), jax.ShapeDtypeStruct((bsz, seq, n_exp), F32)],
        grid=(bsz, seq // tm),
        in_specs=[row(d)] * 6 + [mod] * 3 + [_const_spec(a.shape) for a in consts],
        out_specs=[row(d), row(d // 2), row(d), row(n_exp)],
        compiler_params=_params(("parallel", "parallel")),
        name="mixout",
    )(y, s, attn, gs, ga, x, g1, sh2, sc2, *consts)


def _segment_allreduce(x, lane, width, op):
    total = x.shape[1]
    d = 1
    while d < width:
        partner = jnp.where((lane & d) == 0, pltpu.roll(x, total - d, 1), pltpu.roll(x, d, 1))
        x = op(x, partner)
        d *= 2
    return x


def _route_kernel(s_ref, bias_ref, ltri_ref, idx_ref, gate_ref, rank_ref, counts_ref, base_ref):
    @pl.when(pl.program_id(0) == 0)
    def _():
        base_ref[...] = jnp.zeros_like(base_ref)

    s = s_ref[...]
    tm, n_exp = s.shape
    per_group = n_exp // N_EXPERT_GROUPS
    neg = -jnp.inf
    b = s + bias_ref[...]
    lane = lax.broadcasted_iota(jnp.int32, (tm, n_exp), 1)
    lane_f = lane.astype(F32)
    m1 = _segment_allreduce(b, lane, per_group, jnp.maximum)
    is1 = b == m1
    n1 = _segment_allreduce(jnp.where(is1, 1.0, 0.0), lane, per_group, jnp.add)
    m2 = _segment_allreduce(jnp.where(is1, neg, b), lane, per_group, jnp.maximum)
    gscore = m1 + jnp.where(n1 >= 2.0, m1, m2)
    gid = lane // per_group
    beaten = jnp.zeros((tm, n_exp), F32)
    for j in range(1, N_EXPERT_GROUPS):
        other = pltpu.roll(gscore, j * per_group, 1)
        oid = (gid - j) & (N_EXPERT_GROUPS - 1)
        beaten = beaten + jnp.where((other > gscore) | ((other == gscore) & (oid < gid)), 1.0, 0.0)
    masked = jnp.where(beaten < float(TOPK_GROUPS), b, neg)
    lane_k = lax.broadcasted_iota(jnp.int32, (tm, TOP_K), 1)
    idx = jnp.zeros((tm, TOP_K), F32)
    val = jnp.zeros((tm, TOP_K), F32)
    chosen = jnp.zeros((tm, n_exp), F32)
    for k in range(TOP_K):
        m = jnp.max(masked, axis=1, keepdims=True)
        first = jnp.min(jnp.where(masked == m, lane_f, float(n_exp)), axis=1, keepdims=True)
        sel = lane_f == first
        idx = jnp.where(lane_k == k, first, idx)
        val = jnp.where(lane_k == k, jnp.sum(jnp.where(sel, s, 0.0), axis=1, keepdims=True), val)
        chosen = jnp.where(sel, 1.0, chosen)
        masked = jnp.where(sel, neg, masked)
    before = _dot(ltri_ref[...], chosen.astype(BF16)) + base_ref[...]
    rank = jnp.zeros((tm, TOP_K), F32)
    for k in range(TOP_K):
        sel = lane_f == idx[:, k:k + 1]
        rank = jnp.where(lane_k == k, jnp.sum(jnp.where(sel, before, 0.0), axis=1, keepdims=True), rank)
    idx_ref[...] = idx.astype(jnp.int32)
    rank_ref[...] = rank.astype(jnp.int32)
    gate_ref[...] = val / jnp.sum(val, axis=1, keepdims=True) * ROUTED_SCALE
    base_ref[...] = base_ref[...] + jnp.sum(chosen, axis=0, keepdims=True)
    counts_ref[...] = base_ref[...]


def _route_call(scores, bias, *, tm):
    tokens, n_exp = scores.shape
    r_id = lax.broadcasted_iota(jnp.int32, (tm, tm), 0)
    c_id = lax.broadcasted_iota(jnp.int32, (tm, tm), 1)
    ltri = (c_id < r_id).astype(BF16)
    small = pl.BlockSpec((tm, TOP_K), lambda i: (i, 0))
    return pl.pallas_call(
        _route_kernel,
        out_shape=[jax.ShapeDtypeStruct((tokens, TOP_K), jnp.int32), jax.ShapeDtypeStruct((tokens, TOP_K), F32),
                   jax.ShapeDtypeStruct((tokens, TOP_K), jnp.int32), jax.ShapeDtypeStruct((1, n_exp), F32)],
        grid=(tokens // tm,),
        in_specs=[pl.BlockSpec((tm, n_exp), lambda i: (i, 0)), _const_spec((1, n_exp)), _const_spec((tm, tm))],
        out_specs=[small, small, small, pl.BlockSpec((1, n_exp), lambda i: (0, 0))],
        scratch_shapes=[pltpu.VMEM((1, n_exp), F32)],
        compiler_params=_params(("arbitrary",)),
        name="route",
    )(scores, bias.astype(F32).reshape(1, n_exp), ltri)


def _slot_kernel(idx_ref, rank_ref, start_ref, pos_ref):
    idx = idx_ref[...]
    tm = idx.shape[0]
    n_exp = start_ref.shape[1]
    lane = lax.broadcasted_iota(jnp.int32, (tm, n_exp), 1)
    lane_k = lax.broadcasted_iota(jnp.int32, (tm, TOP_K), 1)
    pos = rank_ref[...]
    for k in range(TOP_K):
        start = jnp.sum(jnp.where(lane == idx[:, k:k + 1], start_ref[...], 0.0), axis=1, keepdims=True)
        pos = pos + jnp.where(lane_k == k, start.astype(jnp.int32), 0)
    pos_ref[...] = pos


def _slot_call(idx, rank, pad_start, *, tm):
    tokens = idx.shape[0]
    n_exp = pad_start.shape[0]
    small = pl.BlockSpec((tm, TOP_K), lambda i: (i, 0))
    return pl.pallas_call(
        _slot_kernel,
        out_shape=jax.ShapeDtypeStruct((tokens, TOP_K), jnp.int32),
        grid=(tokens // tm,),
        in_specs=[small, small, _const_spec((1, n_exp))],
        out_specs=small,
        compiler_params=_params(("parallel",)),
        name="slots",
    )(idx, rank, pad_start.astype(F32).reshape(1, n_exp))


def _dispatch_kernel(cnt_ref, start_ref, padded_ref, npad_ref, pos_ref, h_hbm, xs_hbm, hbuf, zrow, lsem, ssem, zsem):
    i = pl.program_id(0)
    n = pl.num_programs(0)
    nbuf, td = hbuf.shape[:2]
    n_exp = cnt_ref.shape[0]
    slot = i % nbuf

    def zero_copy(row):
        return pltpu.make_async_copy(zrow, xs_hbm.at[pl.ds(row, 1), :], zsem)

    def load(step, s):
        return pltpu.make_async_copy(h_hbm.at[pl.ds(step * td, td), :], hbuf.at[s], lsem.at[s])

    def wait_scatters(s):
        rows = td * TOP_K
        pltpu.make_async_copy(h_hbm.at[pl.ds(0, rows), :], xs_hbm.at[pl.ds(0, rows), :], ssem.at[s]).wait()

    @pl.when(i == 0)
    def _():
        zrow[...] = jnp.zeros_like(zrow)

        def per_expert(e, carry):
            def one(r, c2):
                zero_copy(start_ref[e] + r).start()
                return c2
            return lax.fori_loop(cnt_ref[e], padded_ref[e], one, carry)
        lax.fori_loop(0, n_exp, per_expert, 0)
        load(0, 0).start()

    @pl.when(i >= nbuf - 1)
    def _():
        wait_scatters((i + 1) % nbuf)

    @pl.when(i + 1 < n)
    def _():
        load(i + 1, (i + 1) % nbuf).start()

    load(i, slot).wait()

    for r in range(td * TOP_K):
        pltpu.make_async_copy(hbuf.at[slot, pl.ds(r // TOP_K, 1), :], xs_hbm.at[pl.ds(pos_ref[0, r], 1), :],
                              ssem.at[slot]).start()

    @pl.when(i == n - 1)
    def _():
        for back in range(nbuf - 1):
            @pl.when(i >= back)
            def _():
                wait_scatters((i - back) % nbuf)

        def one(r, c2):
            zero_copy(0).wait()
            return c2
        lax.fori_loop(0, npad_ref[0], one, 0)


def _dispatch_call(counts, pad_start, padded, n_pad, pos_flat, hp, n_slots):
    n_tiles, _, rows = pos_flat.shape
    width = hp.shape[1]
    grid_spec = pltpu.PrefetchScalarGridSpec(
        num_scalar_prefetch=4,
        grid=(n_tiles,),
        in_specs=[pl.BlockSpec((None, 1, rows), lambda i, *_: (i, 0, 0), memory_space=pltpu.SMEM),
                  pl.BlockSpec(memory_space=pl.ANY)],
        out_specs=pl.BlockSpec(memory_space=pl.ANY),
        scratch_shapes=[pltpu.VMEM((DISPATCH_BUFFERS, rows // TOP_K, width), hp.dtype), pltpu.VMEM((1, width), hp.dtype),
                        pltpu.SemaphoreType.DMA((DISPATCH_BUFFERS,)), pltpu.SemaphoreType.DMA((DISPATCH_BUFFERS,)),
                        pltpu.SemaphoreType.DMA],
    )
    return pl.pallas_call(
        _dispatch_kernel,
        out_shape=jax.ShapeDtypeStruct((n_slots, width), hp.dtype),
        grid_spec=grid_spec,
        compiler_params=_params(("arbitrary",)),
        name="dispatch",
    )(counts, pad_start, padded, n_pad, pos_flat, hp)


def _experts_kernel(be_ref, nact_ref, x_ref, wg_ref, wu_ref, wd_ref, y_ref):
    @pl.when(pl.program_id(0) < nact_ref[0])
    def _():
        half = x_ref.shape[1]
        x_hi, x_lo = (v.astype(BF16) for v in _unpack_bf16_pairs(x_ref[...]))
        hg = _dot(x_hi, wg_ref[:half, :]) + _dot(x_lo, wg_ref[half:, :])
        hu = _dot(x_hi, wu_ref[:half, :]) + _dot(x_lo, wu_ref[half:, :])
        y_ref[...] = _pack_bf16_pairs(_dot((_silu(hg) * hu).astype(BF16), wd_ref[...]))


def _experts_call(block_expert, n_active, xs, wg, wu, wd):
    n_blocks = block_expert.shape[0]
    n_slots, half = xs.shape
    bm = n_slots // n_blocks
    d, hidden = wg.shape[1:]
    blk = lambda i, be, na: (jnp.minimum(i, na[0] - 1), 0)
    wsel = lambda i, be, na: (be[jnp.minimum(i, na[0] - 1)], 0, 0)
    grid_spec = pltpu.PrefetchScalarGridSpec(
        num_scalar_prefetch=2,
        grid=(n_blocks,),
        in_specs=[pl.BlockSpec((bm, half), blk),
                  pl.BlockSpec((None, d, hidden), wsel),
                  pl.BlockSpec((None, d, hidden), wsel),
                  pl.BlockSpec((None, hidden, d), wsel)],
        out_specs=pl.BlockSpec((bm, half), blk),
    )
    return pl.pallas_call(
        _experts_kernel,
        out_shape=jax.ShapeDtypeStruct((n_slots, half), xs.dtype),
        grid_spec=grid_spec,
        compiler_params=_params(("arbitrary",)),
        name="experts",
    )(block_expert, n_active, xs, wg, wu, wd)


def _row_gather_start(idx_ref, n_rows, src_hbm, dst_ref, sem):
    for r in range(n_rows):
        pltpu.make_async_copy(src_hbm.at[pl.ds(idx_ref[0, r], 1), :], dst_ref.at[pl.ds(r, 1), :], sem).start()


def _row_gather_wait(n_rows, src_hbm, dst_ref, sem):
    pltpu.make_async_copy(src_hbm.at[pl.ds(0, n_rows), :], dst_ref, sem).wait()


def _combine_kernel(pos_ref, pos_next_ref, y_hbm, gate_ref, shared_ref, x1_ref, g2_ref, lng_ref, lnb_ref, o_ref,
                    buf, sems, *, alpha):
    i = pl.program_id(0)
    n = pl.num_programs(0)
    rows = buf.shape[1]
    tc = o_ref.shape[0]
    slot = i % 2

    @pl.when(i == 0)
    def _():
        _row_gather_start(pos_ref, rows, y_hbm, buf.at[0], sems.at[0])

    @pl.when(i + 1 < n)
    def _():
        _row_gather_start(pos_next_ref, rows, y_hbm, buf.at[1 - slot], sems.at[1 - slot])

    _row_gather_wait(rows, y_hbm, buf.at[slot], sems.at[slot])
    half = buf.shape[2]
    f_hi = shared_ref[:, :half]
    f_lo = shared_ref[:, half:]
    for k in range(rows // tc):
        y_hi, y_lo = _unpack_bf16_pairs(buf[slot, k * tc:(k + 1) * tc, :])
        g = gate_ref[:, k:k + 1]
        f_hi = f_hi + g * y_hi
        f_lo = f_lo + g * y_lo
    f = jnp.concatenate([f_hi, f_lo], axis=1)
    o_ref[...] = _norm_rows(alpha * x1_ref[...] + g2_ref[...] * f, LN_EPS) * lng_ref[...] + lnb_ref[...]


def _combine_call(pos_tiles, y_slots, gates, shared, x1, g2, lng, lnb, *, alpha, tiles_per_batch):
    n_tiles, _, rows = pos_tiles.shape
    tokens, d = x1.shape
    tc = tokens // n_tiles
    last = n_tiles - 1
    row = pl.BlockSpec((tc, d), lambda i: (i, 0))
    return pl.pallas_call(
        functools.partial(_combine_kernel, alpha=alpha),
        out_shape=jax.ShapeDtypeStruct((tokens, d), F32),
        grid=(n_tiles,),
        in_specs=[
            pl.BlockSpec((None, 1, rows), lambda i: (i, 0, 0), memory_space=pltpu.SMEM),
            pl.BlockSpec((None, 1, rows), lambda i: (jnp.minimum(i + 1, last), 0, 0), memory_space=pltpu.SMEM),
            pl.BlockSpec(memory_space=pl.ANY),
            pl.BlockSpec((tc, TOP_K), lambda i: (i, 0)),
            row, row,
            pl.BlockSpec((None, 1, d), lambda i: (i // tiles_per_batch, 0, 0)),
            _const_spec(lng.shape), _const_spec(lnb.shape),
        ],
        out_specs=row,
        scratch_shapes=[pltpu.VMEM((2, rows, y_slots.shape[1]), y_slots.dtype), pltpu.SemaphoreType.DMA((2,))],
        compiler_params=_params(("arbitrary",)),
        name="combine",
    )(pos_tiles, pos_tiles, y_slots, gates, shared, x1, g2, lng, lnb)


def kernel(x, c, ctx, c_ctx, w_mod, b_mod, w_in, s5_lam_re, s5_lam_im, s5_log_dt, s5_b_re, s5_b_im, s5_c_re, s5_c_im, s5_d, w_glu, q_norm_g, k_norm_g, w_branch_ssm, w_branch_attn, w_out, ln1_g, ln1_b, w_router, router_bias, w_exp_gate, w_exp_up, w_exp_down, w_sh_gate, w_sh_up, w_sh_down, ln2_g, ln2_b):
    depth = w_mod.shape[0]
    assert depth == 1, "single-layer block: context outputs are never needed"
    bsz, seq, d = x.shape
    ctx_len = ctx.shape[1]
    kvw = N_KV_HEADS * HEAD_DIM
    n_exp = w_router.shape[2]
    alpha = (2.0 * depth) ** 0.25
    assert seq % GRID_W == 0 and seq % S5_CHUNK == 0 and ctx_len % S5_CHUNK == 0 and bsz % SUBLANES == 0
    lay = 0

    pad = (-(bsz + 1)) % SUBLANES
    c_all = jnp.concatenate([c, c_ctx[None, :], jnp.zeros((pad, d), F32)], axis=0)
    mod = _mod_call(c_all, w_mod[lay], b_mod[lay])
    mod_lat = mod[:bsz].reshape(bsz, N_MOD, 1, d)
    sh1, sc1, g1, sh2, sc2, g2 = (mod_lat[:, k] for k in range(N_MOD))
    mod_ctx = mod[bsz].reshape(N_MOD, 1, 1, d)

    w_in_l = w_in[lay].astype(BF16)
    w_ctx = jnp.concatenate([w_in_l[:, :d], w_in_l[:, 2 * d:2 * d + 2 * kvw]], axis=1)
    head_id = jnp.arange(kvw) // HEAD_DIM
    bd = jnp.where(head_id[:, None] == head_id[None, :], 1.0 / HEAD_DIM, 0.0).astype(BF16)
    tqa, tqb = _rope_tables(seq, q_norm_g[lay], HEAD_DIM ** -0.5)
    tka, tkb = _rope_tables(seq, k_norm_g[lay], 1.0)
    tca = jnp.tile(k_norm_g[lay].astype(F32), N_KV_HEADS)[None, :]
    tm_lat = min(512, seq)
    s_lat, k_lat, v_lat, q_lat, gs_lat, ga_lat = _inproj_call(x, sh1, sc1, w_in_l, bd, tka, tkb, tqa, tqb, tm=tm_lat)
    s_ctx, k_ctx, v_ctx = _inproj_call(ctx, mod_ctx[0], mod_ctx[1], w_ctx, bd, tca, jnp.zeros_like(tca),
                                       tm=min(256, ctx_len))

    w1, wso, lam_tab = _s5_tables(s5_lam_re[lay], s5_lam_im[lay], s5_log_dt[lay], s5_b_re[lay], s5_b_im[lay],
                                  s5_c_re[lay], s5_c_im[lay])
    y_groups = _s5_call(_to_group_major(s_ctx, s_lat), w1, wso, lam_tab, nb=bsz, nc_ctx=ctx_len // S5_CHUNK)
    y_s5 = _to_token_major(y_groups, bsz)

    attn = _attn_call(q_lat, k_ctx, k_lat, v_ctx, v_lat, tq=min(512, seq))

    row = lambda v: v.astype(F32).reshape(1, -1)
    wrs = jnp.concatenate([w_router[lay], w_sh_gate[lay], w_sh_up[lay]], axis=1).astype(BF16)
    x1, hp, shared, scores = _mixout_call(
        y_s5, s_lat, attn, gs_lat, ga_lat, x, g1, sh2, sc2, row(s5_d[lay]),
        w_glu[lay].astype(BF16), w_branch_ssm[lay].astype(BF16), w_branch_attn[lay].astype(BF16),
        w_out[lay].astype(BF16), row(ln1_g[lay]), row(ln1_b[lay]), wrs, w_sh_down[lay].astype(BF16),
        tm=min(256, seq), alpha=alpha, n_exp=n_exp)

    tokens = bsz * seq
    bm = MOE_BLOCK
    idx, gates, rank, counts_f = _route_call(scores.reshape(tokens, n_exp), router_bias[lay], tm=min(256, tokens))
    counts = counts_f.reshape(n_exp).astype(jnp.int32)
    padded = (counts + bm - 1) // bm * bm
    pad_end = jnp.cumsum(padded)
    pad_start = pad_end - padded
    n_blocks = (tokens * TOP_K + n_exp * (bm - 1) + bm - 1) // bm
    block_start = jnp.arange(n_blocks, dtype=jnp.int32) * bm
    block_expert = jnp.minimum(jnp.sum((pad_end[None, :] <= block_start[:, None]).astype(jnp.int32), axis=1), n_exp - 1)
    n_active = (pad_end[-1] // bm).reshape(1)
    n_pad = jnp.sum(padded - counts).reshape(1)
    pos = _slot_call(idx, rank, pad_start, tm=min(512, tokens))
    td = min(DISPATCH_TOKENS, tokens)
    xs = _dispatch_call(counts, pad_start, padded, n_pad, pos.reshape(tokens // td, 1, td * TOP_K),
                        hp.reshape(tokens, d // 2), n_blocks * bm)
    y_slots = _experts_call(block_expert, n_active, xs, w_exp_gate[lay].astype(BF16), w_exp_up[lay].astype(BF16),
                            w_exp_down[lay].astype(BF16))

    tc = min(COMBINE_TOKENS, seq)
    n_tiles = tokens // tc
    pos_tiles = pos.reshape(n_tiles, tc, TOP_K).transpose(0, 2, 1).reshape(n_tiles, 1, TOP_K * tc)
    out = _combine_call(pos_tiles, y_slots, gates, shared.reshape(tokens, d), x1.reshape(tokens, d), g2,
                        row(ln2_g[lay]), row(ln2_b[lay]), alpha=alpha, tiles_per_batch=seq // tc)
    return out.reshape(bsz, seq, d)
```

```python
import functools
import math

import jax
import jax.numpy as jnp
from jax import lax
from jax.experimental import pallas as pl
from jax.experimental.pallas import tpu as pltpu

F32 = jnp.float32
BF16 = jnp.bfloat16

GRID_W = 64
HEAD_DIM = 64
N_KV_HEADS = 4
S5_GROUP_CH = 16
S5_MAX_RE = -1e-4
ROPE_THETA = 10000.0
TOP_K = 8
N_EXPERT_GROUPS = 8
TOPK_GROUPS = 4
ROUTED_SCALE = 2.5
LN_EPS = 1e-5
MOD_EPS = 1e-6
RMS_EPS = 1e-6
N_MOD = 6

LANES = 128
SUBLANES = 8
VMEM_LIMIT_BYTES = 56 * 1024 * 1024

S5_CHUNK = 16
S5_COLS = S5_CHUNK * S5_GROUP_CH
MOE_BLOCK = 512
ROUTE_TOKENS = 512
DISPATCH_TOKENS = 128
DISPATCH_BUFFERS = 3
COMBINE_TOKENS = 128


def _params(sem):
    return pltpu.CompilerParams(dimension_semantics=sem, vmem_limit_bytes=VMEM_LIMIT_BYTES)


def _const_spec(shape):
    nd = len(shape)
    return pl.BlockSpec(shape, lambda *_: (0,) * nd, pipeline_mode=pl.Buffered(1))


def _dot(a, b):
    return jnp.dot(a, b, preferred_element_type=F32)


def _norm_rows(x, eps):
    mu = jnp.mean(x, axis=-1, keepdims=True)
    xc = x - mu
    var = jnp.mean(xc * xc, axis=-1, keepdims=True)
    return xc * lax.rsqrt(var + eps)


def _silu(x):
    return x * jax.nn.sigmoid(x)


def _gelu_tanh(x):
    return 0.5 * x * (1.0 + jnp.tanh(math.sqrt(2.0 / math.pi) * (x + 0.044715 * (x * x * x))))


def _pack_bf16_pairs(v):
    n = v.shape[1] // 2
    bits = lax.bitcast_convert_type(v.astype(BF16).astype(F32), jnp.uint32)
    return bits[:, :n] | (bits[:, n:] >> 16)


def _unpack_bf16_pairs(w):
    hi = lax.bitcast_convert_type(w & jnp.uint32(0xFFFF0000), F32)
    lo = lax.bitcast_convert_type(w << 16, F32)
    return hi, lo


def _mod_kernel(c_ref, w_ref, b_ref, o_ref):
    o_ref[...] = _dot(_silu(c_ref[...]), w_ref[...]) + b_ref[...]


def _mod_call(c_all, w_mod, b_mod):
    rows, d = c_all.shape
    n = w_mod.shape[1]
    tn = d
    return pl.pallas_call(
        _mod_kernel,
        out_shape=jax.ShapeDtypeStruct((rows, n), F32),
        grid=(n // tn,),
        in_specs=[pl.BlockSpec((rows, d), lambda j: (0, 0)),
                  pl.BlockSpec((d, tn), lambda j: (0, j)),
                  pl.BlockSpec((1, tn), lambda j: (0, j))],
        out_specs=pl.BlockSpec((rows, tn), lambda j: (0, j)),
        compiler_params=_params(("arbitrary",)),
        name="mod",
    )(c_all, w_mod, b_mod.reshape(1, n))


def _swap16(t):
    width = t.shape[1]
    lane = lax.broadcasted_iota(jnp.int32, t.shape, 1)
    first = (lane & 16) == 0
    return jnp.where(first, pltpu.roll(t, width - 16, 1), pltpu.roll(t, 16, 1))


def _rms_rope(t, bd_ref, ta, tb):
    msq = _dot((t * t).astype(BF16), bd_ref[...])
    return lax.rsqrt(msq + RMS_EPS) * (t * ta + _swap16(t) * tb)


def _store_padded_heads(t, o_ref):
    rows = t.shape[0]
    lane = lax.broadcasted_iota(jnp.int32, (rows, LANES), 1)
    lo = lane < HEAD_DIM
    for j in range(t.shape[1] // LANES):
        slab = t[:, j * LANES:(j + 1) * LANES]
        a = jnp.where(lo, slab, 0.0)
        b = jnp.where(lo, 0.0, slab)
        pieces = (a, pltpu.roll(a, HEAD_DIM, 1), pltpu.roll(b, HEAD_DIM, 1), b)
        for p, piece in enumerate(pieces):
            c0 = (4 * j + p) * LANES
            o_ref[:, c0:c0 + LANES] = piece.astype(o_ref.dtype)


def _inproj_kernel(x_ref, sh_ref, sc_ref, w_ref, bd_ref, tka_ref, tkb_ref, *rest, d, kvw, has_q):
    if has_q:
        tqa_ref, tqb_ref, s_ref, k_ref, v_ref, q_ref, gs_ref, ga_ref = rest
    else:
        s_ref, k_ref, v_ref = rest
    u = (_norm_rows(x_ref[...], MOD_EPS) * (1.0 + sc_ref[...]) + sh_ref[...]).astype(BF16)
    col = 0
    s_ref[...] = _dot(u, w_ref[:, col:col + d]).astype(s_ref.dtype)
    col += d
    if has_q:
        for c in range(d // kvw):
            q = _dot(u, w_ref[:, col + c * kvw:col + (c + 1) * kvw])
            q_ref[:, c * kvw:(c + 1) * kvw] = _rms_rope(q, bd_ref, tqa_ref[...], tqb_ref[...]).astype(q_ref.dtype)
        col += d
    k = _dot(u, w_ref[:, col:col + kvw])
    _store_padded_heads(_rms_rope(k, bd_ref, tka_ref[...], tkb_ref[...]), k_ref)
    col += kvw
    _store_padded_heads(_dot(u, w_ref[:, col:col + kvw]), v_ref)
    col += kvw
    if has_q:
        gs_ref[...] = jax.nn.sigmoid(_dot(u, w_ref[:, col:col + d])).astype(gs_ref.dtype)
        col += d
        ga_ref[...] = jax.nn.sigmoid(_dot(u, w_ref[:, col:col + d])).astype(ga_ref.dtype)


def _inproj_call(x, shift, scale, w, bd, tka, tkb, tqa=None, tqb=None, *, tm):
    bsz, length, d = x.shape
    has_q = tqa is not None
    kvw = N_KV_HEADS * HEAD_DIM
    per_batch = shift.shape[0] > 1
    tab_rows = tka.shape[0]
    tab_blk = tm if tab_rows > 1 else 1
    mod_spec = pl.BlockSpec((None, 1, d), (lambda b, i: (b, 0, 0)) if per_batch else (lambda b, i: (0, 0, 0)))
    tab_spec = pl.BlockSpec((tab_blk, kvw), (lambda b, i: (i, 0)) if tab_rows > 1 else (lambda b, i: (0, 0)))
    row_spec = lambda width: pl.BlockSpec((None, tm, width), lambda b, i: (b, i, 0))
    in_specs = [row_spec(d), mod_spec, mod_spec, _const_spec(w.shape), _const_spec(bd.shape), tab_spec, tab_spec]
    args = [x, shift, scale, w, bd, tka, tkb]
    widths = [d, 4 * kvw, 4 * kvw]
    if has_q:
        in_specs += [tab_spec, tab_spec]
        args += [tqa, tqb]
        widths += [d, d, d]
    return pl.pallas_call(
        functools.partial(_inproj_kernel, d=d, kvw=kvw, has_q=has_q),
        out_shape=[jax.ShapeDtypeStruct((bsz, length, wd), BF16) for wd in widths],
        grid=(bsz, length // tm),
        in_specs=in_specs,
        out_specs=[row_spec(wd) for wd in widths],
        compiler_params=_params(("parallel", "parallel")),
        name="inproj_lat" if has_q else "inproj_ctx",
    )(*args)


def _rope_tables(seq, gain, scale):
    half = HEAD_DIM // 2
    inv_freq = ROPE_THETA ** (-jnp.arange(0, half, 2, dtype=F32) / half)
    t = jnp.arange(seq, dtype=jnp.int32)
    pos = jnp.stack([(t // GRID_W).astype(F32), (t % GRID_W).astype(F32)], axis=1)
    dim = jnp.arange(HEAD_DIM)
    axis = dim // half
    second = ((dim % half) // (half // 2)) == 1
    freq = inv_freq[dim % (half // 2)]
    ang = pos[:, axis] * freq[None, :]
    partner = jnp.where(second, dim - half // 2, dim + half // 2)
    g = gain.astype(F32)
    ta = jnp.cos(ang) * g[None, :] * scale
    tb = jnp.sin(ang) * jnp.where(second, 1.0, -1.0)[None, :] * g[partner][None, :] * scale
    return jnp.tile(ta, (1, N_KV_HEADS)), jnp.tile(tb, (1, N_KV_HEADS))


def _attn_kernel(q_ref, kc_ref, kl_ref, vc_ref, vl_ref, o_ref):
    nt = (((1,), (1,)), ((), ()))
    rows = q_ref.shape[0]
    lane = lax.broadcasted_iota(jnp.int32, (rows, LANES), 1)
    for j in range(q_ref.shape[1] // LANES):
        qs = q_ref[:, j * LANES:(j + 1) * LANES]
        acc = jnp.zeros((rows, LANES), F32)
        inv = []
        for half in range(2):
            cols = slice(half * LANES, (half + 1) * LANES)
            s_c = lax.dot_general(qs, kc_ref[:, cols], nt, preferred_element_type=F32)
            s_l = lax.dot_general(qs, kl_ref[:, cols], nt, preferred_element_type=F32)
            m = jnp.maximum(jnp.max(s_c, axis=-1, keepdims=True), jnp.max(s_l, axis=-1, keepdims=True))
            e_c = jnp.exp(s_c - m)
            e_l = jnp.exp(s_l - m)
            inv.append(1.0 / (jnp.sum(e_c, axis=-1, keepdims=True) + jnp.sum(e_l, axis=-1, keepdims=True)))
            acc = acc + _dot(e_c.astype(BF16), vc_ref[:, cols]) + _dot(e_l.astype(BF16), vl_ref[:, cols])
        o_ref[:, j * LANES:(j + 1) * LANES] = (acc * jnp.where(lane < HEAD_DIM, inv[0], inv[1])).astype(o_ref.dtype)


def _attn_call(q, kc, kl, vc, vl, *, tq):
    bsz, seq, d = q.shape
    ctx = kc.shape[1]
    gw = d // N_KV_HEADS
    q_spec = pl.BlockSpec((None, tq, gw), lambda b, h, i: (b, i, h))
    kv_spec = lambda length: pl.BlockSpec((None, length, 2 * LANES), lambda b, h, i: (b, 0, h))
    return pl.pallas_call(
        _attn_kernel,
        out_shape=jax.ShapeDtypeStruct((bsz, seq, d), BF16),
        grid=(bsz, N_KV_HEADS, seq // tq),
        in_specs=[q_spec, kv_spec(ctx), kv_spec(seq), kv_spec(ctx), kv_spec(seq)],
        out_specs=q_spec,
        compiler_params=_params(("parallel", "parallel", "arbitrary")),
        name="attention",
    )(q, kc, kl, vc, vl)


def _s5_tables(lam_re, lam_im, log_dt, b_re, b_im, c_re, c_im):
    n = S5_CHUNK
    lam = lax.complex(jnp.minimum(lam_re.astype(F32), S5_MAX_RE), lam_im.astype(F32))
    lam_dt = lam * jnp.exp(log_dt.astype(F32))[..., None]
    b_bar = ((jnp.exp(lam_dt) - 1.0) / lam)[..., None] * lax.complex(b_re.astype(F32), b_im.astype(F32))
    c_mat = lax.complex(c_re.astype(F32), c_im.astype(F32))
    pw = jnp.exp(lam_dt[None] * jnp.arange(n + 1, dtype=F32)[:, None, None, None])
    kern = jnp.real(jnp.einsum('dgcp,jdgp,dgpe->djgce', c_mat, pw[:n], b_bar))
    s_idx = jnp.arange(n)[:, None]
    t_idx = jnp.arange(n)[None, :]

    def toeplitz(k, lag):
        g = k[jnp.clip(lag, 0, n - 1)]
        g = jnp.where((lag >= 0)[:, :, None, None, None], g, 0.0)
        return g.transpose(2, 0, 4, 1, 3)

    toep = toeplitz(kern[0], t_idx - s_idx) + toeplitz(kern[1], s_idx - t_idx)
    groups = toep.shape[0]
    toep = toep.reshape(groups, S5_COLS, S5_COLS)
    wis_f = jnp.einsum('sgp,gpe->gsep', pw[n - 1 - jnp.arange(n), 0], b_bar[0]).reshape(groups, S5_COLS, -1)
    wis_r = jnp.einsum('sgp,gpe->gsep', pw[jnp.arange(n), 1], b_bar[1]).reshape(groups, S5_COLS, -1)
    w1 = jnp.concatenate([toep, jnp.real(wis_f), jnp.real(wis_r), jnp.imag(wis_f), jnp.imag(wis_r)], axis=-1)
    m_f = jnp.einsum('gcp,tgp->gptc', c_mat[0], pw[1 + jnp.arange(n), 0]).reshape(groups, -1, S5_COLS)
    m_r = jnp.einsum('gcp,tgp->gptc', c_mat[1], pw[n - jnp.arange(n), 1]).reshape(groups, -1, S5_COLS)
    wso = jnp.concatenate([jnp.real(m_f), jnp.real(m_r), -jnp.imag(m_f), -jnp.imag(m_r)], axis=1)
    lam_n = jnp.concatenate([pw[n, 0], pw[n, 1]], axis=-1)
    lam_tab = jnp.stack([jnp.real(lam_n), jnp.imag(lam_n)], axis=1)
    return w1.astype(BF16), wso.astype(BF16), lam_tab


def _s5_kernel(u_ref, w1_ref, wso_ref, lam_ref, y_ref, a_ref, xf_re, xf_im, xr_re, xr_im, *, nb, nc_ctx, nc):
    cols = S5_COLS
    half = lam_ref.shape[1] // 2
    a_ref[...] = _dot(u_ref[...], w1_ref[...])
    l_re = lam_ref[0:1, :]
    l_im = lam_ref[1:2, :]
    lo = lax.broadcasted_iota(jnp.int32, (nb, 2 * half), 1) < half

    def step(i, carry):
        x_re, x_im = carry
        i_rev = jnp.where(i < nc_ctx, nc_ctx - 1 - i, nc - 1 + nc_ctx - i)
        rf = pl.ds(pl.multiple_of(i * nb, nb), nb)
        rr = pl.ds(pl.multiple_of(i_rev * nb, nb), nb)
        xf_re[rf, :] = x_re
        xf_im[rf, :] = x_im
        xr_re[rr, :] = x_re
        xr_im[rr, :] = x_im
        s_re = jnp.where(lo, a_ref[rf, cols:cols + 2 * half], a_ref[rr, cols:cols + 2 * half])
        s_im = jnp.where(lo, a_ref[rf, cols + 2 * half:cols + 4 * half], a_ref[rr, cols + 2 * half:cols + 4 * half])
        return l_re * x_re - l_im * x_im + s_re, l_re * x_im + l_im * x_re + s_im

    zero = jnp.zeros((nb, 2 * half), F32)
    lax.fori_loop(0, nc, step, (zero, zero))
    r0 = nc_ctx * nb
    rows = (nc - nc_ctx) * nb
    lo_all = lax.broadcasted_iota(jnp.int32, (rows, 2 * half), 1) < half
    y = a_ref[r0:, 0:cols]
    for k, (f_ref, r_ref) in enumerate(((xf_re, xr_re), (xf_im, xr_im))):
        st = jnp.where(lo_all, f_ref[r0:, :], r_ref[r0:, :])
        hi = st.astype(BF16)
        lo_part = (st - hi.astype(F32)).astype(BF16)
        w = wso_ref[k * 2 * half:(k + 1) * 2 * half, :]
        y = y + _dot(hi, w) + _dot(lo_part, w)
    y_ref[...] = y


def _s5_call(u, w1, wso, lam_tab, *, nb, nc_ctx):
    groups, rows, cols = u.shape
    nc = rows // nb
    out_rows = (nc - nc_ctx) * nb
    st = lam_tab.shape[2]
    return pl.pallas_call(
        functools.partial(_s5_kernel, nb=nb, nc_ctx=nc_ctx, nc=nc),
        out_shape=jax.ShapeDtypeStruct((groups, out_rows, cols), F32),
        grid=(groups,),
        in_specs=[pl.BlockSpec((None, rows, cols), lambda g: (g, 0, 0)),
                  pl.BlockSpec((None,) + w1.shape[1:], lambda g: (g, 0, 0)),
                  pl.BlockSpec((None,) + wso.shape[1:], lambda g: (g, 0, 0)),
                  pl.BlockSpec((None, 2, st), lambda g: (g, 0, 0))],
        out_specs=pl.BlockSpec((None, out_rows, cols), lambda g: (g, 0, 0)),
        scratch_shapes=[pltpu.VMEM((rows, w1.shape[2]), F32)] + [pltpu.VMEM((rows, st), F32)] * 4,
        compiler_params=_params(("parallel",)),
        name="s5",
    )(u, w1, wso, lam_tab)


def _to_group_major(s_ctx, s_lat):
    s_all = jnp.concatenate([s_ctx, s_lat], axis=1)
    bsz, length, width = s_all.shape
    groups = width // S5_GROUP_CH
    nc = length // S5_CHUNK
    t = s_all.reshape(bsz, nc, S5_CHUNK, groups, S5_GROUP_CH).transpose(3, 1, 0, 2, 4)
    return t.reshape(groups, nc * bsz, S5_COLS)


def _to_token_major(y, bsz):
    groups, rows, _ = y.shape
    nc = rows // bsz
    t = y.reshape(groups, nc, bsz, S5_CHUNK, S5_GROUP_CH).transpose(2, 1, 3, 0, 4)
    return t.reshape(bsz, nc * S5_CHUNK, groups * S5_GROUP_CH)


def _mixout_kernel(y_ref, s_ref, at_ref, gs_ref, ga_ref, x_ref, g1_ref, sh2_ref, sc2_ref, dsk_ref,
                   wglu_ref, wbs_ref, wba_ref, wout_ref, lng_ref, lnb_ref, wrt_ref, wss_ref, wsd_ref,
                   x1_ref, h_ref, shared_ref, scores_ref, *, alpha):
    y = y_ref[...] + s_ref[...].astype(F32) * dsk_ref[...]
    z = _gelu_tanh(y)
    ssm = z * jax.nn.sigmoid(_dot(z.astype(BF16), wglu_ref[...]))
    merged = (gs_ref[...].astype(F32) * _dot(ssm.astype(BF16), wbs_ref[...])
              + ga_ref[...].astype(F32) * _dot(at_ref[...], wba_ref[...]))
    y_mix = _dot(merged.astype(BF16), wout_ref[...])
    x1 = _norm_rows(alpha * x_ref[...] + g1_ref[...] * y_mix, LN_EPS) * lng_ref[...] + lnb_ref[...]
    x1_ref[...] = x1
    h = _norm_rows(x1, MOD_EPS) * (1.0 + sc2_ref[...]) + sh2_ref[...]
    h_ref[...] = _pack_bf16_pairs(h)
    hb = h.astype(BF16)
    scores_ref[...] = jax.nn.sigmoid(
        lax.dot_general(wrt_ref[...], hb, (((1,), (1,)), ((), ())), preferred_element_type=F32))
    ss = _dot(hb, wss_ref[...])
    sh_hidden = wsd_ref.shape[0]
    hid = _silu(ss[:, :sh_hidden]) * ss[:, sh_hidden:]
    shared_ref[...] = _dot(hid.astype(BF16), wsd_ref[...])


def _mixout_call(y, s, attn, gs, ga, x, g1, sh2, sc2, dsk, wglu, wbs, wba, wout, lng, lnb, wrt, wss, wsd,
                 *, tm, alpha):
    bsz, seq, d = x.shape
    n_exp = wrt.shape[0]
    row = lambda width: pl.BlockSpec((None, tm, width), lambda b, i: (b, i, 0))
    mod = pl.BlockSpec((None, 1, d), lambda b, i: (b, 0, 0))
    consts = [dsk, wglu, wbs, wba, wout, lng, lnb, wrt, wss, wsd]
    return pl.pallas_call(
        functools.partial(_mixout_kernel, alpha=alpha),
        out_shape=[jax.ShapeDtypeStruct((bsz, seq, d), F32), jax.ShapeDtypeStruct((bsz, seq, d // 2), jnp.uint32),
                   jax.ShapeDtypeStruct((bsz, seq, d), F32), jax.ShapeDtypeStruct((bsz, n_exp, seq), F32)],
        grid=(bsz, seq // tm),
        in_specs=[row(d)] * 6 + [mod] * 3 + [_const_spec(a.shape) for a in consts],
        out_specs=[row(d), row(d // 2), row(d), pl.BlockSpec((None, n_exp, tm), lambda b, i: (b, 0, i))],
        compiler_params=_params(("parallel", "parallel")),
        name="mixout",
    )(y, s, attn, gs, ga, x, g1, sh2, sc2, *consts)


def _stack_rows(rows):
    n = rows[0].shape[1]
    row_k = lax.broadcasted_iota(jnp.int32, (len(rows), n), 0)
    out = jnp.zeros((len(rows), n), rows[0].dtype)
    for k, r in enumerate(rows):
        out = jnp.where(row_k == k, r, out)
    return out


def _route_kernel(s_ref, bias_ref, utri_ref, idx_ref, gate_ref, rank_ref, counts_ref, base_ref):
    @pl.when((pl.program_id(0) == 0) & (pl.program_id(1) == 0))
    def _():
        base_ref[...] = jnp.zeros_like(base_ref)

    s = s_ref[...]
    n_exp, tm = s.shape
    per_group = n_exp // N_EXPERT_GROUPS
    neg = -jnp.inf
    b = s + bias_ref[...]
    bg = b.reshape(N_EXPERT_GROUPS, per_group, tm)
    m1 = jnp.max(bg, axis=1, keepdims=True)
    is1 = bg == m1
    n1 = jnp.sum(jnp.where(is1, 1.0, 0.0), axis=1, keepdims=True)
    m2 = jnp.max(jnp.where(is1, neg, bg), axis=1, keepdims=True)
    gscore = m1 + jnp.where(n1 >= 2.0, m1, m2)
    gs = [gscore[g] for g in range(N_EXPERT_GROUPS)]
    kept = []
    for g in range(N_EXPERT_GROUPS):
        beaten = jnp.zeros((1, tm), F32)
        for o in range(N_EXPERT_GROUPS):
            if o != g:
                wins = (gs[o] >= gs[g]) if o < g else (gs[o] > gs[g])
                beaten = beaten + jnp.where(wins, 1.0, 0.0)
        kept.append(jnp.where(beaten < float(TOPK_GROUPS), bg[g], neg))
    masked = jnp.concatenate(kept, axis=0)
    row_f = lax.broadcasted_iota(jnp.int32, (n_exp, tm), 0).astype(F32)
    chosen = jnp.zeros((n_exp, tm), F32)
    firsts, vals = [], []
    for _ in range(TOP_K):
        m = jnp.max(masked, axis=0, keepdims=True)
        first = jnp.min(jnp.where(masked == m, row_f, float(n_exp)), axis=0, keepdims=True)
        sel = row_f == first
        firsts.append(first)
        vals.append(jnp.sum(jnp.where(sel, s, 0.0), axis=0, keepdims=True))
        chosen = jnp.where(sel, 1.0, chosen)
        masked = jnp.where(sel, neg, masked)
    before = _dot(chosen.astype(BF16), utri_ref[...]) + base_ref[...]
    ranks = [jnp.sum(jnp.where(row_f == f, before, 0.0), axis=0, keepdims=True) for f in firsts]
    idx_ref[...] = _stack_rows(firsts).astype(jnp.int32)
    rank_ref[...] = _stack_rows(ranks).astype(jnp.int32)
    val = _stack_rows(vals)
    gate_ref[...] = val / jnp.sum(val, axis=0, keepdims=True) * ROUTED_SCALE
    base_ref[...] = base_ref[...] + jnp.sum(chosen, axis=1, keepdims=True)
    counts_ref[...] = base_ref[...]


def _route_call(scores_t, bias, *, tm):
    bsz, n_exp, seq = scores_t.shape
    tiles = seq // tm
    tokens = bsz * seq
    r_id = lax.broadcasted_iota(jnp.int32, (tm, tm), 0)
    c_id = lax.broadcasted_iota(jnp.int32, (tm, tm), 1)
    utri = (r_id < c_id).astype(BF16)
    small = pl.BlockSpec((TOP_K, tm), lambda b, i: (0, b * tiles + i))
    col = pl.BlockSpec((n_exp, 1), lambda b, i: (0, 0))
    return pl.pallas_call(
        _route_kernel,
        out_shape=[jax.ShapeDtypeStruct((TOP_K, tokens), jnp.int32), jax.ShapeDtypeStruct((TOP_K, tokens), F32),
                   jax.ShapeDtypeStruct((TOP_K, tokens), jnp.int32), jax.ShapeDtypeStruct((n_exp, 1), F32)],
        grid=(bsz, tiles),
        in_specs=[pl.BlockSpec((None, n_exp, tm), lambda b, i: (b, 0, i)), col, _const_spec((tm, tm))],
        out_specs=[small, small, small, col],
        scratch_shapes=[pltpu.VMEM((n_exp, 1), F32)],
        compiler_params=_params(("arbitrary", "arbitrary")),
        name="route",
    )(scores_t, bias.astype(F32).reshape(n_exp, 1), utri)


def _slot_kernel(idx_ref, rank_ref, start_ref, pos_ref):
    idx = idx_ref[...]
    tm = idx.shape[1]
    n_exp = start_ref.shape[0]
    row = lax.broadcasted_iota(jnp.int32, (n_exp, tm), 0)
    starts = [jnp.sum(jnp.where(row == idx[k:k + 1, :], start_ref[...], 0.0), axis=0, keepdims=True)
              for k in range(TOP_K)]
    pos_ref[...] = rank_ref[...] + _stack_rows(starts).astype(jnp.int32)


def _slot_call(idx, rank, pad_start, *, tm):
    tokens = idx.shape[1]
    n_exp = pad_start.shape[0]
    small = pl.BlockSpec((TOP_K, tm), lambda i: (0, i))
    return pl.pallas_call(
        _slot_kernel,
        out_shape=jax.ShapeDtypeStruct((TOP_K, tokens), jnp.int32),
        grid=(tokens // tm,),
        in_specs=[small, small, _const_spec((n_exp, 1))],
        out_specs=small,
        compiler_params=_params(("parallel",)),
        name="slots",
    )(idx, rank, pad_start.astype(F32).reshape(n_exp, 1))


def _dispatch_kernel(cnt_ref, start_ref, padded_ref, npad_ref, pos_ref, h_hbm, xs_hbm, hbuf, zrow, lsem, ssem, zsem):
    i = pl.program_id(0)
    n = pl.num_programs(0)
    nbuf, td = hbuf.shape[:2]
    n_exp = cnt_ref.shape[0]
    slot = i % nbuf

    def zero_copy(row):
        return pltpu.make_async_copy(zrow, xs_hbm.at[pl.ds(row, 1), :], zsem)

    def load(step, s):
        return pltpu.make_async_copy(h_hbm.at[pl.ds(step * td, td), :], hbuf.at[s], lsem.at[s])

    def wait_scatters(s):
        rows = td * TOP_K
        pltpu.make_async_copy(h_hbm.at[pl.ds(0, rows), :], xs_hbm.at[pl.ds(0, rows), :], ssem.at[s]).wait()

    @pl.when(i == 0)
    def _():
        zrow[...] = jnp.zeros_like(zrow)

        def per_expert(e, carry):
            def one(r, c2):
                zero_copy(start_ref[e] + r).start()
                return c2
            return lax.fori_loop(cnt_ref[e], padded_ref[e], one, carry)
        lax.fori_loop(0, n_exp, per_expert, 0)
        load(0, 0).start()

    @pl.when(i >= nbuf - 1)
    def _():
        wait_scatters((i + 1) % nbuf)

    @pl.when(i + 1 < n)
    def _():
        load(i + 1, (i + 1) % nbuf).start()

    load(i, slot).wait()

    for t in range(td):
        for k in range(TOP_K):
            pltpu.make_async_copy(hbuf.at[slot, pl.ds(t, 1), :], xs_hbm.at[pl.ds(pos_ref[k, t], 1), :],
                                  ssem.at[slot]).start()

    @pl.when(i == n - 1)
    def _():
        for back in range(nbuf - 1):
            @pl.when(i >= back)
            def _():
                wait_scatters((i - back) % nbuf)

        def one(r, c2):
            zero_copy(0).wait()
            return c2
        lax.fori_loop(0, npad_ref[0], one, 0)


def _dispatch_call(counts, pad_start, padded, n_pad, pos, hp, n_slots, *, td):
    tokens, width = hp.shape
    grid_spec = pltpu.PrefetchScalarGridSpec(
        num_scalar_prefetch=4,
        grid=(tokens // td,),
        in_specs=[pl.BlockSpec((TOP_K, td), lambda i, *_: (0, i), memory_space=pltpu.SMEM),
                  pl.BlockSpec(memory_space=pl.ANY)],
        out_specs=pl.BlockSpec(memory_space=pl.ANY),
        scratch_shapes=[pltpu.VMEM((DISPATCH_BUFFERS, td, width), hp.dtype), pltpu.VMEM((1, width), hp.dtype),
                        pltpu.SemaphoreType.DMA((DISPATCH_BUFFERS,)), pltpu.SemaphoreType.DMA((DISPATCH_BUFFERS,)),
                        pltpu.SemaphoreType.DMA],
    )
    return pl.pallas_call(
        _dispatch_kernel,
        out_shape=jax.ShapeDtypeStruct((n_slots, width), hp.dtype),
        grid_spec=grid_spec,
        compiler_params=_params(("arbitrary",)),
        name="dispatch",
    )(counts, pad_start, padded, n_pad, pos, hp)


def _experts_kernel(be_ref, nact_ref, x_ref, wg_ref, wu_ref, wd_ref, y_ref, wg_s, wu_s, wd_s):
    i = pl.program_id(0)
    last = nact_ref[0] - 1
    expert = be_ref[jnp.minimum(i, last)]
    previous = be_ref[jnp.minimum(jnp.maximum(i, 1) - 1, last)]

    @pl.when((i == 0) | (expert != previous))
    def _():
        wg_s[...] = wg_ref[...].astype(BF16)
        wu_s[...] = wu_ref[...].astype(BF16)
        wd_s[...] = wd_ref[...].astype(BF16)

    @pl.when(i <= last)
    def _():
        half = x_ref.shape[1]
        x_hi, x_lo = (v.astype(BF16) for v in _unpack_bf16_pairs(x_ref[...]))
        hg = _dot(x_hi, wg_s[:half, :]) + _dot(x_lo, wg_s[half:, :])
        hu = _dot(x_hi, wu_s[:half, :]) + _dot(x_lo, wu_s[half:, :])
        y_ref[...] = _pack_bf16_pairs(_dot((_silu(hg) * hu).astype(BF16), wd_s[...]))


def _experts_call(block_expert, n_active, xs, wg, wu, wd):
    n_blocks = block_expert.shape[0]
    n_slots, half = xs.shape
    bm = n_slots // n_blocks
    d, hidden = wg.shape[1:]
    blk = lambda i, be, na: (jnp.minimum(i, na[0] - 1), 0)
    wsel = lambda i, be, na: (be[jnp.minimum(i, na[0] - 1)], 0, 0)
    grid_spec = pltpu.PrefetchScalarGridSpec(
        num_scalar_prefetch=2,
        grid=(n_blocks,),
        in_specs=[pl.BlockSpec((bm, half), blk),
                  pl.BlockSpec((None, d, hidden), wsel),
                  pl.BlockSpec((None, d, hidden), wsel),
                  pl.BlockSpec((None, hidden, d), wsel)],
        out_specs=pl.BlockSpec((bm, half), blk),
        scratch_shapes=[pltpu.VMEM((d, hidden), BF16), pltpu.VMEM((d, hidden), BF16), pltpu.VMEM((hidden, d), BF16)],
    )
    return pl.pallas_call(
        _experts_kernel,
        out_shape=jax.ShapeDtypeStruct((n_slots, half), xs.dtype),
        grid_spec=grid_spec,
        compiler_params=_params(("arbitrary",)),
        name="experts",
    )(block_expert, n_active, xs, wg, wu, wd)


def _row_gather_start(idx_ref, n_rows, src_hbm, dst_ref, sem):
    tc = idx_ref.shape[1]
    for r in range(n_rows):
        pltpu.make_async_copy(src_hbm.at[pl.ds(idx_ref[r // tc, r % tc], 1), :], dst_ref.at[pl.ds(r, 1), :],
                              sem).start()


def _row_gather_wait(n_rows, src_hbm, dst_ref, sem):
    pltpu.make_async_copy(src_hbm.at[pl.ds(0, n_rows), :], dst_ref, sem).wait()


def _combine_kernel(pos_ref, pos_next_ref, y_hbm, gate_ref, shared_ref, x1_ref, g2_ref, lng_ref, lnb_ref, o_ref,
                    buf_a, buf_b, sems, *, alpha):
    i = pl.program_id(0)
    n = pl.num_programs(0)
    rows, half = buf_a.shape
    tc = o_ref.shape[0]

    @pl.when(i == 0)
    def _():
        _row_gather_start(pos_ref, rows, y_hbm, buf_a, sems.at[0])

    def step(cur, cur_sem, nxt, nxt_sem):
        _row_gather_wait(rows, y_hbm, cur, cur_sem)
        _row_gather_start(pos_next_ref, rows, y_hbm, nxt, nxt_sem)
        r_id = lax.broadcasted_iota(jnp.int32, (tc, tc), 0)
        c_id = lax.broadcasted_iota(jnp.int32, (tc, tc), 1)
        f_hi = shared_ref[:, :half]
        f_lo = shared_ref[:, half:]
        for k in range(rows // tc):
            y_hi, y_lo = _unpack_bf16_pairs(cur[k * tc:(k + 1) * tc, :])
            g = jnp.sum(jnp.where(r_id == c_id, gate_ref[k:k + 1, :], 0.0), axis=1, keepdims=True)
            f_hi = f_hi + g * y_hi
            f_lo = f_lo + g * y_lo
        f = jnp.concatenate([f_hi, f_lo], axis=1)
        o_ref[...] = _norm_rows(alpha * x1_ref[...] + g2_ref[...] * f, LN_EPS) * lng_ref[...] + lnb_ref[...]

        @pl.when(i == n - 1)
        def _():
            _row_gather_wait(rows, y_hbm, nxt, nxt_sem)

    @pl.when(i % 2 == 0)
    def _():
        step(buf_a, sems.at[0], buf_b, sems.at[1])

    @pl.when(i % 2 == 1)
    def _():
        step(buf_b, sems.at[1], buf_a, sems.at[0])


def _combine_call(pos, y_slots, gates, shared, x1, g2, lng, lnb, *, alpha, tc, tiles_per_batch):
    tokens, d = x1.shape
    n_tiles = tokens // tc
    last = n_tiles - 1
    rows = TOP_K * tc
    row = pl.BlockSpec((tc, d), lambda i: (i, 0))
    buf = pltpu.VMEM((rows, y_slots.shape[1]), y_slots.dtype)
    return pl.pallas_call(
        functools.partial(_combine_kernel, alpha=alpha),
        out_shape=jax.ShapeDtypeStruct((tokens, d), F32),
        grid=(n_tiles,),
        in_specs=[
            pl.BlockSpec((TOP_K, tc), lambda i: (0, i), memory_space=pltpu.SMEM),
            pl.BlockSpec((TOP_K, tc), lambda i: (0, jnp.minimum(i + 1, last)), memory_space=pltpu.SMEM),
            pl.BlockSpec(memory_space=pl.ANY),
            pl.BlockSpec((TOP_K, tc), lambda i: (0, i)),
            row, row,
            pl.BlockSpec((None, 1, d), lambda i: (i // tiles_per_batch, 0, 0)),
            _const_spec(lng.shape), _const_spec(lnb.shape),
        ],
        out_specs=row,
        scratch_shapes=[buf, buf, pltpu.SemaphoreType.DMA((2,))],
        compiler_params=_params(("arbitrary",)),
        name="combine",
    )(pos, pos, y_slots, gates, shared, x1, g2, lng, lnb)


def kernel(x, c, ctx, c_ctx, w_mod, b_mod, w_in, s5_lam_re, s5_lam_im, s5_log_dt, s5_b_re, s5_b_im, s5_c_re, s5_c_im, s5_d, w_glu, q_norm_g, k_norm_g, w_branch_ssm, w_branch_attn, w_out, ln1_g, ln1_b, w_router, router_bias, w_exp_gate, w_exp_up, w_exp_down, w_sh_gate, w_sh_up, w_sh_down, ln2_g, ln2_b):
    depth = w_mod.shape[0]
    assert depth == 1, "single-layer block: context outputs are never needed"
    bsz, seq, d = x.shape
    ctx_len = ctx.shape[1]
    kvw = N_KV_HEADS * HEAD_DIM
    n_exp = w_router.shape[2]
    alpha = (2.0 * depth) ** 0.25
    assert seq % GRID_W == 0 and seq % S5_CHUNK == 0 and ctx_len % S5_CHUNK == 0 and bsz % SUBLANES == 0
    lay = 0

    pad = (-(bsz + 1)) % SUBLANES
    c_all = jnp.concatenate([c, c_ctx[None, :], jnp.zeros((pad, d), F32)], axis=0)
    mod = _mod_call(c_all, w_mod[lay], b_mod[lay])
    mod_lat = mod[:bsz].reshape(bsz, N_MOD, 1, d)
    sh1, sc1, g1, sh2, sc2, g2 = (mod_lat[:, k] for k in range(N_MOD))
    mod_ctx = mod[bsz].reshape(N_MOD, 1, 1, d)

    w_in_l = w_in[lay].astype(BF16)
    w_ctx = jnp.concatenate([w_in_l[:, :d], w_in_l[:, 2 * d:2 * d + 2 * kvw]], axis=1)
    head_id = jnp.arange(kvw) // HEAD_DIM
    bd = jnp.where(head_id[:, None] == head_id[None, :], 1.0 / HEAD_DIM, 0.0).astype(BF16)
    tqa, tqb = _rope_tables(seq, q_norm_g[lay], HEAD_DIM ** -0.5)
    tka, tkb = _rope_tables(seq, k_norm_g[lay], 1.0)
    tca = jnp.tile(k_norm_g[lay].astype(F32), N_KV_HEADS)[None, :]
    tm_lat = min(512, seq)
    s_lat, k_lat, v_lat, q_lat, gs_lat, ga_lat = _inproj_call(x, sh1, sc1, w_in_l, bd, tka, tkb, tqa, tqb, tm=tm_lat)
    s_ctx, k_ctx, v_ctx = _inproj_call(ctx, mod_ctx[0], mod_ctx[1], w_ctx, bd, tca, jnp.zeros_like(tca),
                                       tm=min(256, ctx_len))

    w1, wso, lam_tab = _s5_tables(s5_lam_re[lay], s5_lam_im[lay], s5_log_dt[lay], s5_b_re[lay], s5_b_im[lay],
                                  s5_c_re[lay], s5_c_im[lay])
    y_groups = _s5_call(_to_group_major(s_ctx, s_lat), w1, wso, lam_tab, nb=bsz, nc_ctx=ctx_len // S5_CHUNK)
    y_s5 = _to_token_major(y_groups, bsz)

    attn = _attn_call(q_lat, k_ctx, k_lat, v_ctx, v_lat, tq=min(512, seq))

    row = lambda v: v.astype(F32).reshape(1, -1)
    wss = jnp.concatenate([w_sh_gate[lay], w_sh_up[lay]], axis=1).astype(BF16)
    x1, hp, shared, scores_t = _mixout_call(
        y_s5, s_lat, attn, gs_lat, ga_lat, x, g1, sh2, sc2, row(s5_d[lay]),
        w_glu[lay].astype(BF16), w_branch_ssm[lay].astype(BF16), w_branch_attn[lay].astype(BF16),
        w_out[lay].astype(BF16), row(ln1_g[lay]), row(ln1_b[lay]), w_router[lay].T.astype(BF16), wss,
        w_sh_down[lay].astype(BF16), tm=min(256, seq), alpha=alpha)

    tokens = bsz * seq
    bm = MOE_BLOCK
    idx, gates, rank, counts_f = _route_call(scores_t, router_bias[lay], tm=min(ROUTE_TOKENS, seq))
    counts = counts_f.reshape(n_exp).astype(jnp.int32)
    padded = (counts + bm - 1) // bm * bm
    pad_end = jnp.cumsum(padded)
    pad_start = pad_end - padded
    n_blocks = (tokens * TOP_K + n_exp * (bm - 1) + bm - 1) // bm
    block_start = jnp.arange(n_blocks, dtype=jnp.int32) * bm
    block_expert = jnp.minimum(jnp.sum((pad_end[None, :] <= block_start[:, None]).astype(jnp.int32), axis=1), n_exp - 1)
    n_active = (pad_end[-1] // bm).reshape(1)
    n_pad = jnp.sum(padded - counts).reshape(1)
    pos = _slot_call(idx, rank, pad_start, tm=min(512, tokens))
    xs = _dispatch_call(counts, pad_start, padded, n_pad, pos, hp.reshape(tokens, d // 2), n_blocks * bm,
                        td=min(DISPATCH_TOKENS, tokens))
    y_slots = _experts_call(block_expert, n_active, xs, w_exp_gate[lay], w_exp_up[lay], w_exp_down[lay])

    tc = min(COMBINE_TOKENS, seq)
    out = _combine_call(pos, y_slots, gates, shared.reshape(tokens, d), x1.reshape(tokens, d), g2,
                        row(ln2_g[lay]), row(ln2_b[lay]), alpha=alpha, tc=tc, tiles_per_batch=seq // tc)
    return out.reshape(bsz, seq, d)
```

```python
import functools
import math

import jax
import jax.numpy as jnp
from jax import lax
from jax.experimental import pallas as pl
from jax.experimental.pallas import tpu as pltpu

F32 = jnp.float32
BF16 = jnp.bfloat16

GRID_W = 64
HEAD_DIM = 64
N_KV_HEADS = 4
S5_GROUP_CH = 16
S5_MAX_RE = -1e-4
ROPE_THETA = 10000.0
TOP_K = 8
N_EXPERT_GROUPS = 8
TOPK_GROUPS = 4
ROUTED_SCALE = 2.5
LN_EPS = 1e-5
MOD_EPS = 1e-6
RMS_EPS = 1e-6
N_MOD = 6

LANES = 128
SUBLANES = 8
VMEM_LIMIT_BYTES = 56 * 1024 * 1024

S5_CHUNK = 16
S5_COLS = S5_CHUNK * S5_GROUP_CH
MOE_BLOCK = 512
ROUTE_TOKENS = 512
DISPATCH_TOKENS = 128
DISPATCH_BUFFERS = 3
COMBINE_TOKENS = 128


def _params(sem):
    return pltpu.CompilerParams(dimension_semantics=sem, vmem_limit_bytes=VMEM_LIMIT_BYTES)


def _const_spec(shape):
    nd = len(shape)
    return pl.BlockSpec(shape, lambda *_: (0,) * nd, pipeline_mode=pl.Buffered(1))


def _dot(a, b):
    return jnp.dot(a, b, preferred_element_type=F32)


def _norm_rows(x, eps):
    mu = jnp.mean(x, axis=-1, keepdims=True)
    xc = x - mu
    var = jnp.mean(xc * xc, axis=-1, keepdims=True)
    return xc * lax.rsqrt(var + eps)


def _silu(x):
    return x * jax.nn.sigmoid(x)


def _gelu_tanh(x):
    return 0.5 * x * (1.0 + jnp.tanh(math.sqrt(2.0 / math.pi) * (x + 0.044715 * (x * x * x))))


def _pack_bf16_pairs(v):
    n = v.shape[1] // 2
    bits = lax.bitcast_convert_type(v.astype(BF16).astype(F32), jnp.uint32)
    return bits[:, :n] | (bits[:, n:] >> 16)


def _unpack_bf16_pairs(w):
    hi = lax.bitcast_convert_type(w & jnp.uint32(0xFFFF0000), F32)
    lo = lax.bitcast_convert_type(w << 16, F32)
    return hi, lo


def _mod_kernel(c_ref, w_ref, b_ref, o_ref):
    o_ref[...] = _dot(_silu(c_ref[...]), w_ref[...]) + b_ref[...]


def _mod_call(c_all, w_mod, b_mod):
    rows, d = c_all.shape
    n = w_mod.shape[1]
    tn = d
    return pl.pallas_call(
        _mod_kernel,
        out_shape=jax.ShapeDtypeStruct((rows, n), F32),
        grid=(n // tn,),
        in_specs=[pl.BlockSpec((rows, d), lambda j: (0, 0)),
                  pl.BlockSpec((d, tn), lambda j: (0, j)),
                  pl.BlockSpec((1, tn), lambda j: (0, j))],
        out_specs=pl.BlockSpec((rows, tn), lambda j: (0, j)),
        compiler_params=_params(("arbitrary",)),
        name="mod",
    )(c_all, w_mod, b_mod.reshape(1, n))


def _swap16(t):
    width = t.shape[1]
    lane = lax.broadcasted_iota(jnp.int32, t.shape, 1)
    first = (lane & 16) == 0
    return jnp.where(first, pltpu.roll(t, width - 16, 1), pltpu.roll(t, 16, 1))


def _rms_rope(t, bd_ref, ta, tb):
    msq = _dot((t * t).astype(BF16), bd_ref[...])
    return lax.rsqrt(msq + RMS_EPS) * (t * ta + _swap16(t) * tb)


def _store_padded_heads(t, o_ref, ones_lane=False):
    rows = t.shape[0]
    lane = lax.broadcasted_iota(jnp.int32, (rows, LANES), 1)
    lo = lane < HEAD_DIM
    fill_hi = jnp.where(lane == HEAD_DIM, 1.0, 0.0) if ones_lane else 0.0
    fill_lo = jnp.where(lane == 0, 1.0, 0.0) if ones_lane else 0.0
    for j in range(t.shape[1] // LANES):
        slab = t[:, j * LANES:(j + 1) * LANES]
        swapped = pltpu.roll(slab, HEAD_DIM, 1)
        pieces = (jnp.where(lo, slab, fill_hi), jnp.where(lo, fill_lo, swapped),
                  jnp.where(lo, swapped, fill_hi), jnp.where(lo, fill_lo, slab))
        for p, piece in enumerate(pieces):
            c0 = (4 * j + p) * LANES
            o_ref[:, c0:c0 + LANES] = piece.astype(o_ref.dtype)


def _inproj_kernel(x_ref, sh_ref, sc_ref, w_ref, bd_ref, tka_ref, tkb_ref, *rest, d, kvw, has_q):
    if has_q:
        tqa_ref, tqb_ref, s_ref, k_ref, v_ref, q_ref, gs_ref, ga_ref = rest
    else:
        s_ref, k_ref, v_ref = rest
    u = (_norm_rows(x_ref[...], MOD_EPS) * (1.0 + sc_ref[...]) + sh_ref[...]).astype(BF16)
    col = 0
    s_ref[...] = _dot(u, w_ref[:, col:col + d]).astype(s_ref.dtype)
    col += d
    if has_q:
        for c in range(d // kvw):
            q = _dot(u, w_ref[:, col + c * kvw:col + (c + 1) * kvw])
            q_ref[:, c * kvw:(c + 1) * kvw] = _rms_rope(q, bd_ref, tqa_ref[...], tqb_ref[...]).astype(q_ref.dtype)
        col += d
    k = _dot(u, w_ref[:, col:col + kvw])
    _store_padded_heads(_rms_rope(k, bd_ref, tka_ref[...], tkb_ref[...]), k_ref)
    col += kvw
    _store_padded_heads(_dot(u, w_ref[:, col:col + kvw]), v_ref, ones_lane=True)
    col += kvw
    if has_q:
        gs_ref[...] = jax.nn.sigmoid(_dot(u, w_ref[:, col:col + d])).astype(gs_ref.dtype)
        col += d
        ga_ref[...] = jax.nn.sigmoid(_dot(u, w_ref[:, col:col + d])).astype(ga_ref.dtype)


def _inproj_call(x, shift, scale, w, bd, tka, tkb, tqa=None, tqb=None, *, tm):
    bsz, length, d = x.shape
    has_q = tqa is not None
    kvw = N_KV_HEADS * HEAD_DIM
    per_batch = shift.shape[0] > 1
    tab_rows = tka.shape[0]
    tab_blk = tm if tab_rows > 1 else 1
    mod_spec = pl.BlockSpec((None, 1, d), (lambda b, i: (b, 0, 0)) if per_batch else (lambda b, i: (0, 0, 0)))
    tab_spec = pl.BlockSpec((tab_blk, kvw), (lambda b, i: (i, 0)) if tab_rows > 1 else (lambda b, i: (0, 0)))
    row_spec = lambda width: pl.BlockSpec((None, tm, width), lambda b, i: (b, i, 0))
    in_specs = [row_spec(d), mod_spec, mod_spec, _const_spec(w.shape), _const_spec(bd.shape), tab_spec, tab_spec]
    args = [x, shift, scale, w, bd, tka, tkb]
    widths = [d, 4 * kvw, 4 * kvw]
    if has_q:
        in_specs += [tab_spec, tab_spec]
        args += [tqa, tqb]
        widths += [d, d, d]
    return pl.pallas_call(
        functools.partial(_inproj_kernel, d=d, kvw=kvw, has_q=has_q),
        out_shape=[jax.ShapeDtypeStruct((bsz, length, wd), BF16) for wd in widths],
        grid=(bsz, length // tm),
        in_specs=in_specs,
        out_specs=[row_spec(wd) for wd in widths],
        compiler_params=_params(("parallel", "parallel")),
        name="inproj_lat" if has_q else "inproj_ctx",
    )(*args)


def _rope_tables(seq, gain, scale):
    half = HEAD_DIM // 2
    inv_freq = ROPE_THETA ** (-jnp.arange(0, half, 2, dtype=F32) / half)
    t = jnp.arange(seq, dtype=jnp.int32)
    pos = jnp.stack([(t // GRID_W).astype(F32), (t % GRID_W).astype(F32)], axis=1)
    dim = jnp.arange(HEAD_DIM)
    axis = dim // half
    second = ((dim % half) // (half // 2)) == 1
    freq = inv_freq[dim % (half // 2)]
    ang = pos[:, axis] * freq[None, :]
    partner = jnp.where(second, dim - half // 2, dim + half // 2)
    g = gain.astype(F32)
    ta = jnp.cos(ang) * g[None, :] * scale
    tb = jnp.sin(ang) * jnp.where(second, 1.0, -1.0)[None, :] * g[partner][None, :] * scale
    return jnp.tile(ta, (1, N_KV_HEADS)), jnp.tile(tb, (1, N_KV_HEADS))


def _attn_kernel(q_ref, kc_ref, kl_ref, vc_ref, vl_ref, o_ref):
    nt = (((1,), (1,)), ((), ()))
    rows = q_ref.shape[0]
    lane = lax.broadcasted_iota(jnp.int32, (rows, LANES), 1)
    for j in range(q_ref.shape[1] // LANES):
        qs = q_ref[:, j * LANES:(j + 1) * LANES]
        acc = []
        for half in range(2):
            cols = slice(half * LANES, (half + 1) * LANES)
            s_c = lax.dot_general(qs, kc_ref[:, cols], nt, preferred_element_type=F32)
            s_l = lax.dot_general(qs, kl_ref[:, cols], nt, preferred_element_type=F32)
            m = jnp.maximum(jnp.max(s_c, axis=-1, keepdims=True), jnp.max(s_l, axis=-1, keepdims=True))
            e_c = jnp.exp((s_c - m).astype(BF16))
            e_l = jnp.exp((s_l - m).astype(BF16))
            acc.append(_dot(e_c, vc_ref[:, cols]) + _dot(e_l, vl_ref[:, cols]))
        out = jnp.where(lane < HEAD_DIM, acc[0] / acc[0][:, HEAD_DIM:HEAD_DIM + 1], acc[1] / acc[1][:, 0:1])
        o_ref[:, j * LANES:(j + 1) * LANES] = out.astype(o_ref.dtype)


def _attn_call(q, kc, kl, vc, vl, *, tq):
    bsz, seq, d = q.shape
    ctx = kc.shape[1]
    gw = d // N_KV_HEADS
    q_spec = pl.BlockSpec((None, tq, gw), lambda b, h, i: (b, i, h))
    kv_spec = lambda length: pl.BlockSpec((None, length, 2 * LANES), lambda b, h, i: (b, 0, h))
    return pl.pallas_call(
        _attn_kernel,
        out_shape=jax.ShapeDtypeStruct((bsz, seq, d), BF16),
        grid=(bsz, N_KV_HEADS, seq // tq),
        in_specs=[q_spec, kv_spec(ctx), kv_spec(seq), kv_spec(ctx), kv_spec(seq)],
        out_specs=q_spec,
        compiler_params=_params(("parallel", "parallel", "arbitrary")),
        name="attention",
    )(q, kc, kl, vc, vl)


def _s5_tables(lam_re, lam_im, log_dt, b_re, b_im, c_re, c_im):
    n = S5_CHUNK
    lam = lax.complex(jnp.minimum(lam_re.astype(F32), S5_MAX_RE), lam_im.astype(F32))
    lam_dt = lam * jnp.exp(log_dt.astype(F32))[..., None]
    b_bar = ((jnp.exp(lam_dt) - 1.0) / lam)[..., None] * lax.complex(b_re.astype(F32), b_im.astype(F32))
    c_mat = lax.complex(c_re.astype(F32), c_im.astype(F32))
    pw = jnp.exp(lam_dt[None] * jnp.arange(n + 1, dtype=F32)[:, None, None, None])
    kern = jnp.real(jnp.einsum('dgcp,jdgp,dgpe->djgce', c_mat, pw[:n], b_bar))
    s_idx = jnp.arange(n)[:, None]
    t_idx = jnp.arange(n)[None, :]

    def toeplitz(k, lag):
        g = k[jnp.clip(lag, 0, n - 1)]
        g = jnp.where((lag >= 0)[:, :, None, None, None], g, 0.0)
        return g.transpose(2, 0, 4, 1, 3)

    toep = toeplitz(kern[0], t_idx - s_idx) + toeplitz(kern[1], s_idx - t_idx)
    groups = toep.shape[0]
    toep = toep.reshape(groups, S5_COLS, S5_COLS)
    wis_f = jnp.einsum('sgp,gpe->gsep', pw[n - 1 - jnp.arange(n), 0], b_bar[0]).reshape(groups, S5_COLS, -1)
    wis_r = jnp.einsum('sgp,gpe->gsep', pw[jnp.arange(n), 1], b_bar[1]).reshape(groups, S5_COLS, -1)
    w1 = jnp.concatenate([toep, jnp.real(wis_f), jnp.real(wis_r), jnp.imag(wis_f), jnp.imag(wis_r)], axis=-1)
    m_f = jnp.einsum('gcp,tgp->gptc', c_mat[0], pw[1 + jnp.arange(n), 0]).reshape(groups, -1, S5_COLS)
    m_r = jnp.einsum('gcp,tgp->gptc', c_mat[1], pw[n - jnp.arange(n), 1]).reshape(groups, -1, S5_COLS)
    wso = jnp.concatenate([jnp.real(m_f), jnp.real(m_r), -jnp.imag(m_f), -jnp.imag(m_r)], axis=1)
    lam_n = jnp.concatenate([pw[n, 0], pw[n, 1]], axis=-1)
    lam_tab = jnp.stack([jnp.real(lam_n), jnp.imag(lam_n)], axis=1)
    return w1.astype(BF16), wso.astype(BF16), lam_tab


def _s5_kernel(uc_ref, ul_ref, w1_ref, wso_ref, lam_ref, y_ref, a_ref, xf_re, xf_im, xr_re, xr_im,
               *, nb, nc_ctx, nc):
    cols = S5_COLS
    half = lam_ref.shape[1] // 2
    a_ref[:nc_ctx * nb, :] = _dot(uc_ref[...], w1_ref[...])
    a_ref[nc_ctx * nb:, :] = _dot(ul_ref[...], w1_ref[...])
    l_re = lam_ref[0:1, :]
    l_im = lam_ref[1:2, :]
    lo = lax.broadcasted_iota(jnp.int32, (nb, 2 * half), 1) < half

    def step(i, carry):
        x_re, x_im = carry
        i_rev = jnp.where(i < nc_ctx, nc_ctx - 1 - i, nc - 1 + nc_ctx - i)
        rf = pl.ds(pl.multiple_of(i * nb, nb), nb)
        rr = pl.ds(pl.multiple_of(i_rev * nb, nb), nb)
        xf_re[rf, :] = x_re
        xf_im[rf, :] = x_im
        xr_re[rr, :] = x_re
        xr_im[rr, :] = x_im
        s_re = jnp.where(lo, a_ref[rf, cols:cols + 2 * half], a_ref[rr, cols:cols + 2 * half])
        s_im = jnp.where(lo, a_ref[rf, cols + 2 * half:cols + 4 * half], a_ref[rr, cols + 2 * half:cols + 4 * half])
        return l_re * x_re - l_im * x_im + s_re, l_re * x_im + l_im * x_re + s_im

    zero = jnp.zeros((nb, 2 * half), F32)
    lax.fori_loop(0, nc, step, (zero, zero))
    r0 = nc_ctx * nb
    rows = (nc - nc_ctx) * nb
    lo_all = lax.broadcasted_iota(jnp.int32, (rows, 2 * half), 1) < half
    y = a_ref[r0:, 0:cols]
    for k, (f_ref, r_ref) in enumerate(((xf_re, xr_re), (xf_im, xr_im))):
        st = jnp.where(lo_all, f_ref[r0:, :], r_ref[r0:, :])
        hi = st.astype(BF16)
        lo_part = (st - hi.astype(F32)).astype(BF16)
        w = wso_ref[k * 2 * half:(k + 1) * 2 * half, :]
        y = y + _dot(hi, w) + _dot(lo_part, w)
    y_ref[...] = y.astype(y_ref.dtype)


def _s5_call(u_ctx, u_lat, w1, wso, lam_tab, *, nb):
    groups, out_rows, cols = u_lat.shape
    ctx_rows = u_ctx.shape[1]
    rows = ctx_rows + out_rows
    nc = rows // nb
    nc_ctx = ctx_rows // nb
    st = lam_tab.shape[2]
    return pl.pallas_call(
        functools.partial(_s5_kernel, nb=nb, nc_ctx=nc_ctx, nc=nc),
        out_shape=jax.ShapeDtypeStruct((groups, out_rows, cols), BF16),
        grid=(groups,),
        in_specs=[pl.BlockSpec((None, ctx_rows, cols), lambda g: (g, 0, 0)),
                  pl.BlockSpec((None, out_rows, cols), lambda g: (g, 0, 0)),
                  pl.BlockSpec((None,) + w1.shape[1:], lambda g: (g, 0, 0)),
                  pl.BlockSpec((None,) + wso.shape[1:], lambda g: (g, 0, 0)),
                  pl.BlockSpec((None, 2, st), lambda g: (g, 0, 0))],
        out_specs=pl.BlockSpec((None, out_rows, cols), lambda g: (g, 0, 0)),
        scratch_shapes=[pltpu.VMEM((rows, w1.shape[2]), F32)] + [pltpu.VMEM((rows, st), F32)] * 4,
        compiler_params=_params(("parallel",)),
        name="s5",
    )(u_ctx, u_lat, w1, wso, lam_tab)


def _to_group_major(s):
    bsz, length, width = s.shape
    groups = width // S5_GROUP_CH
    nc = length // S5_CHUNK
    t = s.reshape(bsz, nc, S5_CHUNK, groups, S5_GROUP_CH).transpose(3, 1, 0, 2, 4)
    return t.reshape(groups, nc * bsz, S5_COLS)


def _to_token_major(y, bsz):
    groups, rows, _ = y.shape
    nc = rows // bsz
    t = y.reshape(groups, nc, bsz, S5_CHUNK, S5_GROUP_CH).transpose(2, 1, 3, 0, 4)
    return t.reshape(bsz, nc * S5_CHUNK, groups * S5_GROUP_CH)


def _mixout_kernel(y_ref, s_ref, at_ref, gs_ref, ga_ref, x_ref, g1_ref, sh2_ref, sc2_ref, dsk_ref,
                   wglu_ref, wbs_ref, wba_ref, wout_ref, lng_ref, lnb_ref, wrt_ref, wss_ref, wsd_ref,
                   x1_ref, h_ref, shared_ref, scores_ref, *, alpha):
    y = y_ref[...].astype(F32) + s_ref[...].astype(F32) * dsk_ref[...]
    z = _gelu_tanh(y)
    ssm = z * jax.nn.sigmoid(_dot(z.astype(BF16), wglu_ref[...]))
    merged = (gs_ref[...].astype(F32) * _dot(ssm.astype(BF16), wbs_ref[...])
              + ga_ref[...].astype(F32) * _dot(at_ref[...], wba_ref[...]))
    y_mix = _dot(merged.astype(BF16), wout_ref[...])
    x1 = _norm_rows(alpha * x_ref[...] + g1_ref[...] * y_mix, LN_EPS) * lng_ref[...] + lnb_ref[...]
    x1_ref[...] = x1
    h = _norm_rows(x1, MOD_EPS) * (1.0 + sc2_ref[...]) + sh2_ref[...]
    h_ref[...] = _pack_bf16_pairs(h)
    hb = h.astype(BF16)
    scores_ref[...] = jax.nn.sigmoid(
        lax.dot_general(wrt_ref[...], hb, (((1,), (1,)), ((), ())), preferred_element_type=F32))
    ss = _dot(hb, wss_ref[...])
    sh_hidden = wsd_ref.shape[0]
    hid = _silu(ss[:, :sh_hidden]) * ss[:, sh_hidden:]
    shared_ref[...] = _dot(hid.astype(BF16), wsd_ref[...])


def _mixout_call(y, s, attn, gs, ga, x, g1, sh2, sc2, dsk, wglu, wbs, wba, wout, lng, lnb, wrt, wss, wsd,
                 *, tm, alpha):
    bsz, seq, d = x.shape
    n_exp = wrt.shape[0]
    row = lambda width: pl.BlockSpec((None, tm, width), lambda b, i: (b, i, 0))
    mod = pl.BlockSpec((None, 1, d), lambda b, i: (b, 0, 0))
    consts = [dsk, wglu, wbs, wba, wout, lng, lnb, wrt, wss, wsd]
    return pl.pallas_call(
        functools.partial(_mixout_kernel, alpha=alpha),
        out_shape=[jax.ShapeDtypeStruct((bsz, seq, d), F32), jax.ShapeDtypeStruct((bsz, seq, d // 2), jnp.uint32),
                   jax.ShapeDtypeStruct((bsz, seq, d), F32), jax.ShapeDtypeStruct((bsz, n_exp, seq), F32)],
        grid=(bsz, seq // tm),
        in_specs=[row(d)] * 6 + [mod] * 3 + [_const_spec(a.shape) for a in consts],
        out_specs=[row(d), row(d // 2), row(d), pl.BlockSpec((None, n_exp, tm), lambda b, i: (b, 0, i))],
        compiler_params=_params(("parallel", "parallel")),
        name="mixout",
    )(y, s, attn, gs, ga, x, g1, sh2, sc2, *consts)


def _stack_rows(rows):
    n = rows[0].shape[1]
    row_k = lax.broadcasted_iota(jnp.int32, (len(rows), n), 0)
    out = jnp.zeros((len(rows), n), rows[0].dtype)
    for k, r in enumerate(rows):
        out = jnp.where(row_k == k, r, out)
    return out


def _route_kernel(s_ref, bias_ref, utri_ref, idx_ref, gate_ref, rank_ref, counts_ref, base_ref):
    @pl.when((pl.program_id(0) == 0) & (pl.program_id(1) == 0))
    def _():
        base_ref[...] = jnp.zeros_like(base_ref)

    s = s_ref[...]
    n_exp, tm = s.shape
    per_group = n_exp // N_EXPERT_GROUPS
    neg = -jnp.inf
    b = s + bias_ref[...]
    bg = b.reshape(N_EXPERT_GROUPS, per_group, tm)
    m1 = jnp.max(bg, axis=1, keepdims=True)
    is1 = bg == m1
    n1 = jnp.sum(jnp.where(is1, 1.0, 0.0), axis=1, keepdims=True)
    m2 = jnp.max(jnp.where(is1, neg, bg), axis=1, keepdims=True)
    gscore = m1 + jnp.where(n1 >= 2.0, m1, m2)
    gs = [gscore[g] for g in range(N_EXPERT_GROUPS)]
    kept = []
    for g in range(N_EXPERT_GROUPS):
        beaten = jnp.zeros((1, tm), F32)
        for o in range(N_EXPERT_GROUPS):
            if o != g:
                wins = (gs[o] >= gs[g]) if o < g else (gs[o] > gs[g])
                beaten = beaten + jnp.where(wins, 1.0, 0.0)
        kept.append(jnp.where(beaten < float(TOPK_GROUPS), bg[g], neg))
    masked = jnp.concatenate(kept, axis=0)
    row_f = lax.broadcasted_iota(jnp.int32, (n_exp, tm), 0).astype(F32)
    chosen = jnp.zeros((n_exp, tm), F32)
    firsts, vals = [], []
    for _ in range(TOP_K):
        m = jnp.max(masked, axis=0, keepdims=True)
        first = jnp.min(jnp.where(masked == m, row_f, float(n_exp)), axis=0, keepdims=True)
        sel = row_f == first
        firsts.append(first)
        vals.append(jnp.sum(jnp.where(sel, s, 0.0), axis=0, keepdims=True))
        chosen = jnp.where(sel, 1.0, chosen)
        masked = jnp.where(sel, neg, masked)
    before = _dot(chosen.astype(BF16), utri_ref[...]) + base_ref[...]
    ranks = [jnp.sum(jnp.where(row_f == f, before, 0.0), axis=0, keepdims=True) for f in firsts]
    idx_ref[...] = _stack_rows(firsts).astype(jnp.int32)
    rank_ref[...] = _stack_rows(ranks).astype(jnp.int32)
    val = _stack_rows(vals)
    gate_ref[...] = val / jnp.sum(val, axis=0, keepdims=True) * ROUTED_SCALE
    base_ref[...] = base_ref[...] + jnp.sum(chosen, axis=1, keepdims=True)
    counts_ref[...] = base_ref[...]


def _route_call(scores_t, bias, *, tm):
    bsz, n_exp, seq = scores_t.shape
    tiles = seq // tm
    tokens = bsz * seq
    r_id = lax.broadcasted_iota(jnp.int32, (tm, tm), 0)
    c_id = lax.broadcasted_iota(jnp.int32, (tm, tm), 1)
    utri = (r_id < c_id).astype(BF16)
    small = pl.BlockSpec((TOP_K, tm), lambda b, i: (0, b * tiles + i))
    col = pl.BlockSpec((n_exp, 1), lambda b, i: (0, 0))
    return pl.pallas_call(
        _route_kernel,
        out_shape=[jax.ShapeDtypeStruct((TOP_K, tokens), jnp.int32), jax.ShapeDtypeStruct((TOP_K, tokens), F32),
                   jax.ShapeDtypeStruct((TOP_K, tokens), jnp.int32), jax.ShapeDtypeStruct((n_exp, 1), F32)],
        grid=(bsz, tiles),
        in_specs=[pl.BlockSpec((None, n_exp, tm), lambda b, i: (b, 0, i)), col, _const_spec((tm, tm))],
        out_specs=[small, small, small, col],
        scratch_shapes=[pltpu.VMEM((n_exp, 1), F32)],
        compiler_params=_params(("arbitrary", "arbitrary")),
        name="route",
    )(scores_t, bias.astype(F32).reshape(n_exp, 1), utri)


def _slot_kernel(idx_ref, rank_ref, start_ref, pos_ref):
    idx = idx_ref[...]
    tm = idx.shape[1]
    n_exp = start_ref.shape[0]
    row = lax.broadcasted_iota(jnp.int32, (n_exp, tm), 0)
    starts = [jnp.sum(jnp.where(row == idx[k:k + 1, :], start_ref[...], 0.0), axis=0, keepdims=True)
              for k in range(TOP_K)]
    pos_ref[...] = rank_ref[...] + _stack_rows(starts).astype(jnp.int32)


def _slot_call(idx, rank, pad_start, *, tm):
    tokens = idx.shape[1]
    n_exp = pad_start.shape[0]
    small = pl.BlockSpec((TOP_K, tm), lambda i: (0, i))
    return pl.pallas_call(
        _slot_kernel,
        out_shape=jax.ShapeDtypeStruct((TOP_K, tokens), jnp.int32),
        grid=(tokens // tm,),
        in_specs=[small, small, _const_spec((n_exp, 1))],
        out_specs=small,
        compiler_params=_params(("parallel",)),
        name="slots",
    )(idx, rank, pad_start.astype(F32).reshape(n_exp, 1))


def _dispatch_kernel(cnt_ref, start_ref, padded_ref, npad_ref, pos_ref, h_hbm, xs_hbm, hbuf, zrow, lsem, ssem, zsem):
    i = pl.program_id(0)
    n = pl.num_programs(0)
    nbuf, td = hbuf.shape[:2]
    n_exp = cnt_ref.shape[0]
    slot = i % nbuf

    def zero_copy(row):
        return pltpu.make_async_copy(zrow, xs_hbm.at[pl.ds(row, 1), :], zsem)

    def load(step, s):
        return pltpu.make_async_copy(h_hbm.at[pl.ds(step * td, td), :], hbuf.at[s], lsem.at[s])

    def wait_scatters(s):
        rows = td * TOP_K
        pltpu.make_async_copy(h_hbm.at[pl.ds(0, rows), :], xs_hbm.at[pl.ds(0, rows), :], ssem.at[s]).wait()

    @pl.when(i == 0)
    def _():
        zrow[...] = jnp.zeros_like(zrow)

        def per_expert(e, carry):
            def one(r, c2):
                zero_copy(start_ref[e] + r).start()
                return c2
            return lax.fori_loop(cnt_ref[e], padded_ref[e], one, carry)
        lax.fori_loop(0, n_exp, per_expert, 0)
        load(0, 0).start()

    @pl.when(i >= nbuf - 1)
    def _():
        wait_scatters((i + 1) % nbuf)

    @pl.when(i + 1 < n)
    def _():
        load(i + 1, (i + 1) % nbuf).start()

    load(i, slot).wait()

    for t in range(td):
        for k in range(TOP_K):
            pltpu.make_async_copy(hbuf.at[slot, pl.ds(t, 1), :], xs_hbm.at[pl.ds(pos_ref[k, t], 1), :],
                                  ssem.at[slot]).start()

    @pl.when(i == n - 1)
    def _():
        for back in range(nbuf - 1):
            @pl.when(i >= back)
            def _():
                wait_scatters((i - back) % nbuf)

        def one(r, c2):
            zero_copy(0).wait()
            return c2
        lax.fori_loop(0, npad_ref[0], one, 0)


def _dispatch_call(counts, pad_start, padded, n_pad, pos, hp, n_slots, *, td):
    tokens, width = hp.shape
    grid_spec = pltpu.PrefetchScalarGridSpec(
        num_scalar_prefetch=4,
        grid=(tokens // td,),
        in_specs=[pl.BlockSpec((TOP_K, td), lambda i, *_: (0, i), memory_space=pltpu.SMEM),
                  pl.BlockSpec(memory_space=pl.ANY)],
        out_specs=pl.BlockSpec(memory_space=pl.ANY),
        scratch_shapes=[pltpu.VMEM((DISPATCH_BUFFERS, td, width), hp.dtype), pltpu.VMEM((1, width), hp.dtype),
                        pltpu.SemaphoreType.DMA((DISPATCH_BUFFERS,)), pltpu.SemaphoreType.DMA((DISPATCH_BUFFERS,)),
                        pltpu.SemaphoreType.DMA],
    )
    return pl.pallas_call(
        _dispatch_kernel,
        out_shape=jax.ShapeDtypeStruct((n_slots, width), hp.dtype),
        grid_spec=grid_spec,
        compiler_params=_params(("arbitrary",)),
        name="dispatch",
    )(counts, pad_start, padded, n_pad, pos, hp)


def _experts_kernel(be_ref, nact_ref, x_ref, wg_ref, wu_ref, wd_ref, y_ref, wg_s, wu_s, wd_s):
    i = pl.program_id(0)
    last = nact_ref[0] - 1
    expert = be_ref[jnp.minimum(i, last)]
    previous = be_ref[jnp.minimum(jnp.maximum(i, 1) - 1, last)]

    @pl.when((i == 0) | (expert != previous))
    def _():
        wg_s[...] = wg_ref[...].astype(BF16)
        wu_s[...] = wu_ref[...].astype(BF16)
        wd_s[...] = wd_ref[...].astype(BF16)

    @pl.when(i <= last)
    def _():
        half = x_ref.shape[1]
        x_hi, x_lo = (v.astype(BF16) for v in _unpack_bf16_pairs(x_ref[...]))
        hg = _dot(x_hi, wg_s[:half, :]) + _dot(x_lo, wg_s[half:, :])
        hu = _dot(x_hi, wu_s[:half, :]) + _dot(x_lo, wu_s[half:, :])
        y_ref[...] = _pack_bf16_pairs(_dot((_silu(hg) * hu).astype(BF16), wd_s[...]))


def _experts_call(block_expert, n_active, xs, wg, wu, wd):
    n_blocks = block_expert.shape[0]
    n_slots, half = xs.shape
    bm = n_slots // n_blocks
    d, hidden = wg.shape[1:]
    blk = lambda i, be, na: (jnp.minimum(i, na[0] - 1), 0)
    wsel = lambda i, be, na: (be[jnp.minimum(i, na[0] - 1)], 0, 0)
    grid_spec = pltpu.PrefetchScalarGridSpec(
        num_scalar_prefetch=2,
        grid=(n_blocks,),
        in_specs=[pl.BlockSpec((bm, half), blk),
                  pl.BlockSpec((None, d, hidden), wsel),
                  pl.BlockSpec((None, d, hidden), wsel),
                  pl.BlockSpec((None, hidden, d), wsel)],
        out_specs=pl.BlockSpec((bm, half), blk),
        scratch_shapes=[pltpu.VMEM((d, hidden), BF16), pltpu.VMEM((d, hidden), BF16), pltpu.VMEM((hidden, d), BF16)],
    )
    return pl.pallas_call(
        _experts_kernel,
        out_shape=jax.ShapeDtypeStruct((n_slots, half), xs.dtype),
        grid_spec=grid_spec,
        compiler_params=_params(("arbitrary",)),
        name="experts",
    )(block_expert, n_active, xs, wg, wu, wd)


def _row_gather_start(idx_ref, n_rows, src_hbm, dst_ref, sem):
    tc = idx_ref.shape[1]
    for r in range(n_rows):
        pltpu.make_async_copy(src_hbm.at[pl.ds(idx_ref[r // tc, r % tc], 1), :], dst_ref.at[pl.ds(r, 1), :],
                              sem).start()


def _row_gather_wait(n_rows, src_hbm, dst_ref, sem):
    pltpu.make_async_copy(src_hbm.at[pl.ds(0, n_rows), :], dst_ref, sem).wait()


def _combine_kernel(pos_ref, pos1_ref, pos2_ref, y_hbm, gate_ref, shared_ref, x1_ref, g2_ref, lng_ref, lnb_ref,
                    o_ref, buf_a, buf_b, buf_c, sems, *, alpha):
    i = pl.program_id(0)
    n = pl.num_programs(0)
    rows, half = buf_a.shape
    tc = o_ref.shape[0]
    bufs = (buf_a, buf_b, buf_c)

    @pl.when(i == 0)
    def _():
        _row_gather_start(pos_ref, rows, y_hbm, buf_a, sems.at[0])
        _row_gather_start(pos1_ref, rows, y_hbm, buf_b, sems.at[1])

    def step(r):
        cur, nxt, far = bufs[r], bufs[(r + 1) % 3], bufs[(r + 2) % 3]
        _row_gather_wait(rows, y_hbm, cur, sems.at[r])
        _row_gather_start(pos2_ref, rows, y_hbm, far, sems.at[(r + 2) % 3])
        r_id = lax.broadcasted_iota(jnp.int32, (tc, tc), 0)
        c_id = lax.broadcasted_iota(jnp.int32, (tc, tc), 1)
        f_hi = shared_ref[:, :half]
        f_lo = shared_ref[:, half:]
        for k in range(rows // tc):
            y_hi, y_lo = _unpack_bf16_pairs(cur[k * tc:(k + 1) * tc, :])
            g = jnp.sum(jnp.where(r_id == c_id, gate_ref[k:k + 1, :], 0.0), axis=1, keepdims=True)
            f_hi = f_hi + g * y_hi
            f_lo = f_lo + g * y_lo
        f = jnp.concatenate([f_hi, f_lo], axis=1)
        o_ref[...] = _norm_rows(alpha * x1_ref[...] + g2_ref[...] * f, LN_EPS) * lng_ref[...] + lnb_ref[...]

        @pl.when(i == n - 1)
        def _():
            _row_gather_wait(rows, y_hbm, nxt, sems.at[(r + 1) % 3])
            _row_gather_wait(rows, y_hbm, far, sems.at[(r + 2) % 3])

    for r in range(3):
        @pl.when(i % 3 == r)
        def _():
            step(r)


def _combine_call(pos, y_slots, gates, shared, x1, g2, lng, lnb, *, alpha, tc, tiles_per_batch):
    tokens, d = x1.shape
    n_tiles = tokens // tc
    last = n_tiles - 1
    rows = TOP_K * tc
    row = pl.BlockSpec((tc, d), lambda i: (i, 0))
    buf = pltpu.VMEM((rows, y_slots.shape[1]), y_slots.dtype)
    return pl.pallas_call(
        functools.partial(_combine_kernel, alpha=alpha),
        out_shape=jax.ShapeDtypeStruct((tokens, d), F32),
        grid=(n_tiles,),
        in_specs=[
            pl.BlockSpec((TOP_K, tc), lambda i: (0, i), memory_space=pltpu.SMEM),
            pl.BlockSpec((TOP_K, tc), lambda i: (0, jnp.minimum(i + 1, last)), memory_space=pltpu.SMEM),
            pl.BlockSpec((TOP_K, tc), lambda i: (0, jnp.minimum(i + 2, last)), memory_space=pltpu.SMEM),
            pl.BlockSpec(memory_space=pl.ANY),
            pl.BlockSpec((TOP_K, tc), lambda i: (0, i)),
            row, row,
            pl.BlockSpec((None, 1, d), lambda i: (i // tiles_per_batch, 0, 0)),
            _const_spec(lng.shape), _const_spec(lnb.shape),
        ],
        out_specs=row,
        scratch_shapes=[buf, buf, buf, pltpu.SemaphoreType.DMA((3,))],
        compiler_params=_params(("arbitrary",)),
        name="combine",
    )(pos, pos, pos, y_slots, gates, shared, x1, g2, lng, lnb)


def kernel(x, c, ctx, c_ctx, w_mod, b_mod, w_in, s5_lam_re, s5_lam_im, s5_log_dt, s5_b_re, s5_b_im, s5_c_re, s5_c_im, s5_d, w_glu, q_norm_g, k_norm_g, w_branch_ssm, w_branch_attn, w_out, ln1_g, ln1_b, w_router, router_bias, w_exp_gate, w_exp_up, w_exp_down, w_sh_gate, w_sh_up, w_sh_down, ln2_g, ln2_b):
    depth = w_mod.shape[0]
    assert depth == 1, "single-layer block: context outputs are never needed"
    bsz, seq, d = x.shape
    ctx_len = ctx.shape[1]
    kvw = N_KV_HEADS * HEAD_DIM
    n_exp = w_router.shape[2]
    alpha = (2.0 * depth) ** 0.25
    assert seq % GRID_W == 0 and seq % S5_CHUNK == 0 and ctx_len % S5_CHUNK == 0 and bsz % SUBLANES == 0
    lay = 0

    pad = (-(bsz + 1)) % SUBLANES
    c_all = jnp.concatenate([c, c_ctx[None, :], jnp.zeros((pad, d), F32)], axis=0)
    mod = _mod_call(c_all, w_mod[lay], b_mod[lay])
    mod_lat = mod[:bsz].reshape(bsz, N_MOD, 1, d)
    sh1, sc1, g1, sh2, sc2, g2 = (mod_lat[:, k] for k in range(N_MOD))
    mod_ctx = mod[bsz].reshape(N_MOD, 1, 1, d)

    w_in_l = w_in[lay].astype(BF16)
    w_ctx = jnp.concatenate([w_in_l[:, :d], w_in_l[:, 2 * d:2 * d + 2 * kvw]], axis=1)
    head_id = jnp.arange(kvw) // HEAD_DIM
    bd = jnp.where(head_id[:, None] == head_id[None, :], 1.0 / HEAD_DIM, 0.0).astype(BF16)
    tqa, tqb = _rope_tables(seq, q_norm_g[lay], HEAD_DIM ** -0.5)
    tka, tkb = _rope_tables(seq, k_norm_g[lay], 1.0)
    tca = jnp.tile(k_norm_g[lay].astype(F32), N_KV_HEADS)[None, :]
    tm_lat = min(512, seq)
    s_lat, k_lat, v_lat, q_lat, gs_lat, ga_lat = _inproj_call(x, sh1, sc1, w_in_l, bd, tka, tkb, tqa, tqb, tm=tm_lat)
    s_ctx, k_ctx, v_ctx = _inproj_call(ctx, mod_ctx[0], mod_ctx[1], w_ctx, bd, tca, jnp.zeros_like(tca),
                                       tm=min(256, ctx_len))

    w1, wso, lam_tab = _s5_tables(s5_lam_re[lay], s5_lam_im[lay], s5_log_dt[lay], s5_b_re[lay], s5_b_im[lay],
                                  s5_c_re[lay], s5_c_im[lay])
    y_groups = _s5_call(_to_group_major(s_ctx), _to_group_major(s_lat), w1, wso, lam_tab, nb=bsz)
    y_s5 = _to_token_major(y_groups, bsz)

    attn = _attn_call(q_lat, k_ctx, k_lat, v_ctx, v_lat, tq=min(512, seq))

    row = lambda v: v.astype(F32).reshape(1, -1)
    wss = jnp.concatenate([w_sh_gate[lay], w_sh_up[lay]], axis=1).astype(BF16)
    x1, hp, shared, scores_t = _mixout_call(
        y_s5, s_lat, attn, gs_lat, ga_lat, x, g1, sh2, sc2, row(s5_d[lay]),
        w_glu[lay].astype(BF16), w_branch_ssm[lay].astype(BF16), w_branch_attn[lay].astype(BF16),
        w_out[lay].astype(BF16), row(ln1_g[lay]), row(ln1_b[lay]), w_router[lay].T.astype(BF16), wss,
        w_sh_down[lay].astype(BF16), tm=min(256, seq), alpha=alpha)

    tokens = bsz * seq
    bm = MOE_BLOCK
    idx, gates, rank, counts_f = _route_call(scores_t, router_bias[lay], tm=min(ROUTE_TOKENS, seq))
    counts = counts_f.reshape(n_exp).astype(jnp.int32)
    padded = (counts + bm - 1) // bm * bm
    pad_end = jnp.cumsum(padded)
    pad_start = pad_end - padded
    n_blocks = (tokens * TOP_K + n_exp * (bm - 1) + bm - 1) // bm
    block_start = jnp.arange(n_blocks, dtype=jnp.int32) * bm
    block_expert = jnp.minimum(jnp.sum((pad_end[None, :] <= block_start[:, None]).astype(jnp.int32), axis=1), n_exp - 1)
    n_active = (pad_end[-1] // bm).reshape(1)
    n_pad = jnp.sum(padded - counts).reshape(1)
    pos = _slot_call(idx, rank, pad_start, tm=min(512, tokens))
    xs = _dispatch_call(counts, pad_start, padded, n_pad, pos, hp.reshape(tokens, d // 2), n_blocks * bm,
                        td=min(DISPATCH_TOKENS, tokens))
    y_slots = _experts_call(block_expert, n_active, xs, w_exp_gate[lay], w_exp_up[lay], w_exp_down[lay])

    tc = min(COMBINE_TOKENS, seq)
    out = _combine_call(pos, y_slots, gates, shared.reshape(tokens, d), x1.reshape(tokens, d), g2,
                        row(ln2_g[lay]), row(ln2_b[lay]), alpha=alpha, tc=tc, tiles_per_batch=seq // tc)
    return out.reshape(bsz, seq, d)
```

```python
import functools
import math

import jax
import jax.numpy as jnp
from jax import lax
from jax.experimental import pallas as pl
from jax.experimental.pallas import tpu as pltpu

F32 = jnp.float32
BF16 = jnp.bfloat16

GRID_W = 64
HEAD_DIM = 64
N_KV_HEADS = 4
S5_GROUP_CH = 16
S5_MAX_RE = -1e-4
ROPE_THETA = 10000.0
TOP_K = 8
N_EXPERT_GROUPS = 8
TOPK_GROUPS = 4
ROUTED_SCALE = 2.5
LN_EPS = 1e-5
MOD_EPS = 1e-6
RMS_EPS = 1e-6
N_MOD = 6

LANES = 128
SUBLANES = 8
VMEM_LIMIT_BYTES = 56 * 1024 * 1024

S5_CHUNK = 16
S5_COLS = S5_CHUNK * S5_GROUP_CH
MOE_BLOCK = 512
ROUTE_TOKENS = 512
DISPATCH_TOKENS = 128
DISPATCH_BUFFERS = 3
COMBINE_TOKENS = 128


def _params(sem):
    return pltpu.CompilerParams(dimension_semantics=sem, vmem_limit_bytes=VMEM_LIMIT_BYTES)


def _const_spec(shape):
    nd = len(shape)
    return pl.BlockSpec(shape, lambda *_: (0,) * nd, pipeline_mode=pl.Buffered(1))


def _dot(a, b):
    return jnp.dot(a, b, preferred_element_type=F32)


def _norm_rows(x, eps):
    mu = jnp.mean(x, axis=-1, keepdims=True)
    xc = x - mu
    var = jnp.mean(xc * xc, axis=-1, keepdims=True)
    return xc * lax.rsqrt(var + eps)


def _silu(x):
    return x * jax.nn.sigmoid(x)


def _gelu_tanh(x):
    return 0.5 * x * (1.0 + jnp.tanh(math.sqrt(2.0 / math.pi) * (x + 0.044715 * (x * x * x))))


def _pack_bf16_pairs(v):
    n = v.shape[1] // 2
    bits = lax.bitcast_convert_type(v.astype(BF16).astype(F32), jnp.uint32)
    return bits[:, :n] | (bits[:, n:] >> 16)


def _unpack_bf16_pairs(w):
    hi = lax.bitcast_convert_type(w & jnp.uint32(0xFFFF0000), F32)
    lo = lax.bitcast_convert_type(w << 16, F32)
    return hi, lo


def _mod_kernel(c_ref, w_ref, b_ref, o_ref):
    o_ref[...] = _dot(_silu(c_ref[...]), w_ref[...]) + b_ref[...]


def _mod_call(c_all, w_mod, b_mod):
    rows, d = c_all.shape
    n = w_mod.shape[1]
    tn = d
    return pl.pallas_call(
        _mod_kernel,
        out_shape=jax.ShapeDtypeStruct((rows, n), F32),
        grid=(n // tn,),
        in_specs=[pl.BlockSpec((rows, d), lambda j: (0, 0)),
                  pl.BlockSpec((d, tn), lambda j: (0, j)),
                  pl.BlockSpec((1, tn), lambda j: (0, j))],
        out_specs=pl.BlockSpec((rows, tn), lambda j: (0, j)),
        compiler_params=_params(("arbitrary",)),
        name="mod",
    )(c_all, w_mod, b_mod.reshape(1, n))


def _swap16(t):
    width = t.shape[1]
    lane = lax.broadcasted_iota(jnp.int32, t.shape, 1)
    first = (lane & 16) == 0
    return jnp.where(first, pltpu.roll(t, width - 16, 1), pltpu.roll(t, 16, 1))


def _rms_rope(t, bd_ref, ta, tb):
    msq = _dot((t * t).astype(BF16), bd_ref[...])
    return lax.rsqrt(msq + RMS_EPS) * (t * ta + _swap16(t) * tb)


def _store_padded_heads(t, o_ref, ones_lane=False):
    rows = t.shape[0]
    lane = lax.broadcasted_iota(jnp.int32, (rows, LANES), 1)
    lo = lane < HEAD_DIM
    fill_hi = jnp.where(lane == HEAD_DIM, 1.0, 0.0) if ones_lane else 0.0
    fill_lo = jnp.where(lane == 0, 1.0, 0.0) if ones_lane else 0.0
    for j in range(t.shape[1] // LANES):
        slab = t[:, j * LANES:(j + 1) * LANES]
        swapped = pltpu.roll(slab, HEAD_DIM, 1)
        pieces = (jnp.where(lo, slab, fill_hi), jnp.where(lo, fill_lo, swapped),
                  jnp.where(lo, swapped, fill_hi), jnp.where(lo, fill_lo, slab))
        for p, piece in enumerate(pieces):
            c0 = (4 * j + p) * LANES
            o_ref[:, c0:c0 + LANES] = piece.astype(o_ref.dtype)


def _inproj_kernel(x_ref, sh_ref, sc_ref, w_ref, bd_ref, tka_ref, tkb_ref, *rest, d, kvw, has_q):
    if has_q:
        tqa_ref, tqb_ref, s_ref, k_ref, v_ref, q_ref, gs_ref, ga_ref = rest
    else:
        s_ref, k_ref, v_ref = rest
    u = (_norm_rows(x_ref[...], MOD_EPS) * (1.0 + sc_ref[...]) + sh_ref[...]).astype(BF16)
    col = 0
    s_ref[...] = _dot(u, w_ref[:, col:col + d]).astype(s_ref.dtype)
    col += d
    if has_q:
        for c in range(d // kvw):
            q = _dot(u, w_ref[:, col + c * kvw:col + (c + 1) * kvw])
            q_ref[:, c * kvw:(c + 1) * kvw] = _rms_rope(q, bd_ref, tqa_ref[...], tqb_ref[...]).astype(q_ref.dtype)
        col += d
    k = _dot(u, w_ref[:, col:col + kvw])
    _store_padded_heads(_rms_rope(k, bd_ref, tka_ref[...], tkb_ref[...]), k_ref)
    col += kvw
    _store_padded_heads(_dot(u, w_ref[:, col:col + kvw]), v_ref, ones_lane=True)
    col += kvw
    if has_q:
        gs_ref[...] = jax.nn.sigmoid(_dot(u, w_ref[:, col:col + d])).astype(gs_ref.dtype)
        col += d
        ga_ref[...] = jax.nn.sigmoid(_dot(u, w_ref[:, col:col + d])).astype(ga_ref.dtype)


def _inproj_call(x, shift, scale, w, bd, tka, tkb, tqa=None, tqb=None, *, tm):
    bsz, length, d = x.shape
    has_q = tqa is not None
    kvw = N_KV_HEADS * HEAD_DIM
    per_batch = shift.shape[0] > 1
    tab_rows = tka.shape[0]
    tab_blk = tm if tab_rows > 1 else 1
    mod_spec = pl.BlockSpec((None, 1, d), (lambda b, i: (b, 0, 0)) if per_batch else (lambda b, i: (0, 0, 0)))
    tab_spec = pl.BlockSpec((tab_blk, kvw), (lambda b, i: (i, 0)) if tab_rows > 1 else (lambda b, i: (0, 0)))
    row_spec = lambda width: pl.BlockSpec((None, tm, width), lambda b, i: (b, i, 0))
    in_specs = [row_spec(d), mod_spec, mod_spec, _const_spec(w.shape), _const_spec(bd.shape), tab_spec, tab_spec]
    args = [x, shift, scale, w, bd, tka, tkb]
    widths = [d, 4 * kvw, 4 * kvw]
    if has_q:
        in_specs += [tab_spec, tab_spec]
        args += [tqa, tqb]
        widths += [d, d, d]
    return pl.pallas_call(
        functools.partial(_inproj_kernel, d=d, kvw=kvw, has_q=has_q),
        out_shape=[jax.ShapeDtypeStruct((bsz, length, wd), BF16) for wd in widths],
        grid=(bsz, length // tm),
        in_specs=in_specs,
        out_specs=[row_spec(wd) for wd in widths],
        compiler_params=_params(("parallel", "parallel")),
        name="inproj_lat" if has_q else "inproj_ctx",
    )(*args)


def _rope_tables(seq, gain, scale):
    half = HEAD_DIM // 2
    inv_freq = ROPE_THETA ** (-jnp.arange(0, half, 2, dtype=F32) / half)
    t = jnp.arange(seq, dtype=jnp.int32)
    pos = jnp.stack([(t // GRID_W).astype(F32), (t % GRID_W).astype(F32)], axis=1)
    dim = jnp.arange(HEAD_DIM)
    axis = dim // half
    second = ((dim % half) // (half // 2)) == 1
    freq = inv_freq[dim % (half // 2)]
    ang = pos[:, axis] * freq[None, :]
    partner = jnp.where(second, dim - half // 2, dim + half // 2)
    g = gain.astype(F32)
    ta = jnp.cos(ang) * g[None, :] * scale
    tb = jnp.sin(ang) * jnp.where(second, 1.0, -1.0)[None, :] * g[partner][None, :] * scale
    return jnp.tile(ta, (1, N_KV_HEADS)), jnp.tile(tb, (1, N_KV_HEADS))


def _attn_kernel(q_ref, kc_ref, kl_ref, vc_ref, vl_ref, o_ref):
    nt = (((1,), (1,)), ((), ()))
    rows = q_ref.shape[0]
    lane = lax.broadcasted_iota(jnp.int32, (rows, LANES), 1)
    for j in range(q_ref.shape[1] // LANES):
        qs = q_ref[:, j * LANES:(j + 1) * LANES]
        acc = []
        for half in range(2):
            cols = slice(half * LANES, (half + 1) * LANES)
            s_c = lax.dot_general(qs, kc_ref[:, cols], nt, preferred_element_type=F32)
            s_l = lax.dot_general(qs, kl_ref[:, cols], nt, preferred_element_type=F32)
            m = jnp.maximum(jnp.max(s_c, axis=-1, keepdims=True), jnp.max(s_l, axis=-1, keepdims=True))
            e_c = jnp.exp((s_c - m).astype(BF16))
            e_l = jnp.exp((s_l - m).astype(BF16))
            acc.append(_dot(e_c, vc_ref[:, cols]) + _dot(e_l, vl_ref[:, cols]))
        out = jnp.where(lane < HEAD_DIM, acc[0] / acc[0][:, HEAD_DIM:HEAD_DIM + 1], acc[1] / acc[1][:, 0:1])
        o_ref[:, j * LANES:(j + 1) * LANES] = out.astype(o_ref.dtype)


def _attn_call(q, kc, kl, vc, vl, *, tq):
    bsz, seq, d = q.shape
    ctx = kc.shape[1]
    gw = d // N_KV_HEADS
    q_spec = pl.BlockSpec((None, tq, gw), lambda b, h, i: (b, i, h))
    kv_spec = lambda length: pl.BlockSpec((None, length, 2 * LANES), lambda b, h, i: (b, 0, h))
    return pl.pallas_call(
        _attn_kernel,
        out_shape=jax.ShapeDtypeStruct((bsz, seq, d), BF16),
        grid=(bsz, N_KV_HEADS, seq // tq),
        in_specs=[q_spec, kv_spec(ctx), kv_spec(seq), kv_spec(ctx), kv_spec(seq)],
        out_specs=q_spec,
        compiler_params=_params(("parallel", "parallel", "arbitrary")),
        name="attention",
    )(q, kc, kl, vc, vl)


def _s5_tables(lam_re, lam_im, log_dt, b_re, b_im, c_re, c_im):
    n = S5_CHUNK
    lam = lax.complex(jnp.minimum(lam_re.astype(F32), S5_MAX_RE), lam_im.astype(F32))
    lam_dt = lam * jnp.exp(log_dt.astype(F32))[..., None]
    b_bar = ((jnp.exp(lam_dt) - 1.0) / lam)[..., None] * lax.complex(b_re.astype(F32), b_im.astype(F32))
    c_mat = lax.complex(c_re.astype(F32), c_im.astype(F32))
    pw = jnp.exp(lam_dt[None] * jnp.arange(n + 1, dtype=F32)[:, None, None, None])
    kern = jnp.real(jnp.einsum('dgcp,jdgp,dgpe->djgce', c_mat, pw[:n], b_bar))
    s_idx = jnp.arange(n)[:, None]
    t_idx = jnp.arange(n)[None, :]

    def toeplitz(k, lag):
        g = k[jnp.clip(lag, 0, n - 1)]
        g = jnp.where((lag >= 0)[:, :, None, None, None], g, 0.0)
        return g.transpose(2, 0, 4, 1, 3)

    toep = toeplitz(kern[0], t_idx - s_idx) + toeplitz(kern[1], s_idx - t_idx)
    groups = toep.shape[0]
    toep = toep.reshape(groups, S5_COLS, S5_COLS)
    wis_f = jnp.einsum('sgp,gpe->gsep', pw[n - 1 - jnp.arange(n), 0], b_bar[0]).reshape(groups, S5_COLS, -1)
    wis_r = jnp.einsum('sgp,gpe->gsep', pw[jnp.arange(n), 1], b_bar[1]).reshape(groups, S5_COLS, -1)
    w1 = jnp.concatenate([toep, jnp.real(wis_f), jnp.real(wis_r), jnp.imag(wis_f), jnp.imag(wis_r)], axis=-1)
    m_f = jnp.einsum('gcp,tgp->gptc', c_mat[0], pw[1 + jnp.arange(n), 0]).reshape(groups, -1, S5_COLS)
    m_r = jnp.einsum('gcp,tgp->gptc', c_mat[1], pw[n - jnp.arange(n), 1]).reshape(groups, -1, S5_COLS)
    wso = jnp.concatenate([jnp.real(m_f), jnp.real(m_r), -jnp.imag(m_f), -jnp.imag(m_r)], axis=1)
    lam_n = jnp.concatenate([pw[n, 0], pw[n, 1]], axis=-1)
    lam_tab = jnp.stack([jnp.real(lam_n), jnp.imag(lam_n)], axis=1)
    return w1.astype(BF16), wso.astype(BF16), lam_tab


def _s5_kernel(uc_ref, ul_ref, w1_ref, wso_ref, lam_ref, y_ref, a_ref, xf_re, xf_im, xr_re, xr_im,
               *, nb, nc_ctx, nc):
    cols = S5_COLS
    half = lam_ref.shape[1] // 2
    a_ref[:nc_ctx * nb, :] = _dot(uc_ref[...], w1_ref[...])
    a_ref[nc_ctx * nb:, :] = _dot(ul_ref[...], w1_ref[...])
    l_re = lam_ref[0:1, :]
    l_im = lam_ref[1:2, :]
    lo = lax.broadcasted_iota(jnp.int32, (nb, 2 * half), 1) < half

    def step(i, carry):
        x_re, x_im = carry
        i_rev = jnp.where(i < nc_ctx, nc_ctx - 1 - i, nc - 1 + nc_ctx - i)
        rf = pl.ds(pl.multiple_of(i * nb, nb), nb)
        rr = pl.ds(pl.multiple_of(i_rev * nb, nb), nb)
        xf_re[rf, :] = x_re
        xf_im[rf, :] = x_im
        xr_re[rr, :] = x_re
        xr_im[rr, :] = x_im
        s_re = jnp.where(lo, a_ref[rf, cols:cols + 2 * half], a_ref[rr, cols:cols + 2 * half])
        s_im = jnp.where(lo, a_ref[rf, cols + 2 * half:cols + 4 * half], a_ref[rr, cols + 2 * half:cols + 4 * half])
        return l_re * x_re - l_im * x_im + s_re, l_re * x_im + l_im * x_re + s_im

    zero = jnp.zeros((nb, 2 * half), F32)
    lax.fori_loop(0, nc, step, (zero, zero))
    r0 = nc_ctx * nb
    rows = (nc - nc_ctx) * nb
    lo_all = lax.broadcasted_iota(jnp.int32, (rows, 2 * half), 1) < half
    y = a_ref[r0:, 0:cols]
    for k, (f_ref, r_ref) in enumerate(((xf_re, xr_re), (xf_im, xr_im))):
        st = jnp.where(lo_all, f_ref[r0:, :], r_ref[r0:, :])
        hi = st.astype(BF16)
        lo_part = (st - hi.astype(F32)).astype(BF16)
        w = wso_ref[k * 2 * half:(k + 1) * 2 * half, :]
        y = y + _dot(hi, w) + _dot(lo_part, w)
    y_ref[...] = y.astype(y_ref.dtype)


def _s5_call(u_ctx, u_lat, w1, wso, lam_tab, *, nb):
    groups, out_rows, cols = u_lat.shape
    ctx_rows = u_ctx.shape[1]
    rows = ctx_rows + out_rows
    nc = rows // nb
    nc_ctx = ctx_rows // nb
    st = lam_tab.shape[2]
    return pl.pallas_call(
        functools.partial(_s5_kernel, nb=nb, nc_ctx=nc_ctx, nc=nc),
        out_shape=jax.ShapeDtypeStruct((groups, out_rows, cols), BF16),
        grid=(groups,),
        in_specs=[pl.BlockSpec((None, ctx_rows, cols), lambda g: (g, 0, 0)),
                  pl.BlockSpec((None, out_rows, cols), lambda g: (g, 0, 0)),
                  pl.BlockSpec((None,) + w1.shape[1:], lambda g: (g, 0, 0)),
                  pl.BlockSpec((None,) + wso.shape[1:], lambda g: (g, 0, 0)),
                  pl.BlockSpec((None, 2, st), lambda g: (g, 0, 0))],
        out_specs=pl.BlockSpec((None, out_rows, cols), lambda g: (g, 0, 0)),
        scratch_shapes=[pltpu.VMEM((rows, w1.shape[2]), F32)] + [pltpu.VMEM((rows, st), F32)] * 4,
        compiler_params=_params(("parallel",)),
        name="s5",
    )(u_ctx, u_lat, w1, wso, lam_tab)


def _to_group_major(s):
    bsz, length, width = s.shape
    groups = width // S5_GROUP_CH
    nc = length // S5_CHUNK
    t = s.reshape(bsz, nc, S5_CHUNK, groups, S5_GROUP_CH).transpose(3, 1, 0, 2, 4)
    return t.reshape(groups, nc * bsz, S5_COLS)


def _to_token_major(y, bsz):
    groups, rows, _ = y.shape
    nc = rows // bsz
    t = y.reshape(groups, nc, bsz, S5_CHUNK, S5_GROUP_CH).transpose(2, 1, 3, 0, 4)
    return t.reshape(bsz, nc * S5_CHUNK, groups * S5_GROUP_CH)


def _mixout_kernel(y_ref, s_ref, at_ref, gs_ref, ga_ref, x_ref, g1_ref, sh2_ref, sc2_ref, dsk_ref,
                   wglu_ref, wbs_ref, wba_ref, wout_ref, lng_ref, lnb_ref, wrt_ref, wss_ref, wsd_ref,
                   x1_ref, h_ref, shared_ref, scores_ref, *, alpha):
    y = y_ref[...].astype(F32) + s_ref[...].astype(F32) * dsk_ref[...]
    z = _gelu_tanh(y)
    ssm = z * jax.nn.sigmoid(_dot(z.astype(BF16), wglu_ref[...]))
    merged = (gs_ref[...].astype(F32) * _dot(ssm.astype(BF16), wbs_ref[...])
              + ga_ref[...].astype(F32) * _dot(at_ref[...], wba_ref[...]))
    y_mix = _dot(merged.astype(BF16), wout_ref[...])
    x1 = _norm_rows(alpha * x_ref[...] + g1_ref[...] * y_mix, LN_EPS) * lng_ref[...] + lnb_ref[...]
    x1_ref[...] = x1
    h = _norm_rows(x1, MOD_EPS) * (1.0 + sc2_ref[...]) + sh2_ref[...]
    h_ref[...] = _pack_bf16_pairs(h)
    hb = h.astype(BF16)
    scores_ref[...] = jax.nn.sigmoid(
        lax.dot_general(wrt_ref[...], hb, (((1,), (1,)), ((), ())), preferred_element_type=F32))
    ss = _dot(hb, wss_ref[...])
    sh_hidden = wsd_ref.shape[0]
    hid = _silu(ss[:, :sh_hidden]) * ss[:, sh_hidden:]
    shared_ref[...] = _dot(hid.astype(BF16), wsd_ref[...])


def _mixout_call(y, s, attn, gs, ga, x, g1, sh2, sc2, dsk, wglu, wbs, wba, wout, lng, lnb, wrt, wss, wsd,
                 *, tm, alpha):
    bsz, seq, d = x.shape
    n_exp = wrt.shape[0]
    row = lambda width: pl.BlockSpec((None, tm, width), lambda b, i: (b, i, 0))
    mod = pl.BlockSpec((None, 1, d), lambda b, i: (b, 0, 0))
    consts = [dsk, wglu, wbs, wba, wout, lng, lnb, wrt, wss, wsd]
    return pl.pallas_call(
        functools.partial(_mixout_kernel, alpha=alpha),
        out_shape=[jax.ShapeDtypeStruct((bsz, seq, d), F32), jax.ShapeDtypeStruct((bsz, seq, d // 2), jnp.uint32),
                   jax.ShapeDtypeStruct((bsz, seq, d), F32), jax.ShapeDtypeStruct((bsz, n_exp, seq), F32)],
        grid=(bsz, seq // tm),
        in_specs=[row(d)] * 6 + [mod] * 3 + [_const_spec(a.shape) for a in consts],
        out_specs=[row(d), row(d // 2), row(d), pl.BlockSpec((None, n_exp, tm), lambda b, i: (b, 0, i))],
        compiler_params=_params(("parallel", "parallel")),
        name="mixout",
    )(y, s, attn, gs, ga, x, g1, sh2, sc2, *consts)


def _stack_rows(rows):
    n = rows[0].shape[1]
    row_k = lax.broadcasted_iota(jnp.int32, (len(rows), n), 0)
    out = jnp.zeros((len(rows), n), rows[0].dtype)
    for k, r in enumerate(rows):
        out = jnp.where(row_k == k, r, out)
    return out


def _route_kernel(s_ref, bias_ref, utri_ref, idx_ref, gate_ref, rank_ref, counts_ref, base_ref):
    @pl.when((pl.program_id(0) == 0) & (pl.program_id(1) == 0))
    def _():
        base_ref[...] = jnp.zeros_like(base_ref)

    s = s_ref[...]
    n_exp, tm = s.shape
    per_group = n_exp // N_EXPERT_GROUPS
    neg = -jnp.inf
    b = s + bias_ref[...]
    bg = b.reshape(N_EXPERT_GROUPS, per_group, tm)
    m1 = jnp.max(bg, axis=1, keepdims=True)
    is1 = bg == m1
    n1 = jnp.sum(jnp.where(is1, 1.0, 0.0), axis=1, keepdims=True)
    m2 = jnp.max(jnp.where(is1, neg, bg), axis=1, keepdims=True)
    gscore = m1 + jnp.where(n1 >= 2.0, m1, m2)
    gs = [gscore[g] for g in range(N_EXPERT_GROUPS)]
    kept = []
    for g in range(N_EXPERT_GROUPS):
        beaten = jnp.zeros((1, tm), F32)
        for o in range(N_EXPERT_GROUPS):
            if o != g:
                wins = (gs[o] >= gs[g]) if o < g else (gs[o] > gs[g])
                beaten = beaten + jnp.where(wins, 1.0, 0.0)
        kept.append(jnp.where(beaten < float(TOPK_GROUPS), bg[g], neg))
    masked = jnp.concatenate(kept, axis=0)
    row_f = lax.broadcasted_iota(jnp.int32, (n_exp, tm), 0).astype(F32)
    chosen = jnp.zeros((n_exp, tm), F32)
    firsts, vals = [], []
    for _ in range(TOP_K):
        m = jnp.max(masked, axis=0, keepdims=True)
        first = jnp.min(jnp.where(masked == m, row_f, float(n_exp)), axis=0, keepdims=True)
        sel = row_f == first
        firsts.append(first)
        vals.append(jnp.sum(jnp.where(sel, s, 0.0), axis=0, keepdims=True))
        chosen = jnp.where(sel, 1.0, chosen)
        masked = jnp.where(sel, neg, masked)
    before = _dot(chosen.astype(BF16), utri_ref[...]) + base_ref[...]
    ranks = [jnp.sum(jnp.where(row_f == f, before, 0.0), axis=0, keepdims=True) for f in firsts]
    idx_ref[...] = _stack_rows(firsts).astype(jnp.int32)
    rank_ref[...] = _stack_rows(ranks).astype(jnp.int32)
    val = _stack_rows(vals)
    gate_ref[...] = val / jnp.sum(val, axis=0, keepdims=True) * ROUTED_SCALE
    base_ref[...] = base_ref[...] + jnp.sum(chosen, axis=1, keepdims=True)
    counts_ref[...] = base_ref[...]


def _route_call(scores_t, bias, *, tm):
    bsz, n_exp, seq = scores_t.shape
    tiles = seq // tm
    tokens = bsz * seq
    r_id = lax.broadcasted_iota(jnp.int32, (tm, tm), 0)
    c_id = lax.broadcasted_iota(jnp.int32, (tm, tm), 1)
    utri = (r_id < c_id).astype(BF16)
    small = pl.BlockSpec((TOP_K, tm), lambda b, i: (0, b * tiles + i))
    col = pl.BlockSpec((n_exp, 1), lambda b, i: (0, 0))
    return pl.pallas_call(
        _route_kernel,
        out_shape=[jax.ShapeDtypeStruct((TOP_K, tokens), jnp.int32), jax.ShapeDtypeStruct((TOP_K, tokens), F32),
                   jax.ShapeDtypeStruct((TOP_K, tokens), jnp.int32), jax.ShapeDtypeStruct((n_exp, 1), F32)],
        grid=(bsz, tiles),
        in_specs=[pl.BlockSpec((None, n_exp, tm), lambda b, i: (b, 0, i)), col, _const_spec((tm, tm))],
        out_specs=[small, small, small, col],
        scratch_shapes=[pltpu.VMEM((n_exp, 1), F32)],
        compiler_params=_params(("arbitrary", "arbitrary")),
        name="route",
    )(scores_t, bias.astype(F32).reshape(n_exp, 1), utri)


def _slot_kernel(idx_ref, rank_ref, start_ref, pos_ref):
    idx = idx_ref[...]
    tm = idx.shape[1]
    n_exp = start_ref.shape[0]
    row = lax.broadcasted_iota(jnp.int32, (n_exp, tm), 0)
    starts = [jnp.sum(jnp.where(row == idx[k:k + 1, :], start_ref[...], 0.0), axis=0, keepdims=True)
              for k in range(TOP_K)]
    pos_ref[...] = rank_ref[...] + _stack_rows(starts).astype(jnp.int32)


def _slot_call(idx, rank, pad_start, *, tm):
    tokens = idx.shape[1]
    n_exp = pad_start.shape[0]
    small = pl.BlockSpec((TOP_K, tm), lambda i: (0, i))
    return pl.pallas_call(
        _slot_kernel,
        out_shape=jax.ShapeDtypeStruct((TOP_K, tokens), jnp.int32),
        grid=(tokens // tm,),
        in_specs=[small, small, _const_spec((n_exp, 1))],
        out_specs=small,
        compiler_params=_params(("parallel",)),
        name="slots",
    )(idx, rank, pad_start.astype(F32).reshape(n_exp, 1))


def _dispatch_kernel(cnt_ref, start_ref, padded_ref, npad_ref, pos_ref, h_hbm, xs_hbm, hbuf, zrow, lsem, ssem, zsem):
    i = pl.program_id(0)
    n = pl.num_programs(0)
    nbuf, td = hbuf.shape[:2]
    n_exp = cnt_ref.shape[0]
    slot = i % nbuf

    def zero_copy(row):
        return pltpu.make_async_copy(zrow, xs_hbm.at[pl.ds(row, 1), :], zsem)

    def load(step, s):
        return pltpu.make_async_copy(h_hbm.at[pl.ds(step * td, td), :], hbuf.at[s], lsem.at[s])

    def wait_scatters(s):
        rows = td * TOP_K
        pltpu.make_async_copy(h_hbm.at[pl.ds(0, rows), :], xs_hbm.at[pl.ds(0, rows), :], ssem.at[s]).wait()

    @pl.when(i == 0)
    def _():
        zrow[...] = jnp.zeros_like(zrow)

        def per_expert(e, carry):
            def one(r, c2):
                zero_copy(start_ref[e] + r).start()
                return c2
            return lax.fori_loop(cnt_ref[e], padded_ref[e], one, carry)
        lax.fori_loop(0, n_exp, per_expert, 0)
        load(0, 0).start()

    @pl.when(i >= nbuf - 1)
    def _():
        wait_scatters((i + 1) % nbuf)

    @pl.when(i + 1 < n)
    def _():
        load(i + 1, (i + 1) % nbuf).start()

    load(i, slot).wait()

    for t in range(td):
        for k in range(TOP_K):
            pltpu.make_async_copy(hbuf.at[slot, pl.ds(t, 1), :], xs_hbm.at[pl.ds(pos_ref[k, t], 1), :],
                                  ssem.at[slot]).start(priority=k % 2)

    @pl.when(i == n - 1)
    def _():
        for back in range(nbuf - 1):
            @pl.when(i >= back)
            def _():
                wait_scatters((i - back) % nbuf)

        def one(r, c2):
            zero_copy(0).wait()
            return c2
        lax.fori_loop(0, npad_ref[0], one, 0)


def _dispatch_call(counts, pad_start, padded, n_pad, pos, hp, n_slots, *, td):
    tokens, width = hp.shape
    grid_spec = pltpu.PrefetchScalarGridSpec(
        num_scalar_prefetch=4,
        grid=(tokens // td,),
        in_specs=[pl.BlockSpec((TOP_K, td), lambda i, *_: (0, i), memory_space=pltpu.SMEM),
                  pl.BlockSpec(memory_space=pl.ANY)],
        out_specs=pl.BlockSpec(memory_space=pl.ANY),
        scratch_shapes=[pltpu.VMEM((DISPATCH_BUFFERS, td, width), hp.dtype), pltpu.VMEM((1, width), hp.dtype),
                        pltpu.SemaphoreType.DMA((DISPATCH_BUFFERS,)), pltpu.SemaphoreType.DMA((DISPATCH_BUFFERS,)),
                        pltpu.SemaphoreType.DMA],
    )
    return pl.pallas_call(
        _dispatch_kernel,
        out_shape=jax.ShapeDtypeStruct((n_slots, width), hp.dtype),
        grid_spec=grid_spec,
        compiler_params=_params(("arbitrary",)),
        name="dispatch",
    )(counts, pad_start, padded, n_pad, pos, hp)


def _experts_kernel(be_ref, nact_ref, x_ref, wg_ref, wu_ref, wd_ref, y_ref, wg_s, wu_s, wd_s):
    i = pl.program_id(0)
    last = nact_ref[0] - 1
    expert = be_ref[jnp.minimum(i, last)]
    previous = be_ref[jnp.minimum(jnp.maximum(i, 1) - 1, last)]

    @pl.when((i == 0) | (expert != previous))
    def _():
        wg_s[...] = wg_ref[...].astype(BF16)
        wu_s[...] = wu_ref[...].astype(BF16)
        wd_s[...] = wd_ref[...].astype(BF16)

    @pl.when(i <= last)
    def _():
        half = x_ref.shape[1]
        x_hi, x_lo = (v.astype(BF16) for v in _unpack_bf16_pairs(x_ref[...]))
        hg = _dot(x_hi, wg_s[:half, :]) + _dot(x_lo, wg_s[half:, :])
        hu = _dot(x_hi, wu_s[:half, :]) + _dot(x_lo, wu_s[half:, :])
        y_ref[...] = _pack_bf16_pairs(_dot((_silu(hg) * hu).astype(BF16), wd_s[...]))


def _experts_call(block_expert, n_active, xs, wg, wu, wd):
    n_blocks = block_expert.shape[0]
    n_slots, half = xs.shape
    bm = n_slots // n_blocks
    d, hidden = wg.shape[1:]
    blk = lambda i, be, na: (jnp.minimum(i, na[0] - 1), 0)
    wsel = lambda i, be, na: (be[jnp.minimum(i, na[0] - 1)], 0, 0)
    grid_spec = pltpu.PrefetchScalarGridSpec(
        num_scalar_prefetch=2,
        grid=(n_blocks,),
        in_specs=[pl.BlockSpec((bm, half), blk),
                  pl.BlockSpec((None, d, hidden), wsel),
                  pl.BlockSpec((None, d, hidden), wsel),
                  pl.BlockSpec((None, hidden, d), wsel)],
        out_specs=pl.BlockSpec((bm, half), blk),
        scratch_shapes=[pltpu.VMEM((d, hidden), BF16), pltpu.VMEM((d, hidden), BF16), pltpu.VMEM((hidden, d), BF16)],
    )
    return pl.pallas_call(
        _experts_kernel,
        out_shape=jax.ShapeDtypeStruct((n_slots, half), xs.dtype),
        grid_spec=grid_spec,
        compiler_params=_params(("arbitrary",)),
        name="experts",
    )(block_expert, n_active, xs, wg, wu, wd)


def _row_gather_start(idx_ref, n_rows, src_hbm, dst_ref, sem):
    tc = idx_ref.shape[1]
    for r in range(n_rows):
        pltpu.make_async_copy(src_hbm.at[pl.ds(idx_ref[r // tc, r % tc], 1), :], dst_ref.at[pl.ds(r, 1), :],
                              sem).start(priority=r % 2)


def _row_gather_wait(n_rows, src_hbm, dst_ref, sem):
    pltpu.make_async_copy(src_hbm.at[pl.ds(0, n_rows), :], dst_ref, sem).wait()


def _combine_kernel(pos_ref, pos1_ref, pos2_ref, y_hbm, gate_ref, shared_ref, x1_ref, g2_ref, lng_ref, lnb_ref,
                    o_ref, buf_a, buf_b, buf_c, sems, *, alpha):
    i = pl.program_id(0)
    n = pl.num_programs(0)
    rows, half = buf_a.shape
    tc = o_ref.shape[0]
    bufs = (buf_a, buf_b, buf_c)

    @pl.when(i == 0)
    def _():
        _row_gather_start(pos_ref, rows, y_hbm, buf_a, sems.at[0])
        _row_gather_start(pos1_ref, rows, y_hbm, buf_b, sems.at[1])

    def step(r):
        cur, nxt, far = bufs[r], bufs[(r + 1) % 3], bufs[(r + 2) % 3]
        _row_gather_wait(rows, y_hbm, cur, sems.at[r])
        _row_gather_start(pos2_ref, rows, y_hbm, far, sems.at[(r + 2) % 3])
        r_id = lax.broadcasted_iota(jnp.int32, (tc, tc), 0)
        c_id = lax.broadcasted_iota(jnp.int32, (tc, tc), 1)
        f_hi = shared_ref[:, :half]
        f_lo = shared_ref[:, half:]
        for k in range(rows // tc):
            y_hi, y_lo = _unpack_bf16_pairs(cur[k * tc:(k + 1) * tc, :])
            g = jnp.sum(jnp.where(r_id == c_id, gate_ref[k:k + 1, :], 0.0), axis=1, keepdims=True)
            f_hi = f_hi + g * y_hi
            f_lo = f_lo + g * y_lo
        f = jnp.concatenate([f_hi, f_lo], axis=1)
        o_ref[...] = _norm_rows(alpha * x1_ref[...] + g2_ref[...] * f, LN_EPS) * lng_ref[...] + lnb_ref[...]

        @pl.when(i == n - 1)
        def _():
            _row_gather_wait(rows, y_hbm, nxt, sems.at[(r + 1) % 3])
            _row_gather_wait(rows, y_hbm, far, sems.at[(r + 2) % 3])

    for r in range(3):
        @pl.when(i % 3 == r)
        def _():
            step(r)


def _combine_call(pos, y_slots, gates, shared, x1, g2, lng, lnb, *, alpha, tc, tiles_per_batch):
    tokens, d = x1.shape
    n_tiles = tokens // tc
    last = n_tiles - 1
    rows = TOP_K * tc
    row = pl.BlockSpec((tc, d), lambda i: (i, 0))
    buf = pltpu.VMEM((rows, y_slots.shape[1]), y_slots.dtype)
    return pl.pallas_call(
        functools.partial(_combine_kernel, alpha=alpha),
        out_shape=jax.ShapeDtypeStruct((tokens, d), F32),
        grid=(n_tiles,),
        in_specs=[
            pl.BlockSpec((TOP_K, tc), lambda i: (0, i), memory_space=pltpu.SMEM),
            pl.BlockSpec((TOP_K, tc), lambda i: (0, jnp.minimum(i + 1, last)), memory_space=pltpu.SMEM),
            pl.BlockSpec((TOP_K, tc), lambda i: (0, jnp.minimum(i + 2, last)), memory_space=pltpu.SMEM),
            pl.BlockSpec(memory_space=pl.ANY),
            pl.BlockSpec((TOP_K, tc), lambda i: (0, i)),
            row, row,
            pl.BlockSpec((None, 1, d), lambda i: (i // tiles_per_batch, 0, 0)),
            _const_spec(lng.shape), _const_spec(lnb.shape),
        ],
        out_specs=row,
        scratch_shapes=[buf, buf, buf, pltpu.SemaphoreType.DMA((3,))],
        compiler_params=_params(("arbitrary",)),
        name="combine",
    )(pos, pos, pos, y_slots, gates, shared, x1, g2, lng, lnb)


def kernel(x, c, ctx, c_ctx, w_mod, b_mod, w_in, s5_lam_re, s5_lam_im, s5_log_dt, s5_b_re, s5_b_im, s5_c_re, s5_c_im, s5_d, w_glu, q_norm_g, k_norm_g, w_branch_ssm, w_branch_attn, w_out, ln1_g, ln1_b, w_router, router_bias, w_exp_gate, w_exp_up, w_exp_down, w_sh_gate, w_sh_up, w_sh_down, ln2_g, ln2_b):
    depth = w_mod.shape[0]
    assert depth == 1, "single-layer block: context outputs are never needed"
    bsz, seq, d = x.shape
    ctx_len = ctx.shape[1]
    kvw = N_KV_HEADS * HEAD_DIM
    n_exp = w_router.shape[2]
    alpha = (2.0 * depth) ** 0.25
    assert seq % GRID_W == 0 and seq % S5_CHUNK == 0 and ctx_len % S5_CHUNK == 0 and bsz % SUBLANES == 0
    lay = 0

    pad = (-(bsz + 1)) % SUBLANES
    c_all = jnp.concatenate([c, c_ctx[None, :], jnp.zeros((pad, d), F32)], axis=0)
    mod = _mod_call(c_all, w_mod[lay], b_mod[lay])
    mod_lat = mod[:bsz].reshape(bsz, N_MOD, 1, d)
    sh1, sc1, g1, sh2, sc2, g2 = (mod_lat[:, k] for k in range(N_MOD))
    mod_ctx = mod[bsz].reshape(N_MOD, 1, 1, d)

    w_in_l = w_in[lay].astype(BF16)
    w_ctx = jnp.concatenate([w_in_l[:, :d], w_in_l[:, 2 * d:2 * d + 2 * kvw]], axis=1)
    head_id = jnp.arange(kvw) // HEAD_DIM
    bd = jnp.where(head_id[:, None] == head_id[None, :], 1.0 / HEAD_DIM, 0.0).astype(BF16)
    tqa, tqb = _rope_tables(seq, q_norm_g[lay], HEAD_DIM ** -0.5)
    tka, tkb = _rope_tables(seq, k_norm_g[lay], 1.0)
    tca = jnp.tile(k_norm_g[lay].astype(F32), N_KV_HEADS)[None, :]
    tm_lat = min(512, seq)
    s_lat, k_lat, v_lat, q_lat, gs_lat, ga_lat = _inproj_call(x, sh1, sc1, w_in_l, bd, tka, tkb, tqa, tqb, tm=tm_lat)
    s_ctx, k_ctx, v_ctx = _inproj_call(ctx, mod_ctx[0], mod_ctx[1], w_ctx, bd, tca, jnp.zeros_like(tca),
                                       tm=min(256, ctx_len))

    w1, wso, lam_tab = _s5_tables(s5_lam_re[lay], s5_lam_im[lay], s5_log_dt[lay], s5_b_re[lay], s5_b_im[lay],
                                  s5_c_re[lay], s5_c_im[lay])
    y_groups = _s5_call(_to_group_major(s_ctx), _to_group_major(s_lat), w1, wso, lam_tab, nb=bsz)
    y_s5 = _to_token_major(y_groups, bsz)

    attn = _attn_call(q_lat, k_ctx, k_lat, v_ctx, v_lat, tq=min(512, seq))

    row = lambda v: v.astype(F32).reshape(1, -1)
    wss = jnp.concatenate([w_sh_gate[lay], w_sh_up[lay]], axis=1).astype(BF16)
    x1, hp, shared, scores_t = _mixout_call(
        y_s5, s_lat, attn, gs_lat, ga_lat, x, g1, sh2, sc2, row(s5_d[lay]),
        w_glu[lay].astype(BF16), w_branch_ssm[lay].astype(BF16), w_branch_attn[lay].astype(BF16),
        w_out[lay].astype(BF16), row(ln1_g[lay]), row(ln1_b[lay]), w_router[lay].T.astype(BF16), wss,
        w_sh_down[lay].astype(BF16), tm=min(256, seq), alpha=alpha)

    tokens = bsz * seq
    bm = MOE_BLOCK
    idx, gates, rank, counts_f = _route_call(scores_t, router_bias[lay], tm=min(ROUTE_TOKENS, seq))
    counts = counts_f.reshape(n_exp).astype(jnp.int32)
    padded = (counts + bm - 1) // bm * bm
    pad_end = jnp.cumsum(padded)
    pad_start = pad_end - padded
    n_blocks = (tokens * TOP_K + n_exp * (bm - 1) + bm - 1) // bm
    block_start = jnp.arange(n_blocks, dtype=jnp.int32) * bm
    block_expert = jnp.minimum(jnp.sum((pad_end[None, :] <= block_start[:, None]).astype(jnp.int32), axis=1), n_exp - 1)
    n_active = (pad_end[-1] // bm).reshape(1)
    n_pad = jnp.sum(padded - counts).reshape(1)
    pos = _slot_call(idx, rank, pad_start, tm=min(512, tokens))
    xs = _dispatch_call(counts, pad_start, padded, n_pad, pos, hp.reshape(tokens, d // 2), n_blocks * bm,
                        td=min(DISPATCH_TOKENS, tokens))
    y_slots = _experts_call(block_expert, n_active, xs, w_exp_gate[lay], w_exp_up[lay], w_exp_down[lay])

    tc = min(COMBINE_TOKENS, seq)
    out = _combine_call(pos, y_slots, gates, shared.reshape(tokens, d), x1.reshape(tokens, d), g2,
                        row(ln2_g[lay]), row(ln2_b[lay]), alpha=alpha, tc=tc, tiles_per_batch=seq // tc)
    return out.reshape(bsz, seq, d)
```

```python
import functools
import math

import jax
import jax.numpy as jnp
from jax import lax
from jax.experimental import pallas as pl
from jax.experimental.pallas import tpu as pltpu

F32 = jnp.float32
BF16 = jnp.bfloat16

GRID_W = 64
HEAD_DIM = 64
N_KV_HEADS = 4
S5_GROUP_CH = 16
S5_MAX_RE = -1e-4
ROPE_THETA = 10000.0
TOP_K = 8
N_EXPERT_GROUPS = 8
TOPK_GROUPS = 4
ROUTED_SCALE = 2.5
LN_EPS = 1e-5
MOD_EPS = 1e-6
RMS_EPS = 1e-6
N_MOD = 6

LANES = 128
SUBLANES = 8
VMEM_LIMIT_BYTES = 56 * 1024 * 1024

S5_CHUNK = 16
S5_COLS = S5_CHUNK * S5_GROUP_CH
MOE_BLOCK = 512
ROUTE_TOKENS = 512
DISPATCH_TOKENS = 128
DISPATCH_BUFFERS = 3
COMBINE_TOKENS = 128


def _params(sem):
    return pltpu.CompilerParams(dimension_semantics=sem, vmem_limit_bytes=VMEM_LIMIT_BYTES)


def _const_spec(shape):
    nd = len(shape)
    return pl.BlockSpec(shape, lambda *_: (0,) * nd, pipeline_mode=pl.Buffered(1))


def _dot(a, b):
    return jnp.dot(a, b, preferred_element_type=F32)


def _norm_rows(x, eps):
    mu = jnp.mean(x, axis=-1, keepdims=True)
    xc = x - mu
    var = jnp.mean(xc * xc, axis=-1, keepdims=True)
    return xc * lax.rsqrt(var + eps)


def _silu(x):
    return x * jax.nn.sigmoid(x)


def _gelu_tanh(x):
    return 0.5 * x * (1.0 + jnp.tanh(math.sqrt(2.0 / math.pi) * (x + 0.044715 * (x * x * x))))


def _pack_bf16_pairs(v):
    n = v.shape[1] // 2
    bits = lax.bitcast_convert_type(v.astype(BF16).astype(F32), jnp.uint32)
    return bits[:, :n] | (bits[:, n:] >> 16)


def _unpack_bf16_pairs(w):
    hi = lax.bitcast_convert_type(w & jnp.uint32(0xFFFF0000), F32)
    lo = lax.bitcast_convert_type(w << 16, F32)
    return hi, lo


def _mod_kernel(c_ref, w_ref, b_ref, o_ref):
    o_ref[...] = _dot(_silu(c_ref[...]), w_ref[...]) + b_ref[...]


def _mod_call(c_all, w_mod, b_mod):
    rows, d = c_all.shape
    n = w_mod.shape[1]
    tn = d
    return pl.pallas_call(
        _mod_kernel,
        out_shape=jax.ShapeDtypeStruct((rows, n), F32),
        grid=(n // tn,),
        in_specs=[pl.BlockSpec((rows, d), lambda j: (0, 0)),
                  pl.BlockSpec((d, tn), lambda j: (0, j)),
                  pl.BlockSpec((1, tn), lambda j: (0, j))],
        out_specs=pl.BlockSpec((rows, tn), lambda j: (0, j)),
        compiler_params=_params(("arbitrary",)),
        name="mod",
    )(c_all, w_mod, b_mod.reshape(1, n))


def _swap16(t):
    width = t.shape[1]
    lane = lax.broadcasted_iota(jnp.int32, t.shape, 1)
    first = (lane & 16) == 0
    return jnp.where(first, pltpu.roll(t, width - 16, 1), pltpu.roll(t, 16, 1))


def _rms_rope(t, bd_ref, ta, tb):
    msq = _dot((t * t).astype(BF16), bd_ref[...])
    return lax.rsqrt(msq + RMS_EPS) * (t * ta + _swap16(t) * tb)


def _store_padded_heads(t, o_ref, ones_lane=False):
    rows = t.shape[0]
    lane = lax.broadcasted_iota(jnp.int32, (rows, LANES), 1)
    lo = lane < HEAD_DIM
    fill_hi = jnp.where(lane == HEAD_DIM, 1.0, 0.0) if ones_lane else 0.0
    fill_lo = jnp.where(lane == 0, 1.0, 0.0) if ones_lane else 0.0
    for j in range(t.shape[1] // LANES):
        slab = t[:, j * LANES:(j + 1) * LANES]
        swapped = pltpu.roll(slab, HEAD_DIM, 1)
        pieces = (jnp.where(lo, slab, fill_hi), jnp.where(lo, fill_lo, swapped),
                  jnp.where(lo, swapped, fill_hi), jnp.where(lo, fill_lo, slab))
        for p, piece in enumerate(pieces):
            c0 = (4 * j + p) * LANES
            o_ref[:, c0:c0 + LANES] = piece.astype(o_ref.dtype)


def _inproj_kernel(x_ref, sh_ref, sc_ref, w_ref, bd_ref, tka_ref, tkb_ref, *rest, d, kvw, has_q):
    if has_q:
        tqa_ref, tqb_ref, s_ref, k_ref, v_ref, q_ref, gs_ref, ga_ref = rest
    else:
        s_ref, k_ref, v_ref = rest
    u = (_norm_rows(x_ref[...], MOD_EPS) * (1.0 + sc_ref[...]) + sh_ref[...]).astype(BF16)
    col = 0
    s_ref[...] = _dot(u, w_ref[:, col:col + d]).astype(s_ref.dtype)
    col += d
    if has_q:
        for c in range(d // kvw):
            q = _dot(u, w_ref[:, col + c * kvw:col + (c + 1) * kvw])
            q_ref[:, c * kvw:(c + 1) * kvw] = _rms_rope(q, bd_ref, tqa_ref[...], tqb_ref[...]).astype(q_ref.dtype)
        col += d
    k = _dot(u, w_ref[:, col:col + kvw])
    _store_padded_heads(_rms_rope(k, bd_ref, tka_ref[...], tkb_ref[...]), k_ref)
    col += kvw
    _store_padded_heads(_dot(u, w_ref[:, col:col + kvw]), v_ref, ones_lane=True)
    col += kvw
    if has_q:
        gs_ref[...] = jax.nn.sigmoid(_dot(u, w_ref[:, col:col + d])).astype(gs_ref.dtype)
        col += d
        ga_ref[...] = jax.nn.sigmoid(_dot(u, w_ref[:, col:col + d])).astype(ga_ref.dtype)


def _inproj_call(x, shift, scale, w, bd, tka, tkb, tqa=None, tqb=None, *, tm):
    bsz, length, d = x.shape
    has_q = tqa is not None
    kvw = N_KV_HEADS * HEAD_DIM
    per_batch = shift.shape[0] > 1
    tab_rows = tka.shape[0]
    tab_blk = tm if tab_rows > 1 else 1
    mod_spec = pl.BlockSpec((None, 1, d), (lambda b, i: (b, 0, 0)) if per_batch else (lambda b, i: (0, 0, 0)))
    tab_spec = pl.BlockSpec((tab_blk, kvw), (lambda b, i: (i, 0)) if tab_rows > 1 else (lambda b, i: (0, 0)))
    row_spec = lambda width: pl.BlockSpec((None, tm, width), lambda b, i: (b, i, 0))
    in_specs = [row_spec(d), mod_spec, mod_spec, _const_spec(w.shape), _const_spec(bd.shape), tab_spec, tab_spec]
    args = [x, shift, scale, w, bd, tka, tkb]
    widths = [d, 4 * kvw, 4 * kvw]
    if has_q:
        in_specs += [tab_spec, tab_spec]
        args += [tqa, tqb]
        widths += [d, d, d]
    return pl.pallas_call(
        functools.partial(_inproj_kernel, d=d, kvw=kvw, has_q=has_q),
        out_shape=[jax.ShapeDtypeStruct((bsz, length, wd), BF16) for wd in widths],
        grid=(bsz, length // tm),
        in_specs=in_specs,
        out_specs=[row_spec(wd) for wd in widths],
        compiler_params=_params(("parallel", "parallel")),
        name="inproj_lat" if has_q else "inproj_ctx",
    )(*args)


def _rope_tables(seq, gain, scale):
    half = HEAD_DIM // 2
    inv_freq = ROPE_THETA ** (-jnp.arange(0, half, 2, dtype=F32) / half)
    t = jnp.arange(seq, dtype=jnp.int32)
    pos = jnp.stack([(t // GRID_W).astype(F32), (t % GRID_W).astype(F32)], axis=1)
    dim = jnp.arange(HEAD_DIM)
    axis = dim // half
    second = ((dim % half) // (half // 2)) == 1
    freq = inv_freq[dim % (half // 2)]
    ang = pos[:, axis] * freq[None, :]
    partner = jnp.where(second, dim - half // 2, dim + half // 2)
    g = gain.astype(F32)
    ta = jnp.cos(ang) * g[None, :] * scale
    tb = jnp.sin(ang) * jnp.where(second, 1.0, -1.0)[None, :] * g[partner][None, :] * scale
    return jnp.tile(ta, (1, N_KV_HEADS)), jnp.tile(tb, (1, N_KV_HEADS))


def _attn_kernel(q_ref, kc_ref, kl_ref, vc_ref, vl_ref, o_ref):
    nt = (((1,), (1,)), ((), ()))
    rows = q_ref.shape[0]
    lane = lax.broadcasted_iota(jnp.int32, (rows, LANES), 1)
    for j in range(q_ref.shape[1] // LANES):
        qs = q_ref[:, j * LANES:(j + 1) * LANES]
        acc = []
        for half in range(2):
            cols = slice(half * LANES, (half + 1) * LANES)
            s_c = lax.dot_general(qs, kc_ref[:, cols], nt, preferred_element_type=F32)
            s_l = lax.dot_general(qs, kl_ref[:, cols], nt, preferred_element_type=F32)
            m = jnp.maximum(jnp.max(s_c, axis=-1, keepdims=True), jnp.max(s_l, axis=-1, keepdims=True))
            e_c = jnp.exp((s_c - m).astype(BF16))
            e_l = jnp.exp((s_l - m).astype(BF16))
            acc.append(_dot(e_c, vc_ref[:, cols]) + _dot(e_l, vl_ref[:, cols]))
        out = jnp.where(lane < HEAD_DIM, acc[0] / acc[0][:, HEAD_DIM:HEAD_DIM + 1], acc[1] / acc[1][:, 0:1])
        o_ref[:, j * LANES:(j + 1) * LANES] = out.astype(o_ref.dtype)


def _attn_call(q, kc, kl, vc, vl, *, tq):
    bsz, seq, d = q.shape
    ctx = kc.shape[1]
    gw = d // N_KV_HEADS
    q_spec = pl.BlockSpec((None, tq, gw), lambda b, h, i: (b, i, h))
    kv_spec = lambda length: pl.BlockSpec((None, length, 2 * LANES), lambda b, h, i: (b, 0, h))
    return pl.pallas_call(
        _attn_kernel,
        out_shape=jax.ShapeDtypeStruct((bsz, seq, d), BF16),
        grid=(bsz, N_KV_HEADS, seq // tq),
        in_specs=[q_spec, kv_spec(ctx), kv_spec(seq), kv_spec(ctx), kv_spec(seq)],
        out_specs=q_spec,
        compiler_params=_params(("parallel", "parallel", "arbitrary")),
        name="attention",
    )(q, kc, kl, vc, vl)


def _s5_tables(lam_re, lam_im, log_dt, b_re, b_im, c_re, c_im):
    n = S5_CHUNK
    lam = lax.complex(jnp.minimum(lam_re.astype(F32), S5_MAX_RE), lam_im.astype(F32))
    lam_dt = lam * jnp.exp(log_dt.astype(F32))[..., None]
    b_bar = ((jnp.exp(lam_dt) - 1.0) / lam)[..., None] * lax.complex(b_re.astype(F32), b_im.astype(F32))
    c_mat = lax.complex(c_re.astype(F32), c_im.astype(F32))
    pw = jnp.exp(lam_dt[None] * jnp.arange(n + 1, dtype=F32)[:, None, None, None])
    kern = jnp.real(jnp.einsum('dgcp,jdgp,dgpe->djgce', c_mat, pw[:n], b_bar))
    s_idx = jnp.arange(n)[:, None]
    t_idx = jnp.arange(n)[None, :]

    def toeplitz(k, lag):
        g = k[jnp.clip(lag, 0, n - 1)]
        g = jnp.where((lag >= 0)[:, :, None, None, None], g, 0.0)
        return g.transpose(2, 0, 4, 1, 3)

    toep = toeplitz(kern[0], t_idx - s_idx) + toeplitz(kern[1], s_idx - t_idx)
    groups = toep.shape[0]
    toep = toep.reshape(groups, S5_COLS, S5_COLS)
    wis_f = jnp.einsum('sgp,gpe->gsep', pw[n - 1 - jnp.arange(n), 0], b_bar[0]).reshape(groups, S5_COLS, -1)
    wis_r = jnp.einsum('sgp,gpe->gsep', pw[jnp.arange(n), 1], b_bar[1]).reshape(groups, S5_COLS, -1)
    w1 = jnp.concatenate([toep, jnp.real(wis_f), jnp.real(wis_r), jnp.imag(wis_f), jnp.imag(wis_r)], axis=-1)
    m_f = jnp.einsum('gcp,tgp->gptc', c_mat[0], pw[1 + jnp.arange(n), 0]).reshape(groups, -1, S5_COLS)
    m_r = jnp.einsum('gcp,tgp->gptc', c_mat[1], pw[n - jnp.arange(n), 1]).reshape(groups, -1, S5_COLS)
    wso = jnp.concatenate([jnp.real(m_f), jnp.real(m_r), -jnp.imag(m_f), -jnp.imag(m_r)], axis=1)
    lam_n = jnp.concatenate([pw[n, 0], pw[n, 1]], axis=-1)
    lam_tab = jnp.stack([jnp.real(lam_n), jnp.imag(lam_n)], axis=1)
    return w1.astype(BF16), wso.astype(BF16), lam_tab


def _s5_kernel(uc_ref, ul_ref, w1_ref, wso_ref, lam_ref, y_ref, a_ref, xf_re, xf_im, xr_re, xr_im,
               *, nb, nc_ctx, nc):
    cols = S5_COLS
    half = lam_ref.shape[1] // 2
    a_ref[:nc_ctx * nb, :] = _dot(uc_ref[...], w1_ref[...])
    a_ref[nc_ctx * nb:, :] = _dot(ul_ref[...], w1_ref[...])
    l_re = lam_ref[0:1, :]
    l_im = lam_ref[1:2, :]
    lo = lax.broadcasted_iota(jnp.int32, (nb, 2 * half), 1) < half

    def step(i, carry):
        x_re, x_im = carry
        i_rev = jnp.where(i < nc_ctx, nc_ctx - 1 - i, nc - 1 + nc_ctx - i)
        rf = pl.ds(pl.multiple_of(i * nb, nb), nb)
        rr = pl.ds(pl.multiple_of(i_rev * nb, nb), nb)
        xf_re[rf, :] = x_re
        xf_im[rf, :] = x_im
        xr_re[rr, :] = x_re
        xr_im[rr, :] = x_im
        s_re = jnp.where(lo, a_ref[rf, cols:cols + 2 * half], a_ref[rr, cols:cols + 2 * half])
        s_im = jnp.where(lo, a_ref[rf, cols + 2 * half:cols + 4 * half], a_ref[rr, cols + 2 * half:cols + 4 * half])
        return l_re * x_re - l_im * x_im + s_re, l_re * x_im + l_im * x_re + s_im

    zero = jnp.zeros((nb, 2 * half), F32)
    lax.fori_loop(0, nc, step, (zero, zero))
    r0 = nc_ctx * nb
    rows = (nc - nc_ctx) * nb
    lo_all = lax.broadcasted_iota(jnp.int32, (rows, 2 * half), 1) < half
    y = a_ref[r0:, 0:cols]
    for k, (f_ref, r_ref) in enumerate(((xf_re, xr_re), (xf_im, xr_im))):
        st = jnp.where(lo_all, f_ref[r0:, :], r_ref[r0:, :])
        hi = st.astype(BF16)
        lo_part = (st - hi.astype(F32)).astype(BF16)
        w = wso_ref[k * 2 * half:(k + 1) * 2 * half, :]
        y = y + _dot(hi, w) + _dot(lo_part, w)
    y_ref[...] = y.astype(y_ref.dtype)


def _s5_call(u_ctx, u_lat, w1, wso, lam_tab, *, nb):
    groups, out_rows, cols = u_lat.shape
    ctx_rows = u_ctx.shape[1]
    rows = ctx_rows + out_rows
    nc = rows // nb
    nc_ctx = ctx_rows // nb
    st = lam_tab.shape[2]
    return pl.pallas_call(
        functools.partial(_s5_kernel, nb=nb, nc_ctx=nc_ctx, nc=nc),
        out_shape=jax.ShapeDtypeStruct((groups, out_rows, cols), BF16),
        grid=(groups,),
        in_specs=[pl.BlockSpec((None, ctx_rows, cols), lambda g: (g, 0, 0)),
                  pl.BlockSpec((None, out_rows, cols), lambda g: (g, 0, 0)),
                  pl.BlockSpec((None,) + w1.shape[1:], lambda g: (g, 0, 0)),
                  pl.BlockSpec((None,) + wso.shape[1:], lambda g: (g, 0, 0)),
                  pl.BlockSpec((None, 2, st), lambda g: (g, 0, 0))],
        out_specs=pl.BlockSpec((None, out_rows, cols), lambda g: (g, 0, 0)),
        scratch_shapes=[pltpu.VMEM((rows, w1.shape[2]), F32)] + [pltpu.VMEM((rows, st), F32)] * 4,
        compiler_params=_params(("parallel",)),
        name="s5",
    )(u_ctx, u_lat, w1, wso, lam_tab)


def _to_group_major(s):
    bsz, length, width = s.shape
    groups = width // S5_GROUP_CH
    nc = length // S5_CHUNK
    t = s.reshape(bsz, nc, S5_CHUNK, groups, S5_GROUP_CH).transpose(3, 1, 0, 2, 4)
    return t.reshape(groups, nc * bsz, S5_COLS)


def _to_token_major(y, bsz):
    groups, rows, _ = y.shape
    nc = rows // bsz
    t = y.reshape(groups, nc, bsz, S5_CHUNK, S5_GROUP_CH).transpose(2, 1, 3, 0, 4)
    return t.reshape(bsz, nc * S5_CHUNK, groups * S5_GROUP_CH)


def _mixout_kernel(y_ref, s_ref, at_ref, gs_ref, ga_ref, x_ref, g1_ref, sh2_ref, sc2_ref, dsk_ref,
                   wglu_ref, wbs_ref, wba_ref, wout_ref, lng_ref, lnb_ref, wrt_ref, wss_ref, wsd_ref,
                   x1_ref, h_ref, shared_ref, scores_ref, *, alpha):
    y = y_ref[...].astype(F32) + s_ref[...].astype(F32) * dsk_ref[...]
    z = _gelu_tanh(y)
    ssm = z * jax.nn.sigmoid(_dot(z.astype(BF16), wglu_ref[...]))
    merged = (gs_ref[...].astype(F32) * _dot(ssm.astype(BF16), wbs_ref[...])
              + ga_ref[...].astype(F32) * _dot(at_ref[...], wba_ref[...]))
    y_mix = _dot(merged.astype(BF16), wout_ref[...])
    x1 = _norm_rows(alpha * x_ref[...] + g1_ref[...] * y_mix, LN_EPS) * lng_ref[...] + lnb_ref[...]
    x1_ref[...] = x1
    h = _norm_rows(x1, MOD_EPS) * (1.0 + sc2_ref[...]) + sh2_ref[...]
    h_ref[...] = _pack_bf16_pairs(h)
    hb = h.astype(BF16)
    scores_ref[...] = jax.nn.sigmoid(
        lax.dot_general(wrt_ref[...], hb, (((1,), (1,)), ((), ())), preferred_element_type=F32))
    ss = _dot(hb, wss_ref[...])
    sh_hidden = wsd_ref.shape[0]
    hid = _silu(ss[:, :sh_hidden]) * ss[:, sh_hidden:]
    shared_ref[...] = _dot(hid.astype(BF16), wsd_ref[...])


def _mixout_call(y, s, attn, gs, ga, x, g1, sh2, sc2, dsk, wglu, wbs, wba, wout, lng, lnb, wrt, wss, wsd,
                 *, tm, alpha):
    bsz, seq, d = x.shape
    n_exp = wrt.shape[0]
    row = lambda width: pl.BlockSpec((None, tm, width), lambda b, i: (b, i, 0))
    mod = pl.BlockSpec((None, 1, d), lambda b, i: (b, 0, 0))
    consts = [dsk, wglu, wbs, wba, wout, lng, lnb, wrt, wss, wsd]
    return pl.pallas_call(
        functools.partial(_mixout_kernel, alpha=alpha),
        out_shape=[jax.ShapeDtypeStruct((bsz, seq, d), F32), jax.ShapeDtypeStruct((bsz, seq, d // 2), jnp.uint32),
                   jax.ShapeDtypeStruct((bsz, seq, d), F32), jax.ShapeDtypeStruct((bsz, n_exp, seq), F32)],
        grid=(bsz, seq // tm),
        in_specs=[row(d)] * 6 + [mod] * 3 + [_const_spec(a.shape) for a in consts],
        out_specs=[row(d), row(d // 2), row(d), pl.BlockSpec((None, n_exp, tm), lambda b, i: (b, 0, i))],
        compiler_params=_params(("parallel", "parallel")),
        name="mixout",
    )(y, s, attn, gs, ga, x, g1, sh2, sc2, *consts)


def _stack_rows(rows):
    n = rows[0].shape[1]
    row_k = lax.broadcasted_iota(jnp.int32, (len(rows), n), 0)
    out = jnp.zeros((len(rows), n), rows[0].dtype)
    for k, r in enumerate(rows):
        out = jnp.where(row_k == k, r, out)
    return out


def _route_kernel(s_ref, bias_ref, utri_ref, idx_ref, gate_ref, rank_ref, counts_ref, base_ref):
    @pl.when((pl.program_id(0) == 0) & (pl.program_id(1) == 0))
    def _():
        base_ref[...] = jnp.zeros_like(base_ref)

    s = s_ref[...]
    n_exp, tm = s.shape
    per_group = n_exp // N_EXPERT_GROUPS
    neg = -jnp.inf
    b = s + bias_ref[...]
    bg = b.reshape(N_EXPERT_GROUPS, per_group, tm)
    m1 = jnp.max(bg, axis=1, keepdims=True)
    is1 = bg == m1
    n1 = jnp.sum(jnp.where(is1, 1.0, 0.0), axis=1, keepdims=True)
    m2 = jnp.max(jnp.where(is1, neg, bg), axis=1, keepdims=True)
    gscore = m1 + jnp.where(n1 >= 2.0, m1, m2)
    gs = [gscore[g] for g in range(N_EXPERT_GROUPS)]
    kept = []
    for g in range(N_EXPERT_GROUPS):
        beaten = jnp.zeros((1, tm), F32)
        for o in range(N_EXPERT_GROUPS):
            if o != g:
                wins = (gs[o] >= gs[g]) if o < g else (gs[o] > gs[g])
                beaten = beaten + jnp.where(wins, 1.0, 0.0)
        kept.append(jnp.where(beaten < float(TOPK_GROUPS), bg[g], neg))
    masked = jnp.concatenate(kept, axis=0)
    row_f = lax.broadcasted_iota(jnp.int32, (n_exp, tm), 0).astype(F32)
    chosen = jnp.zeros((n_exp, tm), F32)
    firsts, vals = [], []
    for _ in range(TOP_K):
        m = jnp.max(masked, axis=0, keepdims=True)
        first = jnp.min(jnp.where(masked == m, row_f, float(n_exp)), axis=0, keepdims=True)
        sel = row_f == first
        firsts.append(first)
        vals.append(jnp.sum(jnp.where(sel, s, 0.0), axis=0, keepdims=True))
        chosen = jnp.where(sel, 1.0, chosen)
        masked = jnp.where(sel, neg, masked)
    before = _dot(chosen.astype(BF16), utri_ref[...]) + base_ref[...]
    ranks = [jnp.sum(jnp.where(row_f == f, before, 0.0), axis=0, keepdims=True) for f in firsts]
    idx_ref[...] = _stack_rows(firsts).astype(jnp.int32)
    rank_ref[...] = _stack_rows(ranks).astype(jnp.int32)
    val = _stack_rows(vals)
    gate_ref[...] = val / jnp.sum(val, axis=0, keepdims=True) * ROUTED_SCALE
    base_ref[...] = base_ref[...] + jnp.sum(chosen, axis=1, keepdims=True)
    counts_ref[...] = base_ref[...]


def _route_call(scores_t, bias, *, tm):
    bsz, n_exp, seq = scores_t.shape
    tiles = seq // tm
    tokens = bsz * seq
    r_id = lax.broadcasted_iota(jnp.int32, (tm, tm), 0)
    c_id = lax.broadcasted_iota(jnp.int32, (tm, tm), 1)
    utri = (r_id < c_id).astype(BF16)
    small = pl.BlockSpec((TOP_K, tm), lambda b, i: (0, b * tiles + i))
    col = pl.BlockSpec((n_exp, 1), lambda b, i: (0, 0))
    return pl.pallas_call(
        _route_kernel,
        out_shape=[jax.ShapeDtypeStruct((TOP_K, tokens), jnp.int32), jax.ShapeDtypeStruct((TOP_K, tokens), F32),
                   jax.ShapeDtypeStruct((TOP_K, tokens), jnp.int32), jax.ShapeDtypeStruct((n_exp, 1), F32)],
        grid=(bsz, tiles),
        in_specs=[pl.BlockSpec((None, n_exp, tm), lambda b, i: (b, 0, i)), col, _const_spec((tm, tm))],
        out_specs=[small, small, small, col],
        scratch_shapes=[pltpu.VMEM((n_exp, 1), F32)],
        compiler_params=_params(("arbitrary", "arbitrary")),
        name="route",
    )(scores_t, bias.astype(F32).reshape(n_exp, 1), utri)


def _slot_kernel(idx_ref, rank_ref, start_ref, pos_ref):
    idx = idx_ref[...]
    tm = idx.shape[1]
    n_exp = start_ref.shape[0]
    row = lax.broadcasted_iota(jnp.int32, (n_exp, tm), 0)
    starts = [jnp.sum(jnp.where(row == idx[k:k + 1, :], start_ref[...], 0.0), axis=0, keepdims=True)
              for k in range(TOP_K)]
    pos_ref[...] = rank_ref[...] + _stack_rows(starts).astype(jnp.int32)


def _slot_call(idx, rank, pad_start, *, tm):
    tokens = idx.shape[1]
    n_exp = pad_start.shape[0]
    small = pl.BlockSpec((TOP_K, tm), lambda i: (0, i))
    return pl.pallas_call(
        _slot_kernel,
        out_shape=jax.ShapeDtypeStruct((TOP_K, tokens), jnp.int32),
        grid=(tokens // tm,),
        in_specs=[small, small, _const_spec((n_exp, 1))],
        out_specs=small,
        compiler_params=_params(("parallel",)),
        name="slots",
    )(idx, rank, pad_start.astype(F32).reshape(n_exp, 1))


def _dispatch_kernel(cnt_ref, start_ref, padded_ref, npad_ref, pos_ref, h_hbm, xs_hbm, hbuf, zrow, lsem, ssem, zsem):
    i = pl.program_id(0)
    n = pl.num_programs(0)
    nbuf, td = hbuf.shape[:2]
    n_exp = cnt_ref.shape[0]
    slot = i % nbuf

    def zero_copy(row):
        return pltpu.make_async_copy(zrow, xs_hbm.at[pl.ds(row, 1), :], zsem)

    def load(step, s):
        return pltpu.make_async_copy(h_hbm.at[pl.ds(step * td, td), :], hbuf.at[s], lsem.at[s])

    def wait_scatters(s):
        rows = td * TOP_K
        pltpu.make_async_copy(h_hbm.at[pl.ds(0, rows), :], xs_hbm.at[pl.ds(0, rows), :], ssem.at[s]).wait()

    @pl.when(i == 0)
    def _():
        zrow[...] = jnp.zeros_like(zrow)

        def per_expert(e, carry):
            def one(r, c2):
                zero_copy(start_ref[e] + r).start()
                return c2
            return lax.fori_loop(cnt_ref[e], padded_ref[e], one, carry)
        lax.fori_loop(0, n_exp, per_expert, 0)
        load(0, 0).start()

    @pl.when(i >= nbuf - 1)
    def _():
        wait_scatters((i + 1) % nbuf)

    @pl.when(i + 1 < n)
    def _():
        load(i + 1, (i + 1) % nbuf).start()

    load(i, slot).wait()

    for t in range(td):
        for k in range(TOP_K):
            pltpu.make_async_copy(hbuf.at[slot, pl.ds(t, 1), :], xs_hbm.at[pl.ds(pos_ref[k, t], 1), :],
                                  ssem.at[slot]).start(priority=k % 2)

    @pl.when(i == n - 1)
    def _():
        for back in range(nbuf - 1):
            @pl.when(i >= back)
            def _():
                wait_scatters((i - back) % nbuf)

        def one(r, c2):
            zero_copy(0).wait()
            return c2
        lax.fori_loop(0, npad_ref[0], one, 0)


def _dispatch_call(counts, pad_start, padded, n_pad, pos, hp, n_slots, *, td):
    tokens, width = hp.shape
    grid_spec = pltpu.PrefetchScalarGridSpec(
        num_scalar_prefetch=4,
        grid=(tokens // td,),
        in_specs=[pl.BlockSpec((TOP_K, td), lambda i, *_: (0, i), memory_space=pltpu.SMEM),
                  pl.BlockSpec(memory_space=pl.ANY)],
        out_specs=pl.BlockSpec(memory_space=pl.ANY),
        scratch_shapes=[pltpu.VMEM((DISPATCH_BUFFERS, td, width), hp.dtype), pltpu.VMEM((1, width), hp.dtype),
                        pltpu.SemaphoreType.DMA((DISPATCH_BUFFERS,)), pltpu.SemaphoreType.DMA((DISPATCH_BUFFERS,)),
                        pltpu.SemaphoreType.DMA],
    )
    return pl.pallas_call(
        _dispatch_kernel,
        out_shape=jax.ShapeDtypeStruct((n_slots, width), hp.dtype),
        grid_spec=grid_spec,
        compiler_params=_params(("arbitrary",)),
        name="dispatch",
    )(counts, pad_start, padded, n_pad, pos, hp)


def _experts_kernel(be_ref, nact_ref, x_ref, wg_ref, wu_ref, wd_ref, y_ref, wg_s, wu_s, wd_s):
    i = pl.program_id(0)
    last = nact_ref[0] - 1
    expert = be_ref[jnp.minimum(i, last)]
    previous = be_ref[jnp.minimum(jnp.maximum(i, 1) - 1, last)]

    @pl.when((i == 0) | (expert != previous))
    def _():
        wg_s[...] = wg_ref[...].astype(BF16)
        wu_s[...] = wu_ref[...].astype(BF16)
        wd_s[...] = wd_ref[...].astype(BF16)

    @pl.when(i <= last)
    def _():
        half = x_ref.shape[1]
        x_hi, x_lo = (v.astype(BF16) for v in _unpack_bf16_pairs(x_ref[...]))
        hg = _dot(x_hi, wg_s[:half, :]) + _dot(x_lo, wg_s[half:, :])
        hu = _dot(x_hi, wu_s[:half, :]) + _dot(x_lo, wu_s[half:, :])
        y_ref[...] = _pack_bf16_pairs(_dot((_silu(hg) * hu).astype(BF16), wd_s[...]))


def _experts_call(block_expert, n_active, xs, wg, wu, wd):
    n_blocks = block_expert.shape[0]
    n_slots, half = xs.shape
    bm = n_slots // n_blocks
    d, hidden = wg.shape[1:]
    blk = lambda i, be, na: (jnp.minimum(i, na[0] - 1), 0)
    wsel = lambda i, be, na: (be[jnp.minimum(i, na[0] - 1)], 0, 0)
    grid_spec = pltpu.PrefetchScalarGridSpec(
        num_scalar_prefetch=2,
        grid=(n_blocks,),
        in_specs=[pl.BlockSpec((bm, half), blk),
                  pl.BlockSpec((None, d, hidden), wsel),
                  pl.BlockSpec((None, d, hidden), wsel),
                  pl.BlockSpec((None, hidden, d), wsel)],
        out_specs=pl.BlockSpec((bm, half), blk),
        scratch_shapes=[pltpu.VMEM((d, hidden), BF16), pltpu.VMEM((d, hidden), BF16), pltpu.VMEM((hidden, d), BF16)],
    )
    return pl.pallas_call(
        _experts_kernel,
        out_shape=jax.ShapeDtypeStruct((n_slots, half), xs.dtype),
        grid_spec=grid_spec,
        compiler_params=_params(("arbitrary",)),
        name="experts",
    )(block_expert, n_active, xs, wg, wu, wd)


def _row_gather_start(idx_ref, n_rows, src_hbm, dst_ref, sem):
    tc = idx_ref.shape[1]
    for r in range(n_rows):
        pltpu.make_async_copy(src_hbm.at[pl.ds(idx_ref[r // tc, r % tc], 1), :], dst_ref.at[pl.ds(r, 1), :],
                              sem).start(priority=r % 2)


def _row_gather_wait(n_rows, src_hbm, dst_ref, sem):
    pltpu.make_async_copy(src_hbm.at[pl.ds(0, n_rows), :], dst_ref, sem).wait()


def _combine_kernel(pos_ref, pos1_ref, pos2_ref, y_hbm, gate_ref, shared_ref, x1_ref, g2_ref, lng_ref, lnb_ref,
                    o_ref, buf_a, buf_b, buf_c, sems, *, alpha):
    i = pl.program_id(0)
    n = pl.num_programs(0)
    rows, half = buf_a.shape
    tc = o_ref.shape[0]
    bufs = (buf_a, buf_b, buf_c)

    @pl.when(i == 0)
    def _():
        _row_gather_start(pos_ref, rows, y_hbm, buf_a, sems.at[0])
        _row_gather_start(pos1_ref, rows, y_hbm, buf_b, sems.at[1])

    def step(r):
        cur, nxt, far = bufs[r], bufs[(r + 1) % 3], bufs[(r + 2) % 3]
        _row_gather_wait(rows, y_hbm, cur, sems.at[r])
        _row_gather_start(pos2_ref, rows, y_hbm, far, sems.at[(r + 2) % 3])
        r_id = lax.broadcasted_iota(jnp.int32, (tc, tc), 0)
        c_id = lax.broadcasted_iota(jnp.int32, (tc, tc), 1)
        f_hi = shared_ref[:, :half]
        f_lo = shared_ref[:, half:]
        for k in range(rows // tc):
            y_hi, y_lo = _unpack_bf16_pairs(cur[k * tc:(k + 1) * tc, :])
            g = jnp.sum(jnp.where(r_id == c_id, gate_ref[k:k + 1, :], 0.0), axis=1, keepdims=True)
            f_hi = f_hi + g * y_hi
            f_lo = f_lo + g * y_lo
        f = jnp.concatenate([f_hi, f_lo], axis=1)
        o_ref[...] = _norm_rows(alpha * x1_ref[...] + g2_ref[...] * f, LN_EPS) * lng_ref[...] + lnb_ref[...]

        @pl.when(i == n - 1)
        def _():
            _row_gather_wait(rows, y_hbm, nxt, sems.at[(r + 1) % 3])
            _row_gather_wait(rows, y_hbm, far, sems.at[(r + 2) % 3])

    for r in range(3):
        @pl.when(i % 3 == r)
        def _():
            step(r)


def _combine_call(pos, y_slots, gates, shared, x1, g2, lng, lnb, *, alpha, tc, tiles_per_batch):
    tokens, d = x1.shape
    n_tiles = tokens // tc
    last = n_tiles - 1
    rows = TOP_K * tc
    row = pl.BlockSpec((tc, d), lambda i: (i, 0))
    buf = pltpu.VMEM((rows, y_slots.shape[1]), y_slots.dtype)
    return pl.pallas_call(
        functools.partial(_combine_kernel, alpha=alpha),
        out_shape=jax.ShapeDtypeStruct((tokens, d), F32),
        grid=(n_tiles,),
        in_specs=[
            pl.BlockSpec((TOP_K, tc), lambda i: (0, i), memory_space=pltpu.SMEM),
            pl.BlockSpec((TOP_K, tc), lambda i: (0, jnp.minimum(i + 1, last)), memory_space=pltpu.SMEM),
            pl.BlockSpec((TOP_K, tc), lambda i: (0, jnp.minimum(i + 2, last)), memory_space=pltpu.SMEM),
            pl.BlockSpec(memory_space=pl.ANY),
            pl.BlockSpec((TOP_K, tc), lambda i: (0, i)),
            row, row,
            pl.BlockSpec((None, 1, d), lambda i: (i // tiles_per_batch, 0, 0)),
            _const_spec(lng.shape), _const_spec(lnb.shape),
        ],
        out_specs=row,
        scratch_shapes=[buf, buf, buf, pltpu.SemaphoreType.DMA((3,))],
        compiler_params=_params(("arbitrary",)),
        name="combine",
    )(pos, pos, pos, y_slots, gates, shared, x1, g2, lng, lnb)


def kernel(x, c, ctx, c_ctx, w_mod, b_mod, w_in, s5_lam_re, s5_lam_im, s5_log_dt, s5_b_re, s5_b_im, s5_c_re, s5_c_im, s5_d, w_glu, q_norm_g, k_norm_g, w_branch_ssm, w_branch_attn, w_out, ln1_g, ln1_b, w_router, router_bias, w_exp_gate, w_exp_up, w_exp_down, w_sh_gate, w_sh_up, w_sh_down, ln2_g, ln2_b):
    depth = w_mod.shape[0]
    assert depth == 1, "single-layer block: context outputs are never needed"
    bsz, seq, d = x.shape
    ctx_len = ctx.shape[1]
    kvw = N_KV_HEADS * HEAD_DIM
    n_exp = w_router.shape[2]
    alpha = (2.0 * depth) ** 0.25
    assert seq % GRID_W == 0 and seq % S5_CHUNK == 0 and ctx_len % S5_CHUNK == 0 and bsz % SUBLANES == 0
    lay = 0

    pad = (-(bsz + 1)) % SUBLANES
    c_all = jnp.concatenate([c, c_ctx[None, :], jnp.zeros((pad, d), F32)], axis=0)
    mod = _mod_call(c_all, w_mod[lay], b_mod[lay])
    mod_lat = mod[:bsz].reshape(bsz, N_MOD, 1, d)
    sh1, sc1, g1, sh2, sc2, g2 = (mod_lat[:, k] for k in range(N_MOD))
    mod_ctx = mod[bsz].reshape(N_MOD, 1, 1, d)

    w_in_l = w_in[lay].astype(BF16)
    w_ctx = jnp.concatenate([w_in_l[:, :d], w_in_l[:, 2 * d:2 * d + 2 * kvw]], axis=1)
    head_id = jnp.arange(kvw) // HEAD_DIM
    bd = jnp.where(head_id[:, None] == head_id[None, :], 1.0 / HEAD_DIM, 0.0).astype(BF16)
    tqa, tqb = _rope_tables(seq, q_norm_g[lay], HEAD_DIM ** -0.5)
    tka, tkb = _rope_tables(seq, k_norm_g[lay], 1.0)
    tca = jnp.tile(k_norm_g[lay].astype(F32), N_KV_HEADS)[None, :]
    tm_lat = min(512, seq)
    s_lat, k_lat, v_lat, q_lat, gs_lat, ga_lat = _inproj_call(x, sh1, sc1, w_in_l, bd, tka, tkb, tqa, tqb, tm=tm_lat)
    s_ctx, k_ctx, v_ctx = _inproj_call(ctx, mod_ctx[0], mod_ctx[1], w_ctx, bd, tca, jnp.zeros_like(tca),
                                       tm=min(256, ctx_len))

    w1, wso, lam_tab = _s5_tables(s5_lam_re[lay], s5_lam_im[lay], s5_log_dt[lay], s5_b_re[lay], s5_b_im[lay],
                                  s5_c_re[lay], s5_c_im[lay])
    y_groups = _s5_call(_to_group_major(s_ctx), _to_group_major(s_lat), w1, wso, lam_tab, nb=bsz)
    y_s5 = _to_token_major(y_groups, bsz)

    attn = _attn_call(q_lat, k_ctx, k_lat, v_ctx, v_lat, tq=min(1024, seq))

    row = lambda v: v.astype(F32).reshape(1, -1)
    wss = jnp.concatenate([w_sh_gate[lay], w_sh_up[lay]], axis=1).astype(BF16)
    x1, hp, shared, scores_t = _mixout_call(
        y_s5, s_lat, attn, gs_lat, ga_lat, x, g1, sh2, sc2, row(s5_d[lay]),
        w_glu[lay].astype(BF16), w_branch_ssm[lay].astype(BF16), w_branch_attn[lay].astype(BF16),
        w_out[lay].astype(BF16), row(ln1_g[lay]), row(ln1_b[lay]), w_router[lay].T.astype(BF16), wss,
        w_sh_down[lay].astype(BF16), tm=min(512, seq), alpha=alpha)

    tokens = bsz * seq
    bm = MOE_BLOCK
    idx, gates, rank, counts_f = _route_call(scores_t, router_bias[lay], tm=min(ROUTE_TOKENS, seq))
    counts = counts_f.reshape(n_exp).astype(jnp.int32)
    padded = (counts + bm - 1) // bm * bm
    pad_end = jnp.cumsum(padded)
    pad_start = pad_end - padded
    n_blocks = (tokens * TOP_K + n_exp * (bm - 1) + bm - 1) // bm
    block_start = jnp.arange(n_blocks, dtype=jnp.int32) * bm
    block_expert = jnp.minimum(jnp.sum((pad_end[None, :] <= block_start[:, None]).astype(jnp.int32), axis=1), n_exp - 1)
    n_active = (pad_end[-1] // bm).reshape(1)
    n_pad = jnp.sum(padded - counts).reshape(1)
    pos = _slot_call(idx, rank, pad_start, tm=min(512, tokens))
    xs = _dispatch_call(counts, pad_start, padded, n_pad, pos, hp.reshape(tokens, d // 2), n_blocks * bm,
                        td=min(DISPATCH_TOKENS, tokens))
    y_slots = _experts_call(block_expert, n_active, xs, w_exp_gate[lay], w_exp_up[lay], w_exp_down[lay])

    tc = min(COMBINE_TOKENS, seq)
    out = _combine_call(pos, y_slots, gates, shared.reshape(tokens, d), x1.reshape(tokens, d), g2,
                        row(ln2_g[lay]), row(ln2_b[lay]), alpha=alpha, tc=tc, tiles_per_batch=seq // tc)
    return out.reshape(bsz, seq, d)
```

```python
import functools
import math

import jax
import jax.numpy as jnp
from jax import lax
from jax.experimental import pallas as pl
from jax.experimental.pallas import tpu as pltpu

F32 = jnp.float32
BF16 = jnp.bfloat16

GRID_W = 64
HEAD_DIM = 64
N_KV_HEADS = 4
S5_GROUP_CH = 16
S5_MAX_RE = -1e-4
ROPE_THETA = 10000.0
TOP_K = 8
N_EXPERT_GROUPS = 8
TOPK_GROUPS = 4
ROUTED_SCALE = 2.5
LN_EPS = 1e-5
MOD_EPS = 1e-6
RMS_EPS = 1e-6
N_MOD = 6

LANES = 128
SUBLANES = 8
VMEM_LIMIT_BYTES = 56 * 1024 * 1024

S5_CHUNK = 16
S5_COLS = S5_CHUNK * S5_GROUP_CH
MOE_BLOCK = 512
ROUTE_TOKENS = 512
DISPATCH_TOKENS = 128
DISPATCH_BUFFERS = 3
COMBINE_TOKENS = 128


def _params(sem):
    return pltpu.CompilerParams(dimension_semantics=sem, vmem_limit_bytes=VMEM_LIMIT_BYTES)


def _const_spec(shape):
    nd = len(shape)
    return pl.BlockSpec(shape, lambda *_: (0,) * nd, pipeline_mode=pl.Buffered(1))


def _dot(a, b):
    return jnp.dot(a, b, preferred_element_type=F32)


def _norm_rows(x, eps):
    mu = jnp.mean(x, axis=-1, keepdims=True)
    xc = x - mu
    var = jnp.mean(xc * xc, axis=-1, keepdims=True)
    return xc * lax.rsqrt(var + eps)


def _silu(x):
    return x * jax.nn.sigmoid(x)


def _gelu_tanh(x):
    return 0.5 * x * (1.0 + jnp.tanh(math.sqrt(2.0 / math.pi) * (x + 0.044715 * (x * x * x))))


def _pack_bf16_pairs(v):
    n = v.shape[1] // 2
    bits = lax.bitcast_convert_type(v.astype(BF16).astype(F32), jnp.uint32)
    return bits[:, :n] | (bits[:, n:] >> 16)


def _store_row_pieces(ref, v):
    rows, width = v.shape
    n = width // LANES
    for j in range(n):
        ref[pl.ds(j, rows, stride=n), :] = v[:, j * LANES:(j + 1) * LANES]


def _load_row_pieces(ref, first_row, rows, n):
    return jnp.concatenate([ref[pl.ds(first_row * n + j, rows, stride=n), :] for j in range(n)], axis=1)


def _unpack_bf16_pairs(w):
    hi = lax.bitcast_convert_type(w & jnp.uint32(0xFFFF0000), F32)
    lo = lax.bitcast_convert_type(w << 16, F32)
    return hi, lo


def _mod_kernel(c_ref, w_ref, b_ref, o_ref):
    o_ref[...] = _dot(_silu(c_ref[...]), w_ref[...]) + b_ref[...]


def _mod_call(c_all, w_mod, b_mod):
    rows, d = c_all.shape
    n = w_mod.shape[1]
    tn = d
    return pl.pallas_call(
        _mod_kernel,
        out_shape=jax.ShapeDtypeStruct((rows, n), F32),
        grid=(n // tn,),
        in_specs=[pl.BlockSpec((rows, d), lambda j: (0, 0)),
                  pl.BlockSpec((d, tn), lambda j: (0, j)),
                  pl.BlockSpec((1, tn), lambda j: (0, j))],
        out_specs=pl.BlockSpec((rows, tn), lambda j: (0, j)),
        compiler_params=_params(("arbitrary",)),
        name="mod",
    )(c_all, w_mod, b_mod.reshape(1, n))


def _swap16(t):
    width = t.shape[1]
    lane = lax.broadcasted_iota(jnp.int32, t.shape, 1)
    first = (lane & 16) == 0
    return jnp.where(first, pltpu.roll(t, width - 16, 1), pltpu.roll(t, 16, 1))


def _rms_rope(t, bd_ref, ta, tb):
    msq = _dot((t * t).astype(BF16), bd_ref[...])
    return lax.rsqrt(msq + RMS_EPS) * (t * ta + _swap16(t) * tb)


def _store_padded_heads(t, o_ref, ones_lane=False):
    rows = t.shape[0]
    lane = lax.broadcasted_iota(jnp.int32, (rows, LANES), 1)
    lo = lane < HEAD_DIM
    fill_hi = jnp.where(lane == HEAD_DIM, 1.0, 0.0) if ones_lane else 0.0
    fill_lo = jnp.where(lane == 0, 1.0, 0.0) if ones_lane else 0.0
    for j in range(t.shape[1] // LANES):
        slab = t[:, j * LANES:(j + 1) * LANES]
        swapped = pltpu.roll(slab, HEAD_DIM, 1)
        pieces = (jnp.where(lo, slab, fill_hi), jnp.where(lo, fill_lo, swapped),
                  jnp.where(lo, swapped, fill_hi), jnp.where(lo, fill_lo, slab))
        for p, piece in enumerate(pieces):
            c0 = (4 * j + p) * LANES
            o_ref[:, c0:c0 + LANES] = piece.astype(o_ref.dtype)


def _inproj_kernel(x_ref, sh_ref, sc_ref, w_ref, bd_ref, tka_ref, tkb_ref, *rest, d, kvw, has_q):
    if has_q:
        tqa_ref, tqb_ref, s_ref, k_ref, v_ref, q_ref, gs_ref, ga_ref = rest
    else:
        s_ref, k_ref, v_ref = rest
    u = (_norm_rows(x_ref[...], MOD_EPS) * (1.0 + sc_ref[...]) + sh_ref[...]).astype(BF16)
    col = 0
    s_ref[...] = _dot(u, w_ref[:, col:col + d]).astype(s_ref.dtype)
    col += d
    if has_q:
        for c in range(d // kvw):
            q = _dot(u, w_ref[:, col + c * kvw:col + (c + 1) * kvw])
            q_ref[:, c * kvw:(c + 1) * kvw] = _rms_rope(q, bd_ref, tqa_ref[...], tqb_ref[...]).astype(q_ref.dtype)
        col += d
    k = _dot(u, w_ref[:, col:col + kvw])
    _store_padded_heads(_rms_rope(k, bd_ref, tka_ref[...], tkb_ref[...]), k_ref)
    col += kvw
    _store_padded_heads(_dot(u, w_ref[:, col:col + kvw]), v_ref, ones_lane=True)
    col += kvw
    if has_q:
        gs_ref[...] = jax.nn.sigmoid(_dot(u, w_ref[:, col:col + d])).astype(gs_ref.dtype)
        col += d
        ga_ref[...] = jax.nn.sigmoid(_dot(u, w_ref[:, col:col + d])).astype(ga_ref.dtype)


def _inproj_call(x, shift, scale, w, bd, tka, tkb, tqa=None, tqb=None, *, tm):
    bsz, length, d = x.shape
    has_q = tqa is not None
    kvw = N_KV_HEADS * HEAD_DIM
    per_batch = shift.shape[0] > 1
    tab_rows = tka.shape[0]
    tab_blk = tm if tab_rows > 1 else 1
    mod_spec = pl.BlockSpec((None, 1, d), (lambda b, i: (b, 0, 0)) if per_batch else (lambda b, i: (0, 0, 0)))
    tab_spec = pl.BlockSpec((tab_blk, kvw), (lambda b, i: (i, 0)) if tab_rows > 1 else (lambda b, i: (0, 0)))
    row_spec = lambda width: pl.BlockSpec((None, tm, width), lambda b, i: (b, i, 0))
    in_specs = [row_spec(d), mod_spec, mod_spec, _const_spec(w.shape), _const_spec(bd.shape), tab_spec, tab_spec]
    args = [x, shift, scale, w, bd, tka, tkb]
    widths = [d, 4 * kvw, 4 * kvw]
    if has_q:
        in_specs += [tab_spec, tab_spec]
        args += [tqa, tqb]
        widths += [d, d, d]
    return pl.pallas_call(
        functools.partial(_inproj_kernel, d=d, kvw=kvw, has_q=has_q),
        out_shape=[jax.ShapeDtypeStruct((bsz, length, wd), BF16) for wd in widths],
        grid=(bsz, length // tm),
        in_specs=in_specs,
        out_specs=[row_spec(wd) for wd in widths],
        compiler_params=_params(("parallel", "parallel")),
        name="inproj_lat" if has_q else "inproj_ctx",
    )(*args)


def _rope_tables(seq, gain, scale):
    half = HEAD_DIM // 2
    inv_freq = ROPE_THETA ** (-jnp.arange(0, half, 2, dtype=F32) / half)
    t = jnp.arange(seq, dtype=jnp.int32)
    pos = jnp.stack([(t // GRID_W).astype(F32), (t % GRID_W).astype(F32)], axis=1)
    dim = jnp.arange(HEAD_DIM)
    axis = dim // half
    second = ((dim % half) // (half // 2)) == 1
    freq = inv_freq[dim % (half // 2)]
    ang = pos[:, axis] * freq[None, :]
    partner = jnp.where(second, dim - half // 2, dim + half // 2)
    g = gain.astype(F32)
    ta = jnp.cos(ang) * g[None, :] * scale
    tb = jnp.sin(ang) * jnp.where(second, 1.0, -1.0)[None, :] * g[partner][None, :] * scale
    return jnp.tile(ta, (1, N_KV_HEADS)), jnp.tile(tb, (1, N_KV_HEADS))


def _attn_kernel(q_ref, kc_ref, kl_ref, vc_ref, vl_ref, o_ref):
    nt = (((1,), (1,)), ((), ()))
    rows = q_ref.shape[0]
    lane = lax.broadcasted_iota(jnp.int32, (rows, LANES), 1)
    for j in range(q_ref.shape[1] // LANES):
        qs = q_ref[:, j * LANES:(j + 1) * LANES]
        acc = []
        for half in range(2):
            cols = slice(half * LANES, (half + 1) * LANES)
            s_c = lax.dot_general(qs, kc_ref[:, cols], nt, preferred_element_type=F32)
            s_l = lax.dot_general(qs, kl_ref[:, cols], nt, preferred_element_type=F32)
            m = jnp.maximum(jnp.max(s_c, axis=-1, keepdims=True), jnp.max(s_l, axis=-1, keepdims=True))
            e_c = jnp.exp((s_c - m).astype(BF16))
            e_l = jnp.exp((s_l - m).astype(BF16))
            acc.append(_dot(e_c, vc_ref[:, cols]) + _dot(e_l, vl_ref[:, cols]))
        out = jnp.where(lane < HEAD_DIM, acc[0] / acc[0][:, HEAD_DIM:HEAD_DIM + 1], acc[1] / acc[1][:, 0:1])
        o_ref[:, j * LANES:(j + 1) * LANES] = out.astype(o_ref.dtype)


def _attn_call(q, kc, kl, vc, vl, *, tq):
    bsz, seq, d = q.shape
    ctx = kc.shape[1]
    gw = d // N_KV_HEADS
    q_spec = pl.BlockSpec((None, tq, gw), lambda b, h, i: (b, i, h))
    kv_spec = lambda length: pl.BlockSpec((None, length, 2 * LANES), lambda b, h, i: (b, 0, h))
    return pl.pallas_call(
        _attn_kernel,
        out_shape=jax.ShapeDtypeStruct((bsz, seq, d), BF16),
        grid=(bsz, N_KV_HEADS, seq // tq),
        in_specs=[q_spec, kv_spec(ctx), kv_spec(seq), kv_spec(ctx), kv_spec(seq)],
        out_specs=q_spec,
        compiler_params=_params(("parallel", "parallel", "arbitrary")),
        name="attention",
    )(q, kc, kl, vc, vl)


def _s5_tables(lam_re, lam_im, log_dt, b_re, b_im, c_re, c_im):
    n = S5_CHUNK
    lam = lax.complex(jnp.minimum(lam_re.astype(F32), S5_MAX_RE), lam_im.astype(F32))
    lam_dt = lam * jnp.exp(log_dt.astype(F32))[..., None]
    b_bar = ((jnp.exp(lam_dt) - 1.0) / lam)[..., None] * lax.complex(b_re.astype(F32), b_im.astype(F32))
    c_mat = lax.complex(c_re.astype(F32), c_im.astype(F32))
    pw = jnp.exp(lam_dt[None] * jnp.arange(n + 1, dtype=F32)[:, None, None, None])
    kern = jnp.real(jnp.einsum('dgcp,jdgp,dgpe->djgce', c_mat, pw[:n], b_bar))
    s_idx = jnp.arange(n)[:, None]
    t_idx = jnp.arange(n)[None, :]

    def toeplitz(k, lag):
        g = k[jnp.clip(lag, 0, n - 1)]
        g = jnp.where((lag >= 0)[:, :, None, None, None], g, 0.0)
        return g.transpose(2, 0, 4, 1, 3)

    toep = toeplitz(kern[0], t_idx - s_idx) + toeplitz(kern[1], s_idx - t_idx)
    groups = toep.shape[0]
    toep = toep.reshape(groups, S5_COLS, S5_COLS)
    wis_f = jnp.einsum('sgp,gpe->gsep', pw[n - 1 - jnp.arange(n), 0], b_bar[0]).reshape(groups, S5_COLS, -1)
    wis_r = jnp.einsum('sgp,gpe->gsep', pw[jnp.arange(n), 1], b_bar[1]).reshape(groups, S5_COLS, -1)
    w1 = jnp.concatenate([toep, jnp.real(wis_f), jnp.real(wis_r), jnp.imag(wis_f), jnp.imag(wis_r)], axis=-1)
    m_f = jnp.einsum('gcp,tgp->gptc', c_mat[0], pw[1 + jnp.arange(n), 0]).reshape(groups, -1, S5_COLS)
    m_r = jnp.einsum('gcp,tgp->gptc', c_mat[1], pw[n - jnp.arange(n), 1]).reshape(groups, -1, S5_COLS)
    wso = jnp.concatenate([jnp.real(m_f), jnp.real(m_r), -jnp.imag(m_f), -jnp.imag(m_r)], axis=1)
    lam_n = jnp.concatenate([pw[n, 0], pw[n, 1]], axis=-1)
    lam_tab = jnp.stack([jnp.real(lam_n), jnp.imag(lam_n)], axis=1)
    return w1.astype(BF16), wso.astype(BF16), lam_tab


def _s5_kernel(uc_ref, ul_ref, w1_ref, wso_ref, lam_ref, y_ref, a_ref, xf_re, xf_im, xr_re, xr_im,
               *, nb, nc_ctx, nc):
    cols = S5_COLS
    half = lam_ref.shape[1] // 2
    a_ref[:nc_ctx * nb, :] = _dot(uc_ref[...], w1_ref[...])
    a_ref[nc_ctx * nb:, :] = _dot(ul_ref[...], w1_ref[...])
    l_re = lam_ref[0:1, :]
    l_im = lam_ref[1:2, :]
    lo = lax.broadcasted_iota(jnp.int32, (nb, 2 * half), 1) < half

    def step(i, carry):
        x_re, x_im = carry
        i_rev = jnp.where(i < nc_ctx, nc_ctx - 1 - i, nc - 1 + nc_ctx - i)
        rf = pl.ds(pl.multiple_of(i * nb, nb), nb)
        rr = pl.ds(pl.multiple_of(i_rev * nb, nb), nb)
        xf_re[rf, :] = x_re
        xf_im[rf, :] = x_im
        xr_re[rr, :] = x_re
        xr_im[rr, :] = x_im
        s_re = jnp.where(lo, a_ref[rf, cols:cols + 2 * half], a_ref[rr, cols:cols + 2 * half])
        s_im = jnp.where(lo, a_ref[rf, cols + 2 * half:cols + 4 * half], a_ref[rr, cols + 2 * half:cols + 4 * half])
        return l_re * x_re - l_im * x_im + s_re, l_re * x_im + l_im * x_re + s_im

    zero = jnp.zeros((nb, 2 * half), F32)
    lax.fori_loop(0, nc, step, (zero, zero))
    r0 = nc_ctx * nb
    rows = (nc - nc_ctx) * nb
    lo_all = lax.broadcasted_iota(jnp.int32, (rows, 2 * half), 1) < half
    y = a_ref[r0:, 0:cols]
    for k, (f_ref, r_ref) in enumerate(((xf_re, xr_re), (xf_im, xr_im))):
        st = jnp.where(lo_all, f_ref[r0:, :], r_ref[r0:, :])
        hi = st.astype(BF16)
        lo_part = (st - hi.astype(F32)).astype(BF16)
        w = wso_ref[k * 2 * half:(k + 1) * 2 * half, :]
        y = y + _dot(hi, w) + _dot(lo_part, w)
    y_ref[...] = y.astype(y_ref.dtype)


def _s5_call(u_ctx, u_lat, w1, wso, lam_tab, *, nb):
    groups, out_rows, cols = u_lat.shape
    ctx_rows = u_ctx.shape[1]
    rows = ctx_rows + out_rows
    nc = rows // nb
    nc_ctx = ctx_rows // nb
    st = lam_tab.shape[2]
    return pl.pallas_call(
        functools.partial(_s5_kernel, nb=nb, nc_ctx=nc_ctx, nc=nc),
        out_shape=jax.ShapeDtypeStruct((groups, out_rows, cols), BF16),
        grid=(groups,),
        in_specs=[pl.BlockSpec((None, ctx_rows, cols), lambda g: (g, 0, 0)),
                  pl.BlockSpec((None, out_rows, cols), lambda g: (g, 0, 0)),
                  pl.BlockSpec((None,) + w1.shape[1:], lambda g: (g, 0, 0)),
                  pl.BlockSpec((None,) + wso.shape[1:], lambda g: (g, 0, 0)),
                  pl.BlockSpec((None, 2, st), lambda g: (g, 0, 0))],
        out_specs=pl.BlockSpec((None, out_rows, cols), lambda g: (g, 0, 0)),
        scratch_shapes=[pltpu.VMEM((rows, w1.shape[2]), F32)] + [pltpu.VMEM((rows, st), F32)] * 4,
        compiler_params=_params(("parallel",)),
        name="s5",
    )(u_ctx, u_lat, w1, wso, lam_tab)


def _to_group_major(s):
    bsz, length, width = s.shape
    groups = width // S5_GROUP_CH
    nc = length // S5_CHUNK
    t = s.reshape(bsz, nc, S5_CHUNK, groups, S5_GROUP_CH).transpose(3, 1, 0, 2, 4)
    return t.reshape(groups, nc * bsz, S5_COLS)


def _to_token_major(y, bsz):
    groups, rows, _ = y.shape
    nc = rows // bsz
    t = y.reshape(groups, nc, bsz, S5_CHUNK, S5_GROUP_CH).transpose(2, 1, 3, 0, 4)
    return t.reshape(bsz, nc * S5_CHUNK, groups * S5_GROUP_CH)


def _mixout_kernel(y_ref, s_ref, at_ref, gs_ref, ga_ref, x_ref, g1_ref, sh2_ref, sc2_ref, dsk_ref,
                   wglu_ref, wbs_ref, wba_ref, wout_ref, lng_ref, lnb_ref, wrt_ref, wss_ref, wsd_ref,
                   x1_ref, h_ref, shared_ref, scores_ref, *, alpha):
    y = y_ref[...].astype(F32) + s_ref[...].astype(F32) * dsk_ref[...]
    z = _gelu_tanh(y)
    ssm = z * jax.nn.sigmoid(_dot(z.astype(BF16), wglu_ref[...]))
    merged = (gs_ref[...].astype(F32) * _dot(ssm.astype(BF16), wbs_ref[...])
              + ga_ref[...].astype(F32) * _dot(at_ref[...], wba_ref[...]))
    y_mix = _dot(merged.astype(BF16), wout_ref[...])
    x1 = _norm_rows(alpha * x_ref[...] + g1_ref[...] * y_mix, LN_EPS) * lng_ref[...] + lnb_ref[...]
    x1_ref[...] = x1
    h = _norm_rows(x1, MOD_EPS) * (1.0 + sc2_ref[...]) + sh2_ref[...]
    _store_row_pieces(h_ref, _pack_bf16_pairs(h))
    hb = h.astype(BF16)
    scores_ref[...] = jax.nn.sigmoid(
        lax.dot_general(wrt_ref[...], hb, (((1,), (1,)), ((), ())), preferred_element_type=F32))
    ss = _dot(hb, wss_ref[...])
    sh_hidden = wsd_ref.shape[0]
    hid = _silu(ss[:, :sh_hidden]) * ss[:, sh_hidden:]
    shared_ref[...] = _dot(hid.astype(BF16), wsd_ref[...])


def _mixout_call(y, s, attn, gs, ga, x, g1, sh2, sc2, dsk, wglu, wbs, wba, wout, lng, lnb, wrt, wss, wsd,
                 *, tm, alpha):
    bsz, seq, d = x.shape
    n_exp = wrt.shape[0]
    pieces = d // 2 // LANES
    row = lambda width: pl.BlockSpec((None, tm, width), lambda b, i: (b, i, 0))
    mod = pl.BlockSpec((None, 1, d), lambda b, i: (b, 0, 0))
    consts = [dsk, wglu, wbs, wba, wout, lng, lnb, wrt, wss, wsd]
    return pl.pallas_call(
        functools.partial(_mixout_kernel, alpha=alpha),
        out_shape=[jax.ShapeDtypeStruct((bsz, seq, d), F32),
                   jax.ShapeDtypeStruct((bsz, seq * pieces, LANES), jnp.uint32),
                   jax.ShapeDtypeStruct((bsz, seq, d), F32), jax.ShapeDtypeStruct((bsz, n_exp, seq), F32)],
        grid=(bsz, seq // tm),
        in_specs=[row(d)] * 6 + [mod] * 3 + [_const_spec(a.shape) for a in consts],
        out_specs=[row(d), pl.BlockSpec((None, tm * pieces, LANES), lambda b, i: (b, i, 0)),
                   row(d), pl.BlockSpec((None, n_exp, tm), lambda b, i: (b, 0, i))],
        compiler_params=_params(("parallel", "parallel")),
        name="mixout",
    )(y, s, attn, gs, ga, x, g1, sh2, sc2, *consts)


def _stack_rows(rows):
    n = rows[0].shape[1]
    row_k = lax.broadcasted_iota(jnp.int32, (len(rows), n), 0)
    out = jnp.zeros((len(rows), n), rows[0].dtype)
    for k, r in enumerate(rows):
        out = jnp.where(row_k == k, r, out)
    return out


def _route_kernel(s_ref, bias_ref, utri_ref, idx_ref, gate_ref, rank_ref, counts_ref, base_ref):
    @pl.when((pl.program_id(0) == 0) & (pl.program_id(1) == 0))
    def _():
        base_ref[...] = jnp.zeros_like(base_ref)

    s = s_ref[...]
    n_exp, tm = s.shape
    per_group = n_exp // N_EXPERT_GROUPS
    neg = -jnp.inf
    b = s + bias_ref[...]
    bg = b.reshape(N_EXPERT_GROUPS, per_group, tm)
    m1 = jnp.max(bg, axis=1, keepdims=True)
    is1 = bg == m1
    n1 = jnp.sum(jnp.where(is1, 1.0, 0.0), axis=1, keepdims=True)
    m2 = jnp.max(jnp.where(is1, neg, bg), axis=1, keepdims=True)
    gscore = m1 + jnp.where(n1 >= 2.0, m1, m2)
    gs = [gscore[g] for g in range(N_EXPERT_GROUPS)]
    kept = []
    for g in range(N_EXPERT_GROUPS):
        beaten = jnp.zeros((1, tm), F32)
        for o in range(N_EXPERT_GROUPS):
            if o != g:
                wins = (gs[o] >= gs[g]) if o < g else (gs[o] > gs[g])
                beaten = beaten + jnp.where(wins, 1.0, 0.0)
        kept.append(jnp.where(beaten < float(TOPK_GROUPS), bg[g], neg))
    masked = jnp.concatenate(kept, axis=0)
    row_f = lax.broadcasted_iota(jnp.int32, (n_exp, tm), 0).astype(F32)
    chosen = jnp.zeros((n_exp, tm), F32)
    firsts, vals = [], []
    for _ in range(TOP_K):
        m = jnp.max(masked, axis=0, keepdims=True)
        first = jnp.min(jnp.where(masked == m, row_f, float(n_exp)), axis=0, keepdims=True)
        sel = row_f == first
        firsts.append(first)
        vals.append(jnp.sum(jnp.where(sel, s, 0.0), axis=0, keepdims=True))
        chosen = jnp.where(sel, 1.0, chosen)
        masked = jnp.where(sel, neg, masked)
    before = _dot(chosen.astype(BF16), utri_ref[...]) + base_ref[...]
    ranks = [jnp.sum(jnp.where(row_f == f, before, 0.0), axis=0, keepdims=True) for f in firsts]
    idx_ref[...] = _stack_rows(firsts).astype(jnp.int32)
    rank_ref[...] = _stack_rows(ranks).astype(jnp.int32)
    val = _stack_rows(vals)
    gate_ref[...] = val / jnp.sum(val, axis=0, keepdims=True) * ROUTED_SCALE
    base_ref[...] = base_ref[...] + jnp.sum(chosen, axis=1, keepdims=True)
    counts_ref[...] = base_ref[...]


def _route_call(scores_t, bias, *, tm):
    bsz, n_exp, seq = scores_t.shape
    tiles = seq // tm
    tokens = bsz * seq
    r_id = lax.broadcasted_iota(jnp.int32, (tm, tm), 0)
    c_id = lax.broadcasted_iota(jnp.int32, (tm, tm), 1)
    utri = (r_id < c_id).astype(BF16)
    small = pl.BlockSpec((TOP_K, tm), lambda b, i: (0, b * tiles + i))
    col = pl.BlockSpec((n_exp, 1), lambda b, i: (0, 0))
    return pl.pallas_call(
        _route_kernel,
        out_shape=[jax.ShapeDtypeStruct((TOP_K, tokens), jnp.int32), jax.ShapeDtypeStruct((TOP_K, tokens), F32),
                   jax.ShapeDtypeStruct((TOP_K, tokens), jnp.int32), jax.ShapeDtypeStruct((n_exp, 1), F32)],
        grid=(bsz, tiles),
        in_specs=[pl.BlockSpec((None, n_exp, tm), lambda b, i: (b, 0, i)), col, _const_spec((tm, tm))],
        out_specs=[small, small, small, col],
        scratch_shapes=[pltpu.VMEM((n_exp, 1), F32)],
        compiler_params=_params(("arbitrary", "arbitrary")),
        name="route",
    )(scores_t, bias.astype(F32).reshape(n_exp, 1), utri)


def _slot_kernel(idx_ref, rank_ref, start_ref, pos_ref):
    idx = idx_ref[...]
    tm = idx.shape[1]
    n_exp = start_ref.shape[0]
    row = lax.broadcasted_iota(jnp.int32, (n_exp, tm), 0)
    starts = [jnp.sum(jnp.where(row == idx[k:k + 1, :], start_ref[...], 0.0), axis=0, keepdims=True)
              for k in range(TOP_K)]
    pos_ref[...] = rank_ref[...] + _stack_rows(starts).astype(jnp.int32)


def _slot_call(idx, rank, pad_start, *, tm):
    tokens = idx.shape[1]
    n_exp = pad_start.shape[0]
    small = pl.BlockSpec((TOP_K, tm), lambda i: (0, i))
    return pl.pallas_call(
        _slot_kernel,
        out_shape=jax.ShapeDtypeStruct((TOP_K, tokens), jnp.int32),
        grid=(tokens // tm,),
        in_specs=[small, small, _const_spec((n_exp, 1))],
        out_specs=small,
        compiler_params=_params(("parallel",)),
        name="slots",
    )(idx, rank, pad_start.astype(F32).reshape(n_exp, 1))


def _dispatch_kernel(start_ref, padded_ref, pos_ref, h_hbm, xs_hbm, hbuf, zblock, lsem, ssem, zsem, *, pieces):
    i = pl.program_id(0)
    n = pl.num_programs(0)
    nbuf = hbuf.shape[0]
    td = hbuf.shape[1] // pieces
    bm = zblock.shape[0] // pieces
    n_exp = start_ref.shape[0]
    slot = i % nbuf

    def zero_copy(first_row):
        return pltpu.make_async_copy(zblock, xs_hbm.at[pl.ds(first_row * pieces, bm * pieces), :], zsem)

    def load(step, s):
        return pltpu.make_async_copy(h_hbm.at[pl.ds(step * td * pieces, td * pieces), :], hbuf.at[s], lsem.at[s])

    def wait_scatters(s):
        rows = td * TOP_K * pieces
        pltpu.make_async_copy(h_hbm.at[pl.ds(0, rows), :], xs_hbm.at[pl.ds(0, rows), :], ssem.at[s]).wait()

    @pl.when(i == 0)
    def _():
        zblock[...] = jnp.zeros_like(zblock)

        def per_expert(e, carry):
            @pl.when(padded_ref[e] > 0)
            def _():
                zero_copy(start_ref[e] + padded_ref[e] - bm).start()
            return carry
        lax.fori_loop(0, n_exp, per_expert, 0)

        def per_expert_wait(e, carry):
            @pl.when(padded_ref[e] > 0)
            def _():
                zero_copy(0).wait()
            return carry
        lax.fori_loop(0, n_exp, per_expert_wait, 0)
        load(0, 0).start()

    @pl.when(i >= nbuf - 1)
    def _():
        wait_scatters((i + 1) % nbuf)

    @pl.when(i + 1 < n)
    def _():
        load(i + 1, (i + 1) % nbuf).start()

    load(i, slot).wait()

    for t in range(td):
        for k in range(TOP_K):
            dst = pl.multiple_of(pos_ref[k, t] * pieces, pieces)
            pltpu.make_async_copy(hbuf.at[slot, pl.ds(t * pieces, pieces), :], xs_hbm.at[pl.ds(dst, pieces), :],
                                  ssem.at[slot]).start(priority=k % 2)

    @pl.when(i == n - 1)
    def _():
        for back in range(nbuf - 1):
            @pl.when(i >= back)
            def _():
                wait_scatters((i - back) % nbuf)


def _dispatch_call(pad_start, padded, pos, hp, n_slots, *, td, bm, pieces):
    tokens = hp.shape[0] // pieces
    grid_spec = pltpu.PrefetchScalarGridSpec(
        num_scalar_prefetch=2,
        grid=(tokens // td,),
        in_specs=[pl.BlockSpec((TOP_K, td), lambda i, *_: (0, i), memory_space=pltpu.SMEM),
                  pl.BlockSpec(memory_space=pl.ANY)],
        out_specs=pl.BlockSpec(memory_space=pl.ANY),
        scratch_shapes=[pltpu.VMEM((DISPATCH_BUFFERS, td * pieces, LANES), hp.dtype),
                        pltpu.VMEM((bm * pieces, LANES), hp.dtype),
                        pltpu.SemaphoreType.DMA((DISPATCH_BUFFERS,)), pltpu.SemaphoreType.DMA((DISPATCH_BUFFERS,)),
                        pltpu.SemaphoreType.DMA],
    )
    return pl.pallas_call(
        functools.partial(_dispatch_kernel, pieces=pieces),
        out_shape=jax.ShapeDtypeStruct((n_slots * pieces, LANES), hp.dtype),
        grid_spec=grid_spec,
        compiler_params=_params(("arbitrary",)),
        name="dispatch",
    )(pad_start, padded, pos, hp)


def _experts_kernel(be_ref, nact_ref, x_ref, wg_ref, wu_ref, wd_ref, y_ref, wg_s, wu_s, wd_s):
    i = pl.program_id(0)
    last = nact_ref[0] - 1
    expert = be_ref[jnp.minimum(i, last)]
    previous = be_ref[jnp.minimum(jnp.maximum(i, 1) - 1, last)]

    @pl.when((i == 0) | (expert != previous))
    def _():
        wg_s[...] = wg_ref[...].astype(BF16)
        wu_s[...] = wu_ref[...].astype(BF16)
        wd_s[...] = wd_ref[...].astype(BF16)

    @pl.when(i <= last)
    def _():
        half = wg_s.shape[0] // 2
        pieces = half // LANES
        bm = x_ref.shape[0] // pieces
        x_hi, x_lo = (v.astype(BF16) for v in _unpack_bf16_pairs(_load_row_pieces(x_ref, 0, bm, pieces)))
        hg = _dot(x_hi, wg_s[:half, :]) + _dot(x_lo, wg_s[half:, :])
        hu = _dot(x_hi, wu_s[:half, :]) + _dot(x_lo, wu_s[half:, :])
        _store_row_pieces(y_ref, _pack_bf16_pairs(_dot((_silu(hg) * hu).astype(BF16), wd_s[...])))


def _experts_call(block_expert, n_active, xs, wg, wu, wd):
    n_blocks = block_expert.shape[0]
    bm, half = xs.shape[0] // n_blocks, LANES
    d, hidden = wg.shape[1:]
    blk = lambda i, be, na: (jnp.minimum(i, na[0] - 1), 0)
    wsel = lambda i, be, na: (be[jnp.minimum(i, na[0] - 1)], 0, 0)
    grid_spec = pltpu.PrefetchScalarGridSpec(
        num_scalar_prefetch=2,
        grid=(n_blocks,),
        in_specs=[pl.BlockSpec((bm, half), blk),
                  pl.BlockSpec((None, d, hidden), wsel),
                  pl.BlockSpec((None, d, hidden), wsel),
                  pl.BlockSpec((None, hidden, d), wsel)],
        out_specs=pl.BlockSpec((bm, half), blk),
        scratch_shapes=[pltpu.VMEM((d, hidden), BF16), pltpu.VMEM((d, hidden), BF16), pltpu.VMEM((hidden, d), BF16)],
    )
    return pl.pallas_call(
        _experts_kernel,
        out_shape=jax.ShapeDtypeStruct(xs.shape, xs.dtype),
        grid_spec=grid_spec,
        compiler_params=_params(("arbitrary",)),
        name="experts",
    )(block_expert, n_active, xs, wg, wu, wd)


def _row_gather_start(idx_ref, n_rows, pieces, src_hbm, dst_ref, sem):
    tc = idx_ref.shape[1]
    for r in range(n_rows):
        src = pl.multiple_of(idx_ref[r // tc, r % tc] * pieces, pieces)
        pltpu.make_async_copy(src_hbm.at[pl.ds(src, pieces), :], dst_ref.at[pl.ds(r * pieces, pieces), :],
                              sem).start(priority=r % 2)


def _row_gather_wait(src_hbm, dst_ref, sem):
    pltpu.make_async_copy(src_hbm.at[pl.ds(0, dst_ref.shape[0]), :], dst_ref, sem).wait()


def _combine_kernel(pos_ref, pos1_ref, pos2_ref, y_hbm, gate_ref, shared_ref, x1_ref, g2_ref, lng_ref, lnb_ref,
                    o_ref, buf_a, buf_b, buf_c, sems, *, alpha):
    i = pl.program_id(0)
    n = pl.num_programs(0)
    tc, d = o_ref.shape
    half = d // 2
    pieces = half // LANES
    rows = buf_a.shape[0] // pieces
    bufs = (buf_a, buf_b, buf_c)

    @pl.when(i == 0)
    def _():
        _row_gather_start(pos_ref, rows, pieces, y_hbm, buf_a, sems.at[0])
        _row_gather_start(pos1_ref, rows, pieces, y_hbm, buf_b, sems.at[1])

    def step(r):
        cur, nxt, far = bufs[r], bufs[(r + 1) % 3], bufs[(r + 2) % 3]
        _row_gather_wait(y_hbm, cur, sems.at[r])
        _row_gather_start(pos2_ref, rows, pieces, y_hbm, far, sems.at[(r + 2) % 3])
        r_id = lax.broadcasted_iota(jnp.int32, (tc, tc), 0)
        c_id = lax.broadcasted_iota(jnp.int32, (tc, tc), 1)
        f_hi = shared_ref[:, :half]
        f_lo = shared_ref[:, half:]
        for k in range(rows // tc):
            y_hi, y_lo = _unpack_bf16_pairs(_load_row_pieces(cur, k * tc, tc, pieces))
            g = jnp.sum(jnp.where(r_id == c_id, gate_ref[k:k + 1, :], 0.0), axis=1, keepdims=True)
            f_hi = f_hi + g * y_hi
            f_lo = f_lo + g * y_lo
        f = jnp.concatenate([f_hi, f_lo], axis=1)
        o_ref[...] = _norm_rows(alpha * x1_ref[...] + g2_ref[...] * f, LN_EPS) * lng_ref[...] + lnb_ref[...]

        @pl.when(i == n - 1)
        def _():
            _row_gather_wait(y_hbm, nxt, sems.at[(r + 1) % 3])
            _row_gather_wait(y_hbm, far, sems.at[(r + 2) % 3])

    for r in range(3):
        @pl.when(i % 3 == r)
        def _():
            step(r)


def _combine_call(pos, y_slots, gates, shared, x1, g2, lng, lnb, *, alpha, tc, tiles_per_batch):
    tokens, d = x1.shape
    n_tiles = tokens // tc
    last = n_tiles - 1
    row = pl.BlockSpec((tc, d), lambda i: (i, 0))
    buf = pltpu.VMEM((TOP_K * tc * (d // 2 // LANES), LANES), y_slots.dtype)
    return pl.pallas_call(
        functools.partial(_combine_kernel, alpha=alpha),
        out_shape=jax.ShapeDtypeStruct((tokens, d), F32),
        grid=(n_tiles,),
        in_specs=[
            pl.BlockSpec((TOP_K, tc), lambda i: (0, i), memory_space=pltpu.SMEM),
            pl.BlockSpec((TOP_K, tc), lambda i: (0, jnp.minimum(i + 1, last)), memory_space=pltpu.SMEM),
            pl.BlockSpec((TOP_K, tc), lambda i: (0, jnp.minimum(i + 2, last)), memory_space=pltpu.SMEM),
            pl.BlockSpec(memory_space=pl.ANY),
            pl.BlockSpec((TOP_K, tc), lambda i: (0, i)),
            row, row,
            pl.BlockSpec((None, 1, d), lambda i: (i // tiles_per_batch, 0, 0)),
            _const_spec(lng.shape), _const_spec(lnb.shape),
        ],
        out_specs=row,
        scratch_shapes=[buf, buf, buf, pltpu.SemaphoreType.DMA((3,))],
        compiler_params=_params(("arbitrary",)),
        name="combine",
    )(pos, pos, pos, y_slots, gates, shared, x1, g2, lng, lnb)


def kernel(x, c, ctx, c_ctx, w_mod, b_mod, w_in, s5_lam_re, s5_lam_im, s5_log_dt, s5_b_re, s5_b_im, s5_c_re, s5_c_im, s5_d, w_glu, q_norm_g, k_norm_g, w_branch_ssm, w_branch_attn, w_out, ln1_g, ln1_b, w_router, router_bias, w_exp_gate, w_exp_up, w_exp_down, w_sh_gate, w_sh_up, w_sh_down, ln2_g, ln2_b):
    depth = w_mod.shape[0]
    assert depth == 1, "single-layer block: context outputs are never needed"
    bsz, seq, d = x.shape
    ctx_len = ctx.shape[1]
    kvw = N_KV_HEADS * HEAD_DIM
    n_exp = w_router.shape[2]
    alpha = (2.0 * depth) ** 0.25
    assert seq % GRID_W == 0 and seq % S5_CHUNK == 0 and ctx_len % S5_CHUNK == 0 and bsz % SUBLANES == 0
    lay = 0

    pad = (-(bsz + 1)) % SUBLANES
    c_all = jnp.concatenate([c, c_ctx[None, :], jnp.zeros((pad, d), F32)], axis=0)
    mod = _mod_call(c_all, w_mod[lay], b_mod[lay])
    mod_lat = mod[:bsz].reshape(bsz, N_MOD, 1, d)
    sh1, sc1, g1, sh2, sc2, g2 = (mod_lat[:, k] for k in range(N_MOD))
    mod_ctx = mod[bsz].reshape(N_MOD, 1, 1, d)

    w_in_l = w_in[lay].astype(BF16)
    w_ctx = jnp.concatenate([w_in_l[:, :d], w_in_l[:, 2 * d:2 * d + 2 * kvw]], axis=1)
    head_id = jnp.arange(kvw) // HEAD_DIM
    bd = jnp.where(head_id[:, None] == head_id[None, :], 1.0 / HEAD_DIM, 0.0).astype(BF16)
    tqa, tqb = _rope_tables(seq, q_norm_g[lay], HEAD_DIM ** -0.5)
    tka, tkb = _rope_tables(seq, k_norm_g[lay], 1.0)
    tca = jnp.tile(k_norm_g[lay].astype(F32), N_KV_HEADS)[None, :]
    tm_lat = min(512, seq)
    s_lat, k_lat, v_lat, q_lat, gs_lat, ga_lat = _inproj_call(x, sh1, sc1, w_in_l, bd, tka, tkb, tqa, tqb, tm=tm_lat)
    s_ctx, k_ctx, v_ctx = _inproj_call(ctx, mod_ctx[0], mod_ctx[1], w_ctx, bd, tca, jnp.zeros_like(tca),
                                       tm=min(256, ctx_len))

    w1, wso, lam_tab = _s5_tables(s5_lam_re[lay], s5_lam_im[lay], s5_log_dt[lay], s5_b_re[lay], s5_b_im[lay],
                                  s5_c_re[lay], s5_c_im[lay])
    y_groups = _s5_call(_to_group_major(s_ctx), _to_group_major(s_lat), w1, wso, lam_tab, nb=bsz)
    y_s5 = _to_token_major(y_groups, bsz)

    attn = _attn_call(q_lat, k_ctx, k_lat, v_ctx, v_lat, tq=min(1024, seq))

    row = lambda v: v.astype(F32).reshape(1, -1)
    wss = jnp.concatenate([w_sh_gate[lay], w_sh_up[lay]], axis=1).astype(BF16)
    x1, hp, shared, scores_t = _mixout_call(
        y_s5, s_lat, attn, gs_lat, ga_lat, x, g1, sh2, sc2, row(s5_d[lay]),
        w_glu[lay].astype(BF16), w_branch_ssm[lay].astype(BF16), w_branch_attn[lay].astype(BF16),
        w_out[lay].astype(BF16), row(ln1_g[lay]), row(ln1_b[lay]), w_router[lay].T.astype(BF16), wss,
        w_sh_down[lay].astype(BF16), tm=min(512, seq), alpha=alpha)

    tokens = bsz * seq
    bm = MOE_BLOCK
    idx, gates, rank, counts_f = _route_call(scores_t, router_bias[lay], tm=min(ROUTE_TOKENS, seq))
    counts = counts_f.reshape(n_exp).astype(jnp.int32)
    padded = (counts + bm - 1) // bm * bm
    pad_end = jnp.cumsum(padded)
    pad_start = pad_end - padded
    n_blocks = (tokens * TOP_K + n_exp * (bm - 1) + bm - 1) // bm
    block_start = jnp.arange(n_blocks, dtype=jnp.int32) * bm
    block_expert = jnp.minimum(jnp.sum((pad_end[None, :] <= block_start[:, None]).astype(jnp.int32), axis=1), n_exp - 1)
    n_active = (pad_end[-1] // bm).reshape(1)
    pos = _slot_call(idx, rank, pad_start, tm=min(512, tokens))
    pieces = d // 2 // LANES
    xs = _dispatch_call(pad_start, padded, pos, hp.reshape(tokens * pieces, LANES), n_blocks * bm,
                        td=min(DISPATCH_TOKENS, tokens), bm=bm, pieces=pieces)
    y_slots = _experts_call(block_expert, n_active, xs, w_exp_gate[lay], w_exp_up[lay], w_exp_down[lay])

    tc = min(COMBINE_TOKENS, seq)
    out = _combine_call(pos, y_slots, gates, shared.reshape(tokens, d), x1.reshape(tokens, d), g2,
                        row(ln2_g[lay]), row(ln2_b[lay]), alpha=alpha, tc=tc, tiles_per_batch=seq // tc)
    return out.reshape(bsz, seq, d)
```

```python
import functools
import math

import jax
import jax.numpy as jnp
from jax import lax
from jax.experimental import pallas as pl
from jax.experimental.pallas import tpu as pltpu

F32 = jnp.float32
BF16 = jnp.bfloat16

GRID_W = 64
HEAD_DIM = 64
N_KV_HEADS = 4
S5_GROUP_CH = 16
S5_MAX_RE = -1e-4
ROPE_THETA = 10000.0
TOP_K = 8
N_EXPERT_GROUPS = 8
TOPK_GROUPS = 4
ROUTED_SCALE = 2.5
LN_EPS = 1e-5
MOD_EPS = 1e-6
RMS_EPS = 1e-6
N_MOD = 6

LANES = 128
SUBLANES = 8
VMEM_LIMIT_BYTES = 56 * 1024 * 1024

INPROJ_TOKENS = 512
INPROJ_CTX_TOKENS = 256
ATTN_TOKENS = 1024
MIXOUT_TOKENS = 512
S5_CHUNK = 16
S5_COLS = S5_CHUNK * S5_GROUP_CH
MOE_BLOCK = 512
ROUTE_TOKENS = 512
DISPATCH_TOKENS = 128
DISPATCH_BUFFERS = 3
COMBINE_TOKENS = 128


def _params(sem):
    return pltpu.CompilerParams(dimension_semantics=sem, vmem_limit_bytes=VMEM_LIMIT_BYTES)


def _const_spec(shape):
    nd = len(shape)
    return pl.BlockSpec(shape, lambda *_: (0,) * nd, pipeline_mode=pl.Buffered(1))


def _dot(a, b):
    return jnp.dot(a, b, preferred_element_type=F32)


def _norm_rows(x, eps):
    mu = jnp.mean(x, axis=-1, keepdims=True)
    xc = x - mu
    var = jnp.mean(xc * xc, axis=-1, keepdims=True)
    return xc * lax.rsqrt(var + eps)


def _silu(x):
    return x * jax.nn.sigmoid(x)


def _gelu_tanh(x):
    return 0.5 * x * (1.0 + jnp.tanh(math.sqrt(2.0 / math.pi) * (x + 0.044715 * (x * x * x))))


def _pack_bf16_pairs(v):
    n = v.shape[1] // 2
    bits = lax.bitcast_convert_type(v.astype(BF16).astype(F32), jnp.uint32)
    return bits[:, :n] | (bits[:, n:] >> 16)


def _store_row_pieces(ref, v):
    rows, width = v.shape
    n = width // LANES
    for j in range(n):
        ref[pl.ds(j, rows, stride=n), :] = v[:, j * LANES:(j + 1) * LANES]


def _load_row_pieces(ref, first_row, rows, n):
    return jnp.concatenate([ref[pl.ds(first_row * n + j, rows, stride=n), :] for j in range(n)], axis=1)


def _unpack_bf16_pairs(w):
    hi = lax.bitcast_convert_type(w & jnp.uint32(0xFFFF0000), F32)
    lo = lax.bitcast_convert_type(w << 16, F32)
    return hi, lo


def _mod_kernel(c_ref, w_ref, b_ref, o_ref):
    o_ref[...] = _dot(_silu(c_ref[...]), w_ref[...]) + b_ref[...]


def _mod_call(c_all, w_mod, b_mod):
    rows, d = c_all.shape
    n = w_mod.shape[1]
    tn = d
    return pl.pallas_call(
        _mod_kernel,
        out_shape=jax.ShapeDtypeStruct((rows, n), F32),
        grid=(n // tn,),
        in_specs=[pl.BlockSpec((rows, d), lambda j: (0, 0)),
                  pl.BlockSpec((d, tn), lambda j: (0, j)),
                  pl.BlockSpec((1, tn), lambda j: (0, j))],
        out_specs=pl.BlockSpec((rows, tn), lambda j: (0, j)),
        compiler_params=_params(("arbitrary",)),
        name="mod",
    )(c_all, w_mod, b_mod.reshape(1, n))


def _swap16(t):
    width = t.shape[1]
    lane = lax.broadcasted_iota(jnp.int32, t.shape, 1)
    first = (lane & 16) == 0
    return jnp.where(first, pltpu.roll(t, width - 16, 1), pltpu.roll(t, 16, 1))


def _rms_rope(t, bd_ref, ta, tb):
    msq = _dot((t * t).astype(BF16), bd_ref[...])
    return lax.rsqrt(msq + RMS_EPS) * (t * ta + _swap16(t) * tb)


def _store_padded_heads(t, o_ref, ones_lane=False):
    rows = t.shape[0]
    lane = lax.broadcasted_iota(jnp.int32, (rows, LANES), 1)
    lo = lane < HEAD_DIM
    fill_hi = jnp.where(lane == HEAD_DIM, 1.0, 0.0) if ones_lane else 0.0
    fill_lo = jnp.where(lane == 0, 1.0, 0.0) if ones_lane else 0.0
    for j in range(t.shape[1] // LANES):
        slab = t[:, j * LANES:(j + 1) * LANES]
        swapped = pltpu.roll(slab, HEAD_DIM, 1)
        pieces = (jnp.where(lo, slab, fill_hi), jnp.where(lo, fill_lo, swapped),
                  jnp.where(lo, swapped, fill_hi), jnp.where(lo, fill_lo, slab))
        for p, piece in enumerate(pieces):
            c0 = (4 * j + p) * LANES
            o_ref[:, c0:c0 + LANES] = piece.astype(o_ref.dtype)


def _inproj_kernel(x_ref, sh_ref, sc_ref, w_ref, bd_ref, tka_ref, tkb_ref, *rest, d, kvw, has_q):
    if has_q:
        tqa_ref, tqb_ref, s_ref, k_ref, v_ref, q_ref, gs_ref, ga_ref = rest
    else:
        s_ref, k_ref, v_ref = rest
    u = (_norm_rows(x_ref[...], MOD_EPS) * (1.0 + sc_ref[...]) + sh_ref[...]).astype(BF16)
    col = 0
    s_ref[...] = _dot(u, w_ref[:, col:col + d]).astype(s_ref.dtype)
    col += d
    if has_q:
        for c in range(d // kvw):
            q = _dot(u, w_ref[:, col + c * kvw:col + (c + 1) * kvw])
            q_ref[:, c * kvw:(c + 1) * kvw] = _rms_rope(q, bd_ref, tqa_ref[...], tqb_ref[...]).astype(q_ref.dtype)
        col += d
    k = _dot(u, w_ref[:, col:col + kvw])
    _store_padded_heads(_rms_rope(k, bd_ref, tka_ref[...], tkb_ref[...]), k_ref)
    col += kvw
    _store_padded_heads(_dot(u, w_ref[:, col:col + kvw]), v_ref, ones_lane=True)
    col += kvw
    if has_q:
        gs_ref[...] = jax.nn.sigmoid(_dot(u, w_ref[:, col:col + d])).astype(gs_ref.dtype)
        col += d
        ga_ref[...] = jax.nn.sigmoid(_dot(u, w_ref[:, col:col + d])).astype(ga_ref.dtype)


def _inproj_call(x, shift, scale, w, bd, tka, tkb, tqa=None, tqb=None, *, tm):
    bsz, length, d = x.shape
    has_q = tqa is not None
    kvw = N_KV_HEADS * HEAD_DIM
    per_batch = shift.shape[0] > 1
    tab_rows = tka.shape[0]
    tab_blk = tm if tab_rows > 1 else 1
    mod_spec = pl.BlockSpec((None, 1, d), (lambda b, i: (b, 0, 0)) if per_batch else (lambda b, i: (0, 0, 0)))
    tab_spec = pl.BlockSpec((tab_blk, kvw), (lambda b, i: (i, 0)) if tab_rows > 1 else (lambda b, i: (0, 0)))
    row_spec = lambda width: pl.BlockSpec((None, tm, width), lambda b, i: (b, i, 0))
    in_specs = [row_spec(d), mod_spec, mod_spec, _const_spec(w.shape), _const_spec(bd.shape), tab_spec, tab_spec]
    args = [x, shift, scale, w, bd, tka, tkb]
    widths = [d, 4 * kvw, 4 * kvw]
    if has_q:
        in_specs += [tab_spec, tab_spec]
        args += [tqa, tqb]
        widths += [d, d, d]
    return pl.pallas_call(
        functools.partial(_inproj_kernel, d=d, kvw=kvw, has_q=has_q),
        out_shape=[jax.ShapeDtypeStruct((bsz, length, wd), BF16) for wd in widths],
        grid=(bsz, length // tm),
        in_specs=in_specs,
        out_specs=[row_spec(wd) for wd in widths],
        compiler_params=_params(("parallel", "parallel")),
        name="inproj_lat" if has_q else "inproj_ctx",
    )(*args)


def _rope_tables(seq, gain, scale):
    half = HEAD_DIM // 2
    inv_freq = ROPE_THETA ** (-jnp.arange(0, half, 2, dtype=F32) / half)
    t = jnp.arange(seq, dtype=jnp.int32)
    pos = jnp.stack([(t // GRID_W).astype(F32), (t % GRID_W).astype(F32)], axis=1)
    dim = jnp.arange(HEAD_DIM)
    axis = dim // half
    second = ((dim % half) // (half // 2)) == 1
    freq = inv_freq[dim % (half // 2)]
    ang = pos[:, axis] * freq[None, :]
    partner = jnp.where(second, dim - half // 2, dim + half // 2)
    g = gain.astype(F32)
    ta = jnp.cos(ang) * g[None, :] * scale
    tb = jnp.sin(ang) * jnp.where(second, 1.0, -1.0)[None, :] * g[partner][None, :] * scale
    return jnp.tile(ta, (1, N_KV_HEADS)), jnp.tile(tb, (1, N_KV_HEADS))


def _attn_kernel(q_ref, kc_ref, kl_ref, vc_ref, vl_ref, o_ref):
    nt = (((1,), (1,)), ((), ()))
    rows = q_ref.shape[0]
    lane = lax.broadcasted_iota(jnp.int32, (rows, LANES), 1)
    for j in range(q_ref.shape[1] // LANES):
        qs = q_ref[:, j * LANES:(j + 1) * LANES]
        acc = []
        for half in range(2):
            cols = slice(half * LANES, (half + 1) * LANES)
            s_c = lax.dot_general(qs, kc_ref[:, cols], nt, preferred_element_type=F32)
            s_l = lax.dot_general(qs, kl_ref[:, cols], nt, preferred_element_type=F32)
            m = jnp.maximum(jnp.max(s_c, axis=-1, keepdims=True), jnp.max(s_l, axis=-1, keepdims=True))
            e_c = jnp.exp((s_c - m).astype(BF16))
            e_l = jnp.exp((s_l - m).astype(BF16))
            acc.append(_dot(e_c, vc_ref[:, cols]) + _dot(e_l, vl_ref[:, cols]))
        out = jnp.where(lane < HEAD_DIM, acc[0] / acc[0][:, HEAD_DIM:HEAD_DIM + 1], acc[1] / acc[1][:, 0:1])
        o_ref[:, j * LANES:(j + 1) * LANES] = out.astype(o_ref.dtype)


def _attn_call(q, kc, kl, vc, vl, *, tq):
    bsz, seq, d = q.shape
    ctx = kc.shape[1]
    gw = d // N_KV_HEADS
    q_spec = pl.BlockSpec((None, tq, gw), lambda b, h, i: (b, i, h))
    kv_spec = lambda length: pl.BlockSpec((None, length, 2 * LANES), lambda b, h, i: (b, 0, h))
    return pl.pallas_call(
        _attn_kernel,
        out_shape=jax.ShapeDtypeStruct((bsz, seq, d), BF16),
        grid=(bsz, N_KV_HEADS, seq // tq),
        in_specs=[q_spec, kv_spec(ctx), kv_spec(seq), kv_spec(ctx), kv_spec(seq)],
        out_specs=q_spec,
        compiler_params=_params(("parallel", "parallel", "arbitrary")),
        name="attention",
    )(q, kc, kl, vc, vl)


def _s5_tables(lam_re, lam_im, log_dt, b_re, b_im, c_re, c_im):
    n = S5_CHUNK
    lam = lax.complex(jnp.minimum(lam_re.astype(F32), S5_MAX_RE), lam_im.astype(F32))
    lam_dt = lam * jnp.exp(log_dt.astype(F32))[..., None]
    b_bar = ((jnp.exp(lam_dt) - 1.0) / lam)[..., None] * lax.complex(b_re.astype(F32), b_im.astype(F32))
    c_mat = lax.complex(c_re.astype(F32), c_im.astype(F32))
    pw = jnp.exp(lam_dt[None] * jnp.arange(n + 1, dtype=F32)[:, None, None, None])
    kern = jnp.real(jnp.einsum('dgcp,jdgp,dgpe->djgce', c_mat, pw[:n], b_bar))
    s_idx = jnp.arange(n)[:, None]
    t_idx = jnp.arange(n)[None, :]

    def toeplitz(k, lag):
        onehot = (lag[None] == jnp.arange(n)[:, None, None]).astype(F32)
        return jnp.einsum('jst,jgce->gsetc', onehot, k, precision=lax.Precision.HIGHEST)

    toep = toeplitz(kern[0], t_idx - s_idx) + toeplitz(kern[1], s_idx - t_idx)
    groups = toep.shape[0]
    toep = toep.reshape(groups, S5_COLS, S5_COLS)
    wis_f = jnp.einsum('sgp,gpe->gsep', pw[n - 1 - jnp.arange(n), 0], b_bar[0]).reshape(groups, S5_COLS, -1)
    wis_r = jnp.einsum('sgp,gpe->gsep', pw[jnp.arange(n), 1], b_bar[1]).reshape(groups, S5_COLS, -1)
    w1 = jnp.concatenate([toep, jnp.real(wis_f), jnp.real(wis_r), jnp.imag(wis_f), jnp.imag(wis_r)], axis=-1)
    m_f = jnp.einsum('gcp,tgp->gptc', c_mat[0], pw[1 + jnp.arange(n), 0]).reshape(groups, -1, S5_COLS)
    m_r = jnp.einsum('gcp,tgp->gptc', c_mat[1], pw[n - jnp.arange(n), 1]).reshape(groups, -1, S5_COLS)
    wso = jnp.concatenate([jnp.real(m_f), jnp.real(m_r), -jnp.imag(m_f), -jnp.imag(m_r)], axis=1)
    lam_n = jnp.concatenate([pw[n, 0], pw[n, 1]], axis=-1)
    lam_tab = jnp.stack([jnp.real(lam_n), jnp.imag(lam_n)], axis=1)
    return w1.astype(BF16), wso.astype(BF16), lam_tab


def _s5_kernel(uc_ref, ul_ref, w1_ref, wso_ref, lam_ref, y_ref, a_ref, xf_re, xf_im, xr_re, xr_im,
               *, nb, nc_ctx, nc):
    cols = S5_COLS
    half = lam_ref.shape[1] // 2
    a_ref[:nc_ctx * nb, :] = _dot(uc_ref[...], w1_ref[...])
    a_ref[nc_ctx * nb:, :] = _dot(ul_ref[...], w1_ref[...])
    l_re = lam_ref[0:1, :]
    l_im = lam_ref[1:2, :]
    lo = lax.broadcasted_iota(jnp.int32, (nb, 2 * half), 1) < half

    def step(i, carry):
        x_re, x_im = carry
        i_rev = jnp.where(i < nc_ctx, nc_ctx - 1 - i, nc - 1 + nc_ctx - i)
        rf = pl.ds(pl.multiple_of(i * nb, nb), nb)
        rr = pl.ds(pl.multiple_of(i_rev * nb, nb), nb)
        xf_re[rf, :] = x_re
        xf_im[rf, :] = x_im
        xr_re[rr, :] = x_re
        xr_im[rr, :] = x_im
        s_re = jnp.where(lo, a_ref[rf, cols:cols + 2 * half], a_ref[rr, cols:cols + 2 * half])
        s_im = jnp.where(lo, a_ref[rf, cols + 2 * half:cols + 4 * half], a_ref[rr, cols + 2 * half:cols + 4 * half])
        return l_re * x_re - l_im * x_im + s_re, l_re * x_im + l_im * x_re + s_im

    zero = jnp.zeros((nb, 2 * half), F32)
    lax.fori_loop(0, nc, step, (zero, zero))
    r0 = nc_ctx * nb
    rows = (nc - nc_ctx) * nb
    lo_all = lax.broadcasted_iota(jnp.int32, (rows, 2 * half), 1) < half
    y = a_ref[r0:, 0:cols]
    for k, (f_ref, r_ref) in enumerate(((xf_re, xr_re), (xf_im, xr_im))):
        st = jnp.where(lo_all, f_ref[r0:, :], r_ref[r0:, :])
        hi = st.astype(BF16)
        lo_part = (st - hi.astype(F32)).astype(BF16)
        w = wso_ref[k * 2 * half:(k + 1) * 2 * half, :]
        y = y + _dot(hi, w) + _dot(lo_part, w)
    y_ref[...] = y.astype(y_ref.dtype)


def _s5_call(u_ctx, u_lat, w1, wso, lam_tab, *, nb):
    groups, out_rows, cols = u_lat.shape
    ctx_rows = u_ctx.shape[1]
    rows = ctx_rows + out_rows
    nc = rows // nb
    nc_ctx = ctx_rows // nb
    st = lam_tab.shape[2]
    return pl.pallas_call(
        functools.partial(_s5_kernel, nb=nb, nc_ctx=nc_ctx, nc=nc),
        out_shape=jax.ShapeDtypeStruct((groups, out_rows, cols), BF16),
        grid=(groups,),
        in_specs=[pl.BlockSpec((None, ctx_rows, cols), lambda g: (g, 0, 0)),
                  pl.BlockSpec((None, out_rows, cols), lambda g: (g, 0, 0)),
                  pl.BlockSpec((None,) + w1.shape[1:], lambda g: (g, 0, 0)),
                  pl.BlockSpec((None,) + wso.shape[1:], lambda g: (g, 0, 0)),
                  pl.BlockSpec((None, 2, st), lambda g: (g, 0, 0))],
        out_specs=pl.BlockSpec((None, out_rows, cols), lambda g: (g, 0, 0)),
        scratch_shapes=[pltpu.VMEM((rows, w1.shape[2]), F32)] + [pltpu.VMEM((rows, st), F32)] * 4,
        compiler_params=_params(("parallel",)),
        name="s5",
    )(u_ctx, u_lat, w1, wso, lam_tab)


def _to_group_major(s):
    bsz, length, width = s.shape
    groups = width // S5_GROUP_CH
    nc = length // S5_CHUNK
    t = s.reshape(bsz, nc, S5_CHUNK, groups, S5_GROUP_CH).transpose(3, 1, 0, 2, 4)
    return t.reshape(groups, nc * bsz, S5_COLS)


def _to_token_major(y, bsz):
    groups, rows, _ = y.shape
    nc = rows // bsz
    t = y.reshape(groups, nc, bsz, S5_CHUNK, S5_GROUP_CH).transpose(2, 1, 3, 0, 4)
    return t.reshape(bsz, nc * S5_CHUNK, groups * S5_GROUP_CH)


def _mixout_kernel(y_ref, s_ref, at_ref, gs_ref, ga_ref, x_ref, g1_ref, sh2_ref, sc2_ref, dsk_ref,
                   wglu_ref, wbs_ref, wba_ref, wout_ref, lng_ref, lnb_ref, wrt_ref, wss_ref, wsd_ref,
                   x1_ref, h_ref, shared_ref, scores_ref, *, alpha):
    y = y_ref[...].astype(F32) + s_ref[...].astype(F32) * dsk_ref[...]
    z = _gelu_tanh(y)
    ssm = z * jax.nn.sigmoid(_dot(z.astype(BF16), wglu_ref[...]))
    merged = (gs_ref[...].astype(F32) * _dot(ssm.astype(BF16), wbs_ref[...])
              + ga_ref[...].astype(F32) * _dot(at_ref[...], wba_ref[...]))
    y_mix = _dot(merged.astype(BF16), wout_ref[...])
    x1 = _norm_rows(alpha * x_ref[...] + g1_ref[...] * y_mix, LN_EPS) * lng_ref[...] + lnb_ref[...]
    x1_ref[...] = x1
    h = _norm_rows(x1, MOD_EPS) * (1.0 + sc2_ref[...]) + sh2_ref[...]
    _store_row_pieces(h_ref, _pack_bf16_pairs(h))
    hb = h.astype(BF16)
    scores_ref[...] = jax.nn.sigmoid(
        lax.dot_general(wrt_ref[...], hb, (((1,), (1,)), ((), ())), preferred_element_type=F32))
    ss = _dot(hb, wss_ref[...])
    sh_hidden = wsd_ref.shape[0]
    hid = _silu(ss[:, :sh_hidden]) * ss[:, sh_hidden:]
    shared_ref[...] = _dot(hid.astype(BF16), wsd_ref[...])


def _mixout_call(y, s, attn, gs, ga, x, g1, sh2, sc2, dsk, wglu, wbs, wba, wout, lng, lnb, wrt, wss, wsd,
                 *, tm, alpha):
    bsz, seq, d = x.shape
    n_exp = wrt.shape[0]
    pieces = d // 2 // LANES
    row = lambda width: pl.BlockSpec((None, tm, width), lambda b, i: (b, i, 0))
    mod = pl.BlockSpec((None, 1, d), lambda b, i: (b, 0, 0))
    consts = [dsk, wglu, wbs, wba, wout, lng, lnb, wrt, wss, wsd]
    return pl.pallas_call(
        functools.partial(_mixout_kernel, alpha=alpha),
        out_shape=[jax.ShapeDtypeStruct((bsz, seq, d), F32),
                   jax.ShapeDtypeStruct((bsz, seq * pieces, LANES), jnp.uint32),
                   jax.ShapeDtypeStruct((bsz, seq, d), F32), jax.ShapeDtypeStruct((bsz, n_exp, seq), F32)],
        grid=(bsz, seq // tm),
        in_specs=[row(d)] * 6 + [mod] * 3 + [_const_spec(a.shape) for a in consts],
        out_specs=[row(d), pl.BlockSpec((None, tm * pieces, LANES), lambda b, i: (b, i, 0)),
                   row(d), pl.BlockSpec((None, n_exp, tm), lambda b, i: (b, 0, i))],
        compiler_params=_params(("parallel", "parallel")),
        name="mixout",
    )(y, s, attn, gs, ga, x, g1, sh2, sc2, *consts)


def _stack_rows(rows):
    n = rows[0].shape[1]
    row_k = lax.broadcasted_iota(jnp.int32, (len(rows), n), 0)
    out = jnp.zeros((len(rows), n), rows[0].dtype)
    for k, r in enumerate(rows):
        out = jnp.where(row_k == k, r, out)
    return out


def _route_kernel(s_ref, bias_ref, utri_ref, idx_ref, gate_ref, rank_ref, counts_ref, base_ref):
    @pl.when((pl.program_id(0) == 0) & (pl.program_id(1) == 0))
    def _():
        base_ref[...] = jnp.zeros_like(base_ref)

    s = s_ref[...]
    n_exp, tm = s.shape
    per_group = n_exp // N_EXPERT_GROUPS
    neg = -jnp.inf
    b = s + bias_ref[...]
    bg = b.reshape(N_EXPERT_GROUPS, per_group, tm)
    m1 = jnp.max(bg, axis=1, keepdims=True)
    is1 = bg == m1
    n1 = jnp.sum(jnp.where(is1, 1.0, 0.0), axis=1, keepdims=True)
    m2 = jnp.max(jnp.where(is1, neg, bg), axis=1, keepdims=True)
    gscore = m1 + jnp.where(n1 >= 2.0, m1, m2)
    gs = [gscore[g] for g in range(N_EXPERT_GROUPS)]
    kept = []
    for g in range(N_EXPERT_GROUPS):
        beaten = jnp.zeros((1, tm), F32)
        for o in range(N_EXPERT_GROUPS):
            if o != g:
                wins = (gs[o] >= gs[g]) if o < g else (gs[o] > gs[g])
                beaten = beaten + jnp.where(wins, 1.0, 0.0)
        kept.append(jnp.where(beaten < float(TOPK_GROUPS), bg[g], neg))
    masked = jnp.concatenate(kept, axis=0)
    row_f = lax.broadcasted_iota(jnp.int32, (n_exp, tm), 0).astype(F32)
    chosen = jnp.zeros((n_exp, tm), F32)
    firsts, vals = [], []
    for _ in range(TOP_K):
        m = jnp.max(masked, axis=0, keepdims=True)
        first = jnp.min(jnp.where(masked == m, row_f, float(n_exp)), axis=0, keepdims=True)
        sel = row_f == first
        firsts.append(first)
        vals.append(jnp.sum(jnp.where(sel, s, 0.0), axis=0, keepdims=True))
        chosen = jnp.where(sel, 1.0, chosen)
        masked = jnp.where(sel, neg, masked)
    before = _dot(chosen.astype(BF16), utri_ref[...]) + base_ref[...]
    ranks = [jnp.sum(jnp.where(row_f == f, before, 0.0), axis=0, keepdims=True) for f in firsts]
    idx_ref[...] = _stack_rows(firsts).astype(jnp.int32)
    rank_ref[...] = _stack_rows(ranks).astype(jnp.int32)
    val = _stack_rows(vals)
    gate_ref[...] = val / jnp.sum(val, axis=0, keepdims=True) * ROUTED_SCALE
    base_ref[...] = base_ref[...] + jnp.sum(chosen, axis=1, keepdims=True)
    counts_ref[...] = base_ref[...]


def _route_call(scores_t, bias, *, tm):
    bsz, n_exp, seq = scores_t.shape
    tiles = seq // tm
    tokens = bsz * seq
    r_id = lax.broadcasted_iota(jnp.int32, (tm, tm), 0)
    c_id = lax.broadcasted_iota(jnp.int32, (tm, tm), 1)
    utri = (r_id < c_id).astype(BF16)
    small = pl.BlockSpec((TOP_K, tm), lambda b, i: (0, b * tiles + i))
    col = pl.BlockSpec((n_exp, 1), lambda b, i: (0, 0))
    return pl.pallas_call(
        _route_kernel,
        out_shape=[jax.ShapeDtypeStruct((TOP_K, tokens), jnp.int32), jax.ShapeDtypeStruct((TOP_K, tokens), F32),
                   jax.ShapeDtypeStruct((TOP_K, tokens), jnp.int32), jax.ShapeDtypeStruct((n_exp, 1), F32)],
        grid=(bsz, tiles),
        in_specs=[pl.BlockSpec((None, n_exp, tm), lambda b, i: (b, 0, i)), col, _const_spec((tm, tm))],
        out_specs=[small, small, small, col],
        scratch_shapes=[pltpu.VMEM((n_exp, 1), F32)],
        compiler_params=_params(("arbitrary", "arbitrary")),
        name="route",
    )(scores_t, bias.astype(F32).reshape(n_exp, 1), utri)


def _slot_kernel(idx_ref, rank_ref, start_ref, pos_ref):
    idx = idx_ref[...]
    tm = idx.shape[1]
    n_exp = start_ref.shape[0]
    row = lax.broadcasted_iota(jnp.int32, (n_exp, tm), 0)
    starts = [jnp.sum(jnp.where(row == idx[k:k + 1, :], start_ref[...], 0.0), axis=0, keepdims=True)
              for k in range(TOP_K)]
    pos_ref[...] = rank_ref[...] + _stack_rows(starts).astype(jnp.int32)


def _slot_call(idx, rank, pad_start, *, tm):
    tokens = idx.shape[1]
    n_exp = pad_start.shape[0]
    small = pl.BlockSpec((TOP_K, tm), lambda i: (0, i))
    return pl.pallas_call(
        _slot_kernel,
        out_shape=jax.ShapeDtypeStruct((TOP_K, tokens), jnp.int32),
        grid=(tokens // tm,),
        in_specs=[small, small, _const_spec((n_exp, 1))],
        out_specs=small,
        compiler_params=_params(("parallel",)),
        name="slots",
    )(idx, rank, pad_start.astype(F32).reshape(n_exp, 1))


def _dispatch_kernel(start_ref, padded_ref, pos_ref, h_hbm, xs_hbm, hbuf, zblock, lsem, ssem, zsem, *, pieces):
    i = pl.program_id(0)
    n = pl.num_programs(0)
    nbuf = hbuf.shape[0]
    td = hbuf.shape[1] // pieces
    bm = zblock.shape[0] // pieces
    n_exp = start_ref.shape[0]
    slot = i % nbuf

    def zero_copy(first_row):
        return pltpu.make_async_copy(zblock, xs_hbm.at[pl.ds(first_row * pieces, bm * pieces), :], zsem)

    def load(step, s):
        return pltpu.make_async_copy(h_hbm.at[pl.ds(step * td * pieces, td * pieces), :], hbuf.at[s], lsem.at[s])

    def wait_scatters(s):
        rows = td * TOP_K * pieces
        pltpu.make_async_copy(h_hbm.at[pl.ds(0, rows), :], xs_hbm.at[pl.ds(0, rows), :], ssem.at[s]).wait()

    @pl.when(i == 0)
    def _():
        zblock[...] = jnp.zeros_like(zblock)

        def per_expert(e, carry):
            @pl.when(padded_ref[e] > 0)
            def _():
                zero_copy(start_ref[e] + padded_ref[e] - bm).start()
            return carry
        lax.fori_loop(0, n_exp, per_expert, 0)

        def per_expert_wait(e, carry):
            @pl.when(padded_ref[e] > 0)
            def _():
                zero_copy(0).wait()
            return carry
        lax.fori_loop(0, n_exp, per_expert_wait, 0)
        load(0, 0).start()

    @pl.when(i >= nbuf - 1)
    def _():
        wait_scatters((i + 1) % nbuf)

    @pl.when(i + 1 < n)
    def _():
        load(i + 1, (i + 1) % nbuf).start()

    load(i, slot).wait()

    for t in range(td):
        for k in range(TOP_K):
            dst = pl.multiple_of(pos_ref[k, t] * pieces, pieces)
            pltpu.make_async_copy(hbuf.at[slot, pl.ds(t * pieces, pieces), :], xs_hbm.at[pl.ds(dst, pieces), :],
                                  ssem.at[slot]).start(priority=k % 2)

    @pl.when(i == n - 1)
    def _():
        for back in range(nbuf - 1):
            @pl.when(i >= back)
            def _():
                wait_scatters((i - back) % nbuf)


def _dispatch_call(pad_start, padded, pos, hp, n_slots, *, td, bm, pieces):
    tokens = hp.shape[0] // pieces
    grid_spec = pltpu.PrefetchScalarGridSpec(
        num_scalar_prefetch=2,
        grid=(tokens // td,),
        in_specs=[pl.BlockSpec((TOP_K, td), lambda i, *_: (0, i), memory_space=pltpu.SMEM),
                  pl.BlockSpec(memory_space=pl.ANY)],
        out_specs=pl.BlockSpec(memory_space=pl.ANY),
        scratch_shapes=[pltpu.VMEM((DISPATCH_BUFFERS, td * pieces, LANES), hp.dtype),
                        pltpu.VMEM((bm * pieces, LANES), hp.dtype),
                        pltpu.SemaphoreType.DMA((DISPATCH_BUFFERS,)), pltpu.SemaphoreType.DMA((DISPATCH_BUFFERS,)),
                        pltpu.SemaphoreType.DMA],
    )
    return pl.pallas_call(
        functools.partial(_dispatch_kernel, pieces=pieces),
        out_shape=jax.ShapeDtypeStruct((n_slots * pieces, LANES), hp.dtype),
        grid_spec=grid_spec,
        compiler_params=_params(("arbitrary",)),
        name="dispatch",
    )(pad_start, padded, pos, hp)


def _experts_kernel(be_ref, nact_ref, x_ref, wg_ref, wu_ref, wd_ref, y_ref, wg_s, wu_s, wd_s):
    i = pl.program_id(0)
    last = nact_ref[0] - 1
    expert = be_ref[jnp.minimum(i, last)]
    previous = be_ref[jnp.minimum(jnp.maximum(i, 1) - 1, last)]

    @pl.when((i == 0) | (expert != previous))
    def _():
        wg_s[...] = wg_ref[...].astype(BF16)
        wu_s[...] = wu_ref[...].astype(BF16)
        wd_s[...] = wd_ref[...].astype(BF16)

    @pl.when(i <= last)
    def _():
        half = wg_s.shape[0] // 2
        pieces = half // LANES
        bm = x_ref.shape[0] // pieces
        x_hi, x_lo = (v.astype(BF16) for v in _unpack_bf16_pairs(_load_row_pieces(x_ref, 0, bm, pieces)))
        hg = _dot(x_hi, wg_s[:half, :]) + _dot(x_lo, wg_s[half:, :])
        hu = _dot(x_hi, wu_s[:half, :]) + _dot(x_lo, wu_s[half:, :])
        _store_row_pieces(y_ref, _pack_bf16_pairs(_dot((_silu(hg) * hu).astype(BF16), wd_s[...])))


def _experts_call(block_expert, n_active, xs, wg, wu, wd):
    n_blocks = block_expert.shape[0]
    block_rows = xs.shape[0] // n_blocks
    d, hidden = wg.shape[1:]
    blk = lambda i, be, na: (jnp.minimum(i, na[0] - 1), 0)
    wsel = lambda i, be, na: (be[jnp.minimum(i, na[0] - 1)], 0, 0)
    grid_spec = pltpu.PrefetchScalarGridSpec(
        num_scalar_prefetch=2,
        grid=(n_blocks,),
        in_specs=[pl.BlockSpec((block_rows, LANES), blk),
                  pl.BlockSpec((None, d, hidden), wsel),
                  pl.BlockSpec((None, d, hidden), wsel),
                  pl.BlockSpec((None, hidden, d), wsel)],
        out_specs=pl.BlockSpec((block_rows, LANES), blk),
        scratch_shapes=[pltpu.VMEM((d, hidden), BF16), pltpu.VMEM((d, hidden), BF16), pltpu.VMEM((hidden, d), BF16)],
    )
    return pl.pallas_call(
        _experts_kernel,
        out_shape=jax.ShapeDtypeStruct(xs.shape, xs.dtype),
        grid_spec=grid_spec,
        compiler_params=_params(("arbitrary",)),
        name="experts",
    )(block_expert, n_active, xs, wg, wu, wd)


def _row_gather_start(idx_ref, n_rows, pieces, src_hbm, dst_ref, sem):
    tc = idx_ref.shape[1]
    for r in range(n_rows):
        src = pl.multiple_of(idx_ref[r // tc, r % tc] * pieces, pieces)
        pltpu.make_async_copy(src_hbm.at[pl.ds(src, pieces), :], dst_ref.at[pl.ds(r * pieces, pieces), :],
                              sem).start(priority=r % 2)


def _row_gather_wait(src_hbm, dst_ref, sem):
    pltpu.make_async_copy(src_hbm.at[pl.ds(0, dst_ref.shape[0]), :], dst_ref, sem).wait()


def _combine_kernel(pos_ref, pos1_ref, pos2_ref, y_hbm, gate_ref, shared_ref, x1_ref, g2_ref, lng_ref, lnb_ref,
                    o_ref, buf_a, buf_b, buf_c, sems, *, alpha):
    i = pl.program_id(0)
    n = pl.num_programs(0)
    tc, d = o_ref.shape
    half = d // 2
    pieces = half // LANES
    rows = buf_a.shape[0] // pieces
    bufs = (buf_a, buf_b, buf_c)

    @pl.when(i == 0)
    def _():
        _row_gather_start(pos_ref, rows, pieces, y_hbm, buf_a, sems.at[0])
        _row_gather_start(pos1_ref, rows, pieces, y_hbm, buf_b, sems.at[1])

    def step(r):
        cur, nxt, far = bufs[r], bufs[(r + 1) % 3], bufs[(r + 2) % 3]
        _row_gather_wait(y_hbm, cur, sems.at[r])
        _row_gather_start(pos2_ref, rows, pieces, y_hbm, far, sems.at[(r + 2) % 3])
        r_id = lax.broadcasted_iota(jnp.int32, (tc, tc), 0)
        c_id = lax.broadcasted_iota(jnp.int32, (tc, tc), 1)
        f_hi = shared_ref[:, :half]
        f_lo = shared_ref[:, half:]
        for k in range(rows // tc):
            y_hi, y_lo = _unpack_bf16_pairs(_load_row_pieces(cur, k * tc, tc, pieces))
            g = jnp.sum(jnp.where(r_id == c_id, gate_ref[k:k + 1, :], 0.0), axis=1, keepdims=True)
            f_hi = f_hi + g * y_hi
            f_lo = f_lo + g * y_lo
        f = jnp.concatenate([f_hi, f_lo], axis=1)
        o_ref[...] = _norm_rows(alpha * x1_ref[...] + g2_ref[...] * f, LN_EPS) * lng_ref[...] + lnb_ref[...]

        @pl.when(i == n - 1)
        def _():
            _row_gather_wait(y_hbm, nxt, sems.at[(r + 1) % 3])
            _row_gather_wait(y_hbm, far, sems.at[(r + 2) % 3])

    for r in range(3):
        @pl.when(i % 3 == r)
        def _():
            step(r)


def _combine_call(pos, y_slots, gates, shared, x1, g2, lng, lnb, *, alpha, tc, tiles_per_batch):
    tokens, d = x1.shape
    n_tiles = tokens // tc
    last = n_tiles - 1
    row = pl.BlockSpec((tc, d), lambda i: (i, 0))
    buf = pltpu.VMEM((TOP_K * tc * (d // 2 // LANES), LANES), y_slots.dtype)
    return pl.pallas_call(
        functools.partial(_combine_kernel, alpha=alpha),
        out_shape=jax.ShapeDtypeStruct((tokens, d), F32),
        grid=(n_tiles,),
        in_specs=[
            pl.BlockSpec((TOP_K, tc), lambda i: (0, i), memory_space=pltpu.SMEM),
            pl.BlockSpec((TOP_K, tc), lambda i: (0, jnp.minimum(i + 1, last)), memory_space=pltpu.SMEM),
            pl.BlockSpec((TOP_K, tc), lambda i: (0, jnp.minimum(i + 2, last)), memory_space=pltpu.SMEM),
            pl.BlockSpec(memory_space=pl.ANY),
            pl.BlockSpec((TOP_K, tc), lambda i: (0, i)),
            row, row,
            pl.BlockSpec((None, 1, d), lambda i: (i // tiles_per_batch, 0, 0)),
            _const_spec(lng.shape), _const_spec(lnb.shape),
        ],
        out_specs=row,
        scratch_shapes=[buf, buf, buf, pltpu.SemaphoreType.DMA((3,))],
        compiler_params=_params(("arbitrary",)),
        name="combine",
    )(pos, pos, pos, y_slots, gates, shared, x1, g2, lng, lnb)


def kernel(x, c, ctx, c_ctx, w_mod, b_mod, w_in, s5_lam_re, s5_lam_im, s5_log_dt, s5_b_re, s5_b_im, s5_c_re, s5_c_im, s5_d, w_glu, q_norm_g, k_norm_g, w_branch_ssm, w_branch_attn, w_out, ln1_g, ln1_b, w_router, router_bias, w_exp_gate, w_exp_up, w_exp_down, w_sh_gate, w_sh_up, w_sh_down, ln2_g, ln2_b):
    depth = w_mod.shape[0]
    assert depth == 1, "single-layer block: context outputs are never needed"
    bsz, seq, d = x.shape
    ctx_len = ctx.shape[1]
    kvw = N_KV_HEADS * HEAD_DIM
    n_exp = w_router.shape[2]
    alpha = (2.0 * depth) ** 0.25
    assert seq % GRID_W == 0 and seq % S5_CHUNK == 0 and ctx_len % S5_CHUNK == 0 and bsz % SUBLANES == 0
    lay = 0

    pad = (-(bsz + 1)) % SUBLANES
    c_all = jnp.concatenate([c, c_ctx[None, :], jnp.zeros((pad, d), F32)], axis=0)
    mod = _mod_call(c_all, w_mod[lay], b_mod[lay])
    mod_lat = mod[:bsz].reshape(bsz, N_MOD, 1, d)
    sh1, sc1, g1, sh2, sc2, g2 = (mod_lat[:, k] for k in range(N_MOD))
    mod_ctx = mod[bsz].reshape(N_MOD, 1, 1, d)

    w_in_l = w_in[lay].astype(BF16)
    w_ctx = jnp.concatenate([w_in_l[:, :d], w_in_l[:, 2 * d:2 * d + 2 * kvw]], axis=1)
    head_id = jnp.arange(kvw) // HEAD_DIM
    bd = jnp.where(head_id[:, None] == head_id[None, :], 1.0 / HEAD_DIM, 0.0).astype(BF16)
    tqa, tqb = _rope_tables(seq, q_norm_g[lay], HEAD_DIM ** -0.5)
    tka, tkb = _rope_tables(seq, k_norm_g[lay], 1.0)
    tca = jnp.tile(k_norm_g[lay].astype(F32), N_KV_HEADS)[None, :]
    s_lat, k_lat, v_lat, q_lat, gs_lat, ga_lat = _inproj_call(x, sh1, sc1, w_in_l, bd, tka, tkb, tqa, tqb,
                                                              tm=min(INPROJ_TOKENS, seq))
    s_ctx, k_ctx, v_ctx = _inproj_call(ctx, mod_ctx[0], mod_ctx[1], w_ctx, bd, tca, jnp.zeros_like(tca),
                                       tm=min(INPROJ_CTX_TOKENS, ctx_len))

    w1, wso, lam_tab = _s5_tables(s5_lam_re[lay], s5_lam_im[lay], s5_log_dt[lay], s5_b_re[lay], s5_b_im[lay],
                                  s5_c_re[lay], s5_c_im[lay])
    y_groups = _s5_call(_to_group_major(s_ctx), _to_group_major(s_lat), w1, wso, lam_tab, nb=bsz)
    y_s5 = _to_token_major(y_groups, bsz)

    attn = _attn_call(q_lat, k_ctx, k_lat, v_ctx, v_lat, tq=min(ATTN_TOKENS, seq))

    row = lambda v: v.astype(F32).reshape(1, -1)
    wss = jnp.concatenate([w_sh_gate[lay], w_sh_up[lay]], axis=1).astype(BF16)
    x1, hp, shared, scores_t = _mixout_call(
        y_s5, s_lat, attn, gs_lat, ga_lat, x, g1, sh2, sc2, row(s5_d[lay]),
        w_glu[lay].astype(BF16), w_branch_ssm[lay].astype(BF16), w_branch_attn[lay].astype(BF16),
        w_out[lay].astype(BF16), row(ln1_g[lay]), row(ln1_b[lay]), w_router[lay].T.astype(BF16), wss,
        w_sh_down[lay].astype(BF16), tm=min(MIXOUT_TOKENS, seq), alpha=alpha)

    tokens = bsz * seq
    bm = MOE_BLOCK
    idx, gates, rank, counts_f = _route_call(scores_t, router_bias[lay], tm=min(ROUTE_TOKENS, seq))
    counts = counts_f.reshape(n_exp).astype(jnp.int32)
    padded = (counts + bm - 1) // bm * bm
    pad_end = jnp.cumsum(padded)
    pad_start = pad_end - padded
    n_blocks = (tokens * TOP_K + n_exp * (bm - 1) + bm - 1) // bm
    block_start = jnp.arange(n_blocks, dtype=jnp.int32) * bm
    block_expert = jnp.minimum(jnp.sum((pad_end[None, :] <= block_start[:, None]).astype(jnp.int32), axis=1), n_exp - 1)
    n_active = (pad_end[-1] // bm).reshape(1)
    pos = _slot_call(idx, rank, pad_start, tm=min(ROUTE_TOKENS, tokens))
    pieces = d // 2 // LANES
    xs = _dispatch_call(pad_start, padded, pos, hp.reshape(tokens * pieces, LANES), n_blocks * bm,
                        td=min(DISPATCH_TOKENS, tokens), bm=bm, pieces=pieces)
    y_slots = _experts_call(block_expert, n_active, xs, w_exp_gate[lay], w_exp_up[lay], w_exp_down[lay])

    tc = min(COMBINE_TOKENS, seq)
    out = _combine_call(pos, y_slots, gates, shared.reshape(tokens, d), x1.reshape(tokens, d), g2,
                        row(ln2_g[lay]), row(ln2_b[lay]), alpha=alpha, tc=tc, tiles_per_batch=seq // tc)
    return out.reshape(bsz, seq, d)
```

```python
import functools
import math

import jax
import jax.numpy as jnp
from jax import lax
from jax.experimental import pallas as pl
from jax.experimental.pallas import tpu as pltpu

F32 = jnp.float32
BF16 = jnp.bfloat16

GRID_W = 64
HEAD_DIM = 64
N_KV_HEADS = 4
S5_GROUP_CH = 16
S5_MAX_RE = -1e-4
ROPE_THETA = 10000.0
TOP_K = 8
N_EXPERT_GROUPS = 8
TOPK_GROUPS = 4
ROUTED_SCALE = 2.5
LN_EPS = 1e-5
MOD_EPS = 1e-6
RMS_EPS = 1e-6
N_MOD = 6

LANES = 128
SUBLANES = 8
VMEM_LIMIT_BYTES = 56 * 1024 * 1024

INPROJ_TOKENS = 512
INPROJ_CTX_TOKENS = 256
ATTN_TOKENS = 2048
MIXOUT_TOKENS = 512
S5_CHUNK = 16
S5_COLS = S5_CHUNK * S5_GROUP_CH
MOE_BLOCK = 1024
ROUTE_TOKENS = 512
DISPATCH_TOKENS = 128
DISPATCH_BUFFERS = 3
COMBINE_TOKENS = 128


def _params(sem):
    return pltpu.CompilerParams(dimension_semantics=sem, vmem_limit_bytes=VMEM_LIMIT_BYTES)


def _const_spec(shape):
    nd = len(shape)
    return pl.BlockSpec(shape, lambda *_: (0,) * nd, pipeline_mode=pl.Buffered(1))


def _dot(a, b):
    return jnp.dot(a, b, preferred_element_type=F32)


def _norm_rows(x, eps):
    mu = jnp.mean(x, axis=-1, keepdims=True)
    xc = x - mu
    var = jnp.mean(xc * xc, axis=-1, keepdims=True)
    return xc * lax.rsqrt(var + eps)


def _silu(x):
    return x * jax.nn.sigmoid(x)


def _gelu_tanh(x):
    return 0.5 * x * (1.0 + jnp.tanh(math.sqrt(2.0 / math.pi) * (x + 0.044715 * (x * x * x))))


def _pack_bf16_pairs(v):
    n = v.shape[1] // 2
    bits = lax.bitcast_convert_type(v.astype(BF16).astype(F32), jnp.uint32)
    return bits[:, :n] | (bits[:, n:] >> 16)


def _store_row_pieces(ref, v):
    rows, width = v.shape
    n = width // LANES
    for j in range(n):
        ref[pl.ds(j, rows, stride=n), :] = v[:, j * LANES:(j + 1) * LANES]


def _load_row_pieces(ref, first_row, rows, n):
    return jnp.concatenate([ref[pl.ds(first_row * n + j, rows, stride=n), :] for j in range(n)], axis=1)


def _unpack_bf16_pairs(w):
    hi = lax.bitcast_convert_type(w & jnp.uint32(0xFFFF0000), F32)
    lo = lax.bitcast_convert_type(w << 16, F32)
    return hi, lo


def _mod_kernel(c_ref, w_ref, b_ref, o_ref):
    o_ref[...] = _dot(_silu(c_ref[...]), w_ref[...]) + b_ref[...]


def _mod_call(c_all, w_mod, b_mod):
    rows, d = c_all.shape
    n = w_mod.shape[1]
    tn = d
    return pl.pallas_call(
        _mod_kernel,
        out_shape=jax.ShapeDtypeStruct((rows, n), F32),
        grid=(n // tn,),
        in_specs=[pl.BlockSpec((rows, d), lambda j: (0, 0)),
                  pl.BlockSpec((d, tn), lambda j: (0, j)),
                  pl.BlockSpec((1, tn), lambda j: (0, j))],
        out_specs=pl.BlockSpec((rows, tn), lambda j: (0, j)),
        compiler_params=_params(("arbitrary",)),
        name="mod",
    )(c_all, w_mod, b_mod.reshape(1, n))


def _swap16(t):
    width = t.shape[1]
    lane = lax.broadcasted_iota(jnp.int32, t.shape, 1)
    first = (lane & 16) == 0
    return jnp.where(first, pltpu.roll(t, width - 16, 1), pltpu.roll(t, 16, 1))


def _rms_rope(t, bd_ref, ta, tb):
    msq = _dot((t * t).astype(BF16), bd_ref[...])
    return lax.rsqrt(msq + RMS_EPS) * (t * ta + _swap16(t) * tb)


def _store_padded_heads(t, o_ref, ones_lane=False):
    rows = t.shape[0]
    lane = lax.broadcasted_iota(jnp.int32, (rows, LANES), 1)
    lo = lane < HEAD_DIM
    fill_hi = jnp.where(lane == HEAD_DIM, 1.0, 0.0) if ones_lane else 0.0
    fill_lo = jnp.where(lane == 0, 1.0, 0.0) if ones_lane else 0.0
    for j in range(t.shape[1] // LANES):
        slab = t[:, j * LANES:(j + 1) * LANES]
        swapped = pltpu.roll(slab, HEAD_DIM, 1)
        pieces = (jnp.where(lo, slab, fill_hi), jnp.where(lo, fill_lo, swapped),
                  jnp.where(lo, swapped, fill_hi), jnp.where(lo, fill_lo, slab))
        for p, piece in enumerate(pieces):
            c0 = (4 * j + p) * LANES
            o_ref[:, c0:c0 + LANES] = piece.astype(o_ref.dtype)


def _inproj_kernel(x_ref, sh_ref, sc_ref, w_ref, bd_ref, tka_ref, tkb_ref, *rest, d, kvw, has_q):
    if has_q:
        tqa_ref, tqb_ref, s_ref, k_ref, v_ref, q_ref, gs_ref, ga_ref = rest
    else:
        s_ref, k_ref, v_ref = rest
    u = (_norm_rows(x_ref[...], MOD_EPS) * (1.0 + sc_ref[...]) + sh_ref[...]).astype(BF16)
    col = 0
    s_ref[...] = _dot(u, w_ref[:, col:col + d]).astype(s_ref.dtype)
    col += d
    if has_q:
        for c in range(d // kvw):
            q = _dot(u, w_ref[:, col + c * kvw:col + (c + 1) * kvw])
            q_ref[:, c * kvw:(c + 1) * kvw] = _rms_rope(q, bd_ref, tqa_ref[...], tqb_ref[...]).astype(q_ref.dtype)
        col += d
    k = _dot(u, w_ref[:, col:col + kvw])
    _store_padded_heads(_rms_rope(k, bd_ref, tka_ref[...], tkb_ref[...]), k_ref)
    col += kvw
    _store_padded_heads(_dot(u, w_ref[:, col:col + kvw]), v_ref, ones_lane=True)
    col += kvw
    if has_q:
        gs_ref[...] = jax.nn.sigmoid(_dot(u, w_ref[:, col:col + d])).astype(gs_ref.dtype)
        col += d
        ga_ref[...] = jax.nn.sigmoid(_dot(u, w_ref[:, col:col + d])).astype(ga_ref.dtype)


def _inproj_call(x, shift, scale, w, bd, tka, tkb, tqa=None, tqb=None, *, tm):
    bsz, length, d = x.shape
    has_q = tqa is not None
    kvw = N_KV_HEADS * HEAD_DIM
    per_batch = shift.shape[0] > 1
    tab_rows = tka.shape[0]
    tab_blk = tm if tab_rows > 1 else 1
    mod_spec = pl.BlockSpec((None, 1, d), (lambda b, i: (b, 0, 0)) if per_batch else (lambda b, i: (0, 0, 0)))
    tab_spec = pl.BlockSpec((tab_blk, kvw), (lambda b, i: (i, 0)) if tab_rows > 1 else (lambda b, i: (0, 0)))
    row_spec = lambda width: pl.BlockSpec((None, tm, width), lambda b, i: (b, i, 0))
    in_specs = [row_spec(d), mod_spec, mod_spec, _const_spec(w.shape), _const_spec(bd.shape), tab_spec, tab_spec]
    args = [x, shift, scale, w, bd, tka, tkb]
    widths = [d, 4 * kvw, 4 * kvw]
    if has_q:
        in_specs += [tab_spec, tab_spec]
        args += [tqa, tqb]
        widths += [d, d, d]
    return pl.pallas_call(
        functools.partial(_inproj_kernel, d=d, kvw=kvw, has_q=has_q),
        out_shape=[jax.ShapeDtypeStruct((bsz, length, wd), BF16) for wd in widths],
        grid=(bsz, length // tm),
        in_specs=in_specs,
        out_specs=[row_spec(wd) for wd in widths],
        compiler_params=_params(("parallel", "parallel")),
        name="inproj_lat" if has_q else "inproj_ctx",
    )(*args)


def _rope_tables(seq, gain, scale):
    half = HEAD_DIM // 2
    inv_freq = ROPE_THETA ** (-jnp.arange(0, half, 2, dtype=F32) / half)
    t = jnp.arange(seq, dtype=jnp.int32)
    pos = jnp.stack([(t // GRID_W).astype(F32), (t % GRID_W).astype(F32)], axis=1)
    dim = jnp.arange(HEAD_DIM)
    axis = dim // half
    second = ((dim % half) // (half // 2)) == 1
    freq = inv_freq[dim % (half // 2)]
    ang = pos[:, axis] * freq[None, :]
    partner = jnp.where(second, dim - half // 2, dim + half // 2)
    g = gain.astype(F32)
    ta = jnp.cos(ang) * g[None, :] * scale
    tb = jnp.sin(ang) * jnp.where(second, 1.0, -1.0)[None, :] * g[partner][None, :] * scale
    return jnp.tile(ta, (1, N_KV_HEADS)), jnp.tile(tb, (1, N_KV_HEADS))


def _attn_kernel(q_ref, kc_ref, kl_ref, vc_ref, vl_ref, o_ref):
    nt = (((1,), (1,)), ((), ()))
    rows = q_ref.shape[0]
    lane = lax.broadcasted_iota(jnp.int32, (rows, LANES), 1)
    for j in range(q_ref.shape[1] // LANES):
        qs = q_ref[:, j * LANES:(j + 1) * LANES]
        acc = []
        for half in range(2):
            cols = slice(half * LANES, (half + 1) * LANES)
            s_c = lax.dot_general(qs, kc_ref[:, cols], nt, preferred_element_type=F32)
            s_l = lax.dot_general(qs, kl_ref[:, cols], nt, preferred_element_type=F32)
            m = jnp.maximum(jnp.max(s_c, axis=-1, keepdims=True), jnp.max(s_l, axis=-1, keepdims=True))
            e_c = jnp.exp((s_c - m).astype(BF16))
            e_l = jnp.exp((s_l - m).astype(BF16))
            acc.append(_dot(e_c, vc_ref[:, cols]) + _dot(e_l, vl_ref[:, cols]))
        out = jnp.where(lane < HEAD_DIM, acc[0] / acc[0][:, HEAD_DIM:HEAD_DIM + 1], acc[1] / acc[1][:, 0:1])
        o_ref[:, j * LANES:(j + 1) * LANES] = out.astype(o_ref.dtype)


def _attn_call(q, kc, kl, vc, vl, *, tq):
    bsz, seq, d = q.shape
    ctx = kc.shape[1]
    gw = d // N_KV_HEADS
    q_spec = pl.BlockSpec((None, tq, gw), lambda b, h, i: (b, i, h))
    kv_spec = lambda length: pl.BlockSpec((None, length, 2 * LANES), lambda b, h, i: (b, 0, h))
    return pl.pallas_call(
        _attn_kernel,
        out_shape=jax.ShapeDtypeStruct((bsz, seq, d), BF16),
        grid=(bsz, N_KV_HEADS, seq // tq),
        in_specs=[q_spec, kv_spec(ctx), kv_spec(seq), kv_spec(ctx), kv_spec(seq)],
        out_specs=q_spec,
        compiler_params=_params(("parallel", "parallel", "arbitrary")),
        name="attention",
    )(q, kc, kl, vc, vl)


def _s5_tables(lam_re, lam_im, log_dt, b_re, b_im, c_re, c_im):
    n = S5_CHUNK
    lam = lax.complex(jnp.minimum(lam_re.astype(F32), S5_MAX_RE), lam_im.astype(F32))
    lam_dt = lam * jnp.exp(log_dt.astype(F32))[..., None]
    b_bar = ((jnp.exp(lam_dt) - 1.0) / lam)[..., None] * lax.complex(b_re.astype(F32), b_im.astype(F32))
    c_mat = lax.complex(c_re.astype(F32), c_im.astype(F32))
    pw = jnp.exp(lam_dt[None] * jnp.arange(n + 1, dtype=F32)[:, None, None, None])
    kern = jnp.real(jnp.einsum('dgcp,jdgp,dgpe->djgce', c_mat, pw[:n], b_bar))
    s_idx = jnp.arange(n)[:, None]
    t_idx = jnp.arange(n)[None, :]

    def toeplitz(k, lag):
        onehot = (lag[None] == jnp.arange(n)[:, None, None]).astype(F32)
        return jnp.einsum('jst,jgce->gsetc', onehot, k, precision=lax.Precision.HIGHEST)

    toep = toeplitz(kern[0], t_idx - s_idx) + toeplitz(kern[1], s_idx - t_idx)
    groups = toep.shape[0]
    toep = toep.reshape(groups, S5_COLS, S5_COLS)
    wis_f = jnp.einsum('sgp,gpe->gsep', pw[n - 1 - jnp.arange(n), 0], b_bar[0]).reshape(groups, S5_COLS, -1)
    wis_r = jnp.einsum('sgp,gpe->gsep', pw[jnp.arange(n), 1], b_bar[1]).reshape(groups, S5_COLS, -1)
    w1 = jnp.concatenate([toep, jnp.real(wis_f), jnp.real(wis_r), jnp.imag(wis_f), jnp.imag(wis_r)], axis=-1)
    m_f = jnp.einsum('gcp,tgp->gptc', c_mat[0], pw[1 + jnp.arange(n), 0]).reshape(groups, -1, S5_COLS)
    m_r = jnp.einsum('gcp,tgp->gptc', c_mat[1], pw[n - jnp.arange(n), 1]).reshape(groups, -1, S5_COLS)
    wso = jnp.concatenate([jnp.real(m_f), jnp.real(m_r), -jnp.imag(m_f), -jnp.imag(m_r)], axis=1)
    lam_n = jnp.concatenate([pw[n, 0], pw[n, 1]], axis=-1)
    lam_tab = jnp.stack([jnp.real(lam_n), jnp.imag(lam_n)], axis=1)
    return w1.astype(BF16), wso.astype(BF16), lam_tab


def _s5_kernel(uc_ref, ul_ref, w1_ref, wso_ref, lam_ref, y_ref, a_ref, xf_re, xf_im, xr_re, xr_im,
               *, nb, nc_ctx, nc):
    cols = S5_COLS
    half = lam_ref.shape[1] // 2
    a_ref[:nc_ctx * nb, :] = _dot(uc_ref[...], w1_ref[...])
    a_ref[nc_ctx * nb:, :] = _dot(ul_ref[...], w1_ref[...])
    l_re = lam_ref[0:1, :]
    l_im = lam_ref[1:2, :]
    lo = lax.broadcasted_iota(jnp.int32, (nb, 2 * half), 1) < half

    def step(i, carry):
        x_re, x_im = carry
        i_rev = jnp.where(i < nc_ctx, nc_ctx - 1 - i, nc - 1 + nc_ctx - i)
        rf = pl.ds(pl.multiple_of(i * nb, nb), nb)
        rr = pl.ds(pl.multiple_of(i_rev * nb, nb), nb)
        xf_re[rf, :] = x_re
        xf_im[rf, :] = x_im
        xr_re[rr, :] = x_re
        xr_im[rr, :] = x_im
        s_re = jnp.where(lo, a_ref[rf, cols:cols + 2 * half], a_ref[rr, cols:cols + 2 * half])
        s_im = jnp.where(lo, a_ref[rf, cols + 2 * half:cols + 4 * half], a_ref[rr, cols + 2 * half:cols + 4 * half])
        return l_re * x_re - l_im * x_im + s_re, l_re * x_im + l_im * x_re + s_im

    zero = jnp.zeros((nb, 2 * half), F32)
    lax.fori_loop(0, nc, step, (zero, zero))
    r0 = nc_ctx * nb
    rows = (nc - nc_ctx) * nb
    lo_all = lax.broadcasted_iota(jnp.int32, (rows, 2 * half), 1) < half
    y = a_ref[r0:, 0:cols]
    for k, (f_ref, r_ref) in enumerate(((xf_re, xr_re), (xf_im, xr_im))):
        st = jnp.where(lo_all, f_ref[r0:, :], r_ref[r0:, :])
        hi = st.astype(BF16)
        lo_part = (st - hi.astype(F32)).astype(BF16)
        w = wso_ref[k * 2 * half:(k + 1) * 2 * half, :]
        y = y + _dot(hi, w) + _dot(lo_part, w)
    y_ref[...] = y.astype(y_ref.dtype)


def _s5_call(u_ctx, u_lat, w1, wso, lam_tab, *, nb):
    groups, out_rows, cols = u_lat.shape
    ctx_rows = u_ctx.shape[1]
    rows = ctx_rows + out_rows
    nc = rows // nb
    nc_ctx = ctx_rows // nb
    st = lam_tab.shape[2]
    return pl.pallas_call(
        functools.partial(_s5_kernel, nb=nb, nc_ctx=nc_ctx, nc=nc),
        out_shape=jax.ShapeDtypeStruct((groups, out_rows, cols), BF16),
        grid=(groups,),
        in_specs=[pl.BlockSpec((None, ctx_rows, cols), lambda g: (g, 0, 0)),
                  pl.BlockSpec((None, out_rows, cols), lambda g: (g, 0, 0)),
                  pl.BlockSpec((None,) + w1.shape[1:], lambda g: (g, 0, 0)),
                  pl.BlockSpec((None,) + wso.shape[1:], lambda g: (g, 0, 0)),
                  pl.BlockSpec((None, 2, st), lambda g: (g, 0, 0))],
        out_specs=pl.BlockSpec((None, out_rows, cols), lambda g: (g, 0, 0)),
        scratch_shapes=[pltpu.VMEM((rows, w1.shape[2]), F32)] + [pltpu.VMEM((rows, st), F32)] * 4,
        compiler_params=_params(("parallel",)),
        name="s5",
    )(u_ctx, u_lat, w1, wso, lam_tab)


def _to_group_major(s):
    bsz, length, width = s.shape
    groups = width // S5_GROUP_CH
    nc = length // S5_CHUNK
    t = s.reshape(bsz, nc, S5_CHUNK, groups, S5_GROUP_CH).transpose(3, 1, 0, 2, 4)
    return t.reshape(groups, nc * bsz, S5_COLS)


def _to_token_major(y, bsz):
    groups, rows, _ = y.shape
    nc = rows // bsz
    t = y.reshape(groups, nc, bsz, S5_CHUNK, S5_GROUP_CH).transpose(2, 1, 3, 0, 4)
    return t.reshape(bsz, nc * S5_CHUNK, groups * S5_GROUP_CH)


def _mixout_kernel(y_ref, s_ref, at_ref, gs_ref, ga_ref, x_ref, g1_ref, sh2_ref, sc2_ref, dsk_ref,
                   wglu_ref, wbs_ref, wba_ref, wout_ref, lng_ref, lnb_ref, wrt_ref, wss_ref, wsd_ref,
                   x1_ref, h_ref, shared_ref, scores_ref, *, alpha):
    y = y_ref[...].astype(F32) + s_ref[...].astype(F32) * dsk_ref[...]
    z = _gelu_tanh(y)
    ssm = z * jax.nn.sigmoid(_dot(z.astype(BF16), wglu_ref[...]))
    merged = (gs_ref[...].astype(F32) * _dot(ssm.astype(BF16), wbs_ref[...])
              + ga_ref[...].astype(F32) * _dot(at_ref[...], wba_ref[...]))
    y_mix = _dot(merged.astype(BF16), wout_ref[...])
    x1 = _norm_rows(alpha * x_ref[...] + g1_ref[...] * y_mix, LN_EPS) * lng_ref[...] + lnb_ref[...]
    x1_ref[...] = x1
    h = _norm_rows(x1, MOD_EPS) * (1.0 + sc2_ref[...]) + sh2_ref[...]
    _store_row_pieces(h_ref, _pack_bf16_pairs(h))
    hb = h.astype(BF16)
    scores_ref[...] = jax.nn.sigmoid(
        lax.dot_general(wrt_ref[...], hb, (((1,), (1,)), ((), ())), preferred_element_type=F32))
    ss = _dot(hb, wss_ref[...])
    sh_hidden = wsd_ref.shape[0]
    hid = _silu(ss[:, :sh_hidden]) * ss[:, sh_hidden:]
    shared_ref[...] = _dot(hid.astype(BF16), wsd_ref[...])


def _mixout_call(y, s, attn, gs, ga, x, g1, sh2, sc2, dsk, wglu, wbs, wba, wout, lng, lnb, wrt, wss, wsd,
                 *, tm, alpha):
    bsz, seq, d = x.shape
    n_exp = wrt.shape[0]
    pieces = d // 2 // LANES
    row = lambda width: pl.BlockSpec((None, tm, width), lambda b, i: (b, i, 0))
    mod = pl.BlockSpec((None, 1, d), lambda b, i: (b, 0, 0))
    consts = [dsk, wglu, wbs, wba, wout, lng, lnb, wrt, wss, wsd]
    return pl.pallas_call(
        functools.partial(_mixout_kernel, alpha=alpha),
        out_shape=[jax.ShapeDtypeStruct((bsz, seq, d), F32),
                   jax.ShapeDtypeStruct((bsz, seq * pieces, LANES), jnp.uint32),
                   jax.ShapeDtypeStruct((bsz, seq, d), F32), jax.ShapeDtypeStruct((bsz, n_exp, seq), F32)],
        grid=(bsz, seq // tm),
        in_specs=[row(d)] * 6 + [mod] * 3 + [_const_spec(a.shape) for a in consts],
        out_specs=[row(d), pl.BlockSpec((None, tm * pieces, LANES), lambda b, i: (b, i, 0)),
                   row(d), pl.BlockSpec((None, n_exp, tm), lambda b, i: (b, 0, i))],
        compiler_params=_params(("parallel", "parallel")),
        name="mixout",
    )(y, s, attn, gs, ga, x, g1, sh2, sc2, *consts)


def _stack_rows(rows):
    n = rows[0].shape[1]
    row_k = lax.broadcasted_iota(jnp.int32, (len(rows), n), 0)
    out = jnp.zeros((len(rows), n), rows[0].dtype)
    for k, r in enumerate(rows):
        out = jnp.where(row_k == k, r, out)
    return out


def _route_kernel(s_ref, bias_ref, utri_ref, idx_ref, gate_ref, rank_ref, counts_ref, base_ref):
    @pl.when((pl.program_id(0) == 0) & (pl.program_id(1) == 0))
    def _():
        base_ref[...] = jnp.zeros_like(base_ref)

    s = s_ref[...]
    n_exp, tm = s.shape
    per_group = n_exp // N_EXPERT_GROUPS
    neg = -jnp.inf
    b = s + bias_ref[...]
    bg = b.reshape(N_EXPERT_GROUPS, per_group, tm)
    m1 = jnp.max(bg, axis=1, keepdims=True)
    is1 = bg == m1
    n1 = jnp.sum(jnp.where(is1, 1.0, 0.0), axis=1, keepdims=True)
    m2 = jnp.max(jnp.where(is1, neg, bg), axis=1, keepdims=True)
    gscore = m1 + jnp.where(n1 >= 2.0, m1, m2)
    gs = [gscore[g] for g in range(N_EXPERT_GROUPS)]
    kept = []
    for g in range(N_EXPERT_GROUPS):
        beaten = jnp.zeros((1, tm), F32)
        for o in range(N_EXPERT_GROUPS):
            if o != g:
                wins = (gs[o] >= gs[g]) if o < g else (gs[o] > gs[g])
                beaten = beaten + jnp.where(wins, 1.0, 0.0)
        kept.append(jnp.where(beaten < float(TOPK_GROUPS), bg[g], neg))
    masked = jnp.concatenate(kept, axis=0)
    row_f = lax.broadcasted_iota(jnp.int32, (n_exp, tm), 0).astype(F32)
    chosen = jnp.zeros((n_exp, tm), F32)
    firsts, vals = [], []
    for _ in range(TOP_K):
        m = jnp.max(masked, axis=0, keepdims=True)
        first = jnp.min(jnp.where(masked == m, row_f, float(n_exp)), axis=0, keepdims=True)
        sel = row_f == first
        firsts.append(first)
        vals.append(jnp.sum(jnp.where(sel, s, 0.0), axis=0, keepdims=True))
        chosen = jnp.where(sel, 1.0, chosen)
        masked = jnp.where(sel, neg, masked)
    before = _dot(chosen.astype(BF16), utri_ref[...]) + base_ref[...]
    ranks = [jnp.sum(jnp.where(row_f == f, before, 0.0), axis=0, keepdims=True) for f in firsts]
    idx_ref[...] = _stack_rows(firsts).astype(jnp.int32)
    rank_ref[...] = _stack_rows(ranks).astype(jnp.int32)
    val = _stack_rows(vals)
    gate_ref[...] = val / jnp.sum(val, axis=0, keepdims=True) * ROUTED_SCALE
    base_ref[...] = base_ref[...] + jnp.sum(chosen, axis=1, keepdims=True)
    counts_ref[...] = base_ref[...]


def _route_call(scores_t, bias, *, tm):
    bsz, n_exp, seq = scores_t.shape
    tiles = seq // tm
    tokens = bsz * seq
    r_id = lax.broadcasted_iota(jnp.int32, (tm, tm), 0)
    c_id = lax.broadcasted_iota(jnp.int32, (tm, tm), 1)
    utri = (r_id < c_id).astype(BF16)
    small = pl.BlockSpec((TOP_K, tm), lambda b, i: (0, b * tiles + i))
    col = pl.BlockSpec((n_exp, 1), lambda b, i: (0, 0))
    return pl.pallas_call(
        _route_kernel,
        out_shape=[jax.ShapeDtypeStruct((TOP_K, tokens), jnp.int32), jax.ShapeDtypeStruct((TOP_K, tokens), F32),
                   jax.ShapeDtypeStruct((TOP_K, tokens), jnp.int32), jax.ShapeDtypeStruct((n_exp, 1), F32)],
        grid=(bsz, tiles),
        in_specs=[pl.BlockSpec((None, n_exp, tm), lambda b, i: (b, 0, i)), col, _const_spec((tm, tm))],
        out_specs=[small, small, small, col],
        scratch_shapes=[pltpu.VMEM((n_exp, 1), F32)],
        compiler_params=_params(("arbitrary", "arbitrary")),
        name="route",
    )(scores_t, bias.astype(F32).reshape(n_exp, 1), utri)


def _slot_kernel(idx_ref, rank_ref, start_ref, pos_ref):
    idx = idx_ref[...]
    tm = idx.shape[1]
    n_exp = start_ref.shape[0]
    row = lax.broadcasted_iota(jnp.int32, (n_exp, tm), 0)
    starts = [jnp.sum(jnp.where(row == idx[k:k + 1, :], start_ref[...], 0.0), axis=0, keepdims=True)
              for k in range(TOP_K)]
    pos_ref[...] = rank_ref[...] + _stack_rows(starts).astype(jnp.int32)


def _slot_call(idx, rank, pad_start, *, tm):
    tokens = idx.shape[1]
    n_exp = pad_start.shape[0]
    small = pl.BlockSpec((TOP_K, tm), lambda i: (0, i))
    return pl.pallas_call(
        _slot_kernel,
        out_shape=jax.ShapeDtypeStruct((TOP_K, tokens), jnp.int32),
        grid=(tokens // tm,),
        in_specs=[small, small, _const_spec((n_exp, 1))],
        out_specs=small,
        compiler_params=_params(("parallel",)),
        name="slots",
    )(idx, rank, pad_start.astype(F32).reshape(n_exp, 1))


def _dispatch_kernel(start_ref, padded_ref, pos_ref, h_hbm, xs_hbm, hbuf, zblock, lsem, ssem, zsem, *, pieces):
    i = pl.program_id(0)
    n = pl.num_programs(0)
    nbuf = hbuf.shape[0]
    td = hbuf.shape[1] // pieces
    bm = zblock.shape[0] // pieces
    n_exp = start_ref.shape[0]
    slot = i % nbuf

    def zero_copy(first_row):
        return pltpu.make_async_copy(zblock, xs_hbm.at[pl.ds(first_row * pieces, bm * pieces), :], zsem)

    def load(step, s):
        return pltpu.make_async_copy(h_hbm.at[pl.ds(step * td * pieces, td * pieces), :], hbuf.at[s], lsem.at[s])

    def wait_scatters(s):
        rows = td * TOP_K * pieces
        pltpu.make_async_copy(h_hbm.at[pl.ds(0, rows), :], xs_hbm.at[pl.ds(0, rows), :], ssem.at[s]).wait()

    @pl.when(i == 0)
    def _():
        zblock[...] = jnp.zeros_like(zblock)

        def per_expert(e, carry):
            @pl.when(padded_ref[e] > 0)
            def _():
                zero_copy(start_ref[e] + padded_ref[e] - bm).start()
            return carry
        lax.fori_loop(0, n_exp, per_expert, 0)

        def per_expert_wait(e, carry):
            @pl.when(padded_ref[e] > 0)
            def _():
                zero_copy(0).wait()
            return carry
        lax.fori_loop(0, n_exp, per_expert_wait, 0)
        load(0, 0).start()

    @pl.when(i >= nbuf - 1)
    def _():
        wait_scatters((i + 1) % nbuf)

    @pl.when(i + 1 < n)
    def _():
        load(i + 1, (i + 1) % nbuf).start()

    load(i, slot).wait()

    for t in range(td):
        for k in range(TOP_K):
            dst = pl.multiple_of(pos_ref[k, t] * pieces, pieces)
            pltpu.make_async_copy(hbuf.at[slot, pl.ds(t * pieces, pieces), :], xs_hbm.at[pl.ds(dst, pieces), :],
                                  ssem.at[slot]).start(priority=k % 2)

    @pl.when(i == n - 1)
    def _():
        for back in range(nbuf - 1):
            @pl.when(i >= back)
            def _():
                wait_scatters((i - back) % nbuf)


def _dispatch_call(pad_start, padded, pos, hp, n_slots, *, td, bm, pieces):
    tokens = hp.shape[0] // pieces
    grid_spec = pltpu.PrefetchScalarGridSpec(
        num_scalar_prefetch=2,
        grid=(tokens // td,),
        in_specs=[pl.BlockSpec((TOP_K, td), lambda i, *_: (0, i), memory_space=pltpu.SMEM),
                  pl.BlockSpec(memory_space=pl.ANY)],
        out_specs=pl.BlockSpec(memory_space=pl.ANY),
        scratch_shapes=[pltpu.VMEM((DISPATCH_BUFFERS, td * pieces, LANES), hp.dtype),
                        pltpu.VMEM((bm * pieces, LANES), hp.dtype),
                        pltpu.SemaphoreType.DMA((DISPATCH_BUFFERS,)), pltpu.SemaphoreType.DMA((DISPATCH_BUFFERS,)),
                        pltpu.SemaphoreType.DMA],
    )
    return pl.pallas_call(
        functools.partial(_dispatch_kernel, pieces=pieces),
        out_shape=jax.ShapeDtypeStruct((n_slots * pieces, LANES), hp.dtype),
        grid_spec=grid_spec,
        compiler_params=_params(("arbitrary",)),
        name="dispatch",
    )(pad_start, padded, pos, hp)


def _experts_kernel(be_ref, nact_ref, x_ref, wg_ref, wu_ref, wd_ref, y_ref, wg_s, wu_s, wd_s):
    i = pl.program_id(0)
    last = nact_ref[0] - 1
    expert = be_ref[jnp.minimum(i, last)]
    previous = be_ref[jnp.minimum(jnp.maximum(i, 1) - 1, last)]

    @pl.when((i == 0) | (expert != previous))
    def _():
        wg_s[...] = wg_ref[...].astype(BF16)
        wu_s[...] = wu_ref[...].astype(BF16)
        wd_s[...] = wd_ref[...].astype(BF16)

    @pl.when(i <= last)
    def _():
        half = wg_s.shape[0] // 2
        pieces = half // LANES
        bm = x_ref.shape[0] // pieces
        x_hi, x_lo = (v.astype(BF16) for v in _unpack_bf16_pairs(_load_row_pieces(x_ref, 0, bm, pieces)))
        hg = _dot(x_hi, wg_s[:half, :]) + _dot(x_lo, wg_s[half:, :])
        hu = _dot(x_hi, wu_s[:half, :]) + _dot(x_lo, wu_s[half:, :])
        _store_row_pieces(y_ref, _pack_bf16_pairs(_dot((_silu(hg) * hu).astype(BF16), wd_s[...])))


def _experts_call(block_expert, n_active, xs, wg, wu, wd):
    n_blocks = block_expert.shape[0]
    block_rows = xs.shape[0] // n_blocks
    d, hidden = wg.shape[1:]
    blk = lambda i, be, na: (jnp.minimum(i, na[0] - 1), 0)
    wsel = lambda i, be, na: (be[jnp.minimum(i, na[0] - 1)], 0, 0)
    grid_spec = pltpu.PrefetchScalarGridSpec(
        num_scalar_prefetch=2,
        grid=(n_blocks,),
        in_specs=[pl.BlockSpec((block_rows, LANES), blk),
                  pl.BlockSpec((None, d, hidden), wsel),
                  pl.BlockSpec((None, d, hidden), wsel),
                  pl.BlockSpec((None, hidden, d), wsel)],
        out_specs=pl.BlockSpec((block_rows, LANES), blk),
        scratch_shapes=[pltpu.VMEM((d, hidden), BF16), pltpu.VMEM((d, hidden), BF16), pltpu.VMEM((hidden, d), BF16)],
    )
    return pl.pallas_call(
        _experts_kernel,
        out_shape=jax.ShapeDtypeStruct(xs.shape, xs.dtype),
        grid_spec=grid_spec,
        compiler_params=_params(("arbitrary",)),
        name="experts",
    )(block_expert, n_active, xs, wg, wu, wd)


def _row_gather_start(idx_ref, n_rows, pieces, src_hbm, dst_ref, sem):
    tc = idx_ref.shape[1]
    for r in range(n_rows):
        src = pl.multiple_of(idx_ref[r // tc, r % tc] * pieces, pieces)
        pltpu.make_async_copy(src_hbm.at[pl.ds(src, pieces), :], dst_ref.at[pl.ds(r * pieces, pieces), :],
                              sem).start(priority=r % 2)


def _row_gather_wait(src_hbm, dst_ref, sem):
    pltpu.make_async_copy(src_hbm.at[pl.ds(0, dst_ref.shape[0]), :], dst_ref, sem).wait()


def _combine_kernel(pos_ref, pos1_ref, pos2_ref, y_hbm, gate_ref, shared_ref, x1_ref, g2_ref, lng_ref, lnb_ref,
                    o_ref, buf_a, buf_b, buf_c, sems, *, alpha):
    i = pl.program_id(0)
    n = pl.num_programs(0)
    tc, d = o_ref.shape
    half = d // 2
    pieces = half // LANES
    rows = buf_a.shape[0] // pieces
    bufs = (buf_a, buf_b, buf_c)

    @pl.when(i == 0)
    def _():
        _row_gather_start(pos_ref, rows, pieces, y_hbm, buf_a, sems.at[0])
        _row_gather_start(pos1_ref, rows, pieces, y_hbm, buf_b, sems.at[1])

    def step(r):
        cur, nxt, far = bufs[r], bufs[(r + 1) % 3], bufs[(r + 2) % 3]
        _row_gather_wait(y_hbm, cur, sems.at[r])
        _row_gather_start(pos2_ref, rows, pieces, y_hbm, far, sems.at[(r + 2) % 3])
        r_id = lax.broadcasted_iota(jnp.int32, (tc, tc), 0)
        c_id = lax.broadcasted_iota(jnp.int32, (tc, tc), 1)
        f_hi = shared_ref[:, :half]
        f_lo = shared_ref[:, half:]
        for k in range(rows // tc):
            y_hi, y_lo = _unpack_bf16_pairs(_load_row_pieces(cur, k * tc, tc, pieces))
            g = jnp.sum(jnp.where(r_id == c_id, gate_ref[k:k + 1, :], 0.0), axis=1, keepdims=True)
            f_hi = f_hi + g * y_hi
            f_lo = f_lo + g * y_lo
        f = jnp.concatenate([f_hi, f_lo], axis=1)
        o_ref[...] = _norm_rows(alpha * x1_ref[...] + g2_ref[...] * f, LN_EPS) * lng_ref[...] + lnb_ref[...]

        @pl.when(i == n - 1)
        def _():
            _row_gather_wait(y_hbm, nxt, sems.at[(r + 1) % 3])
            _row_gather_wait(y_hbm, far, sems.at[(r + 2) % 3])

    for r in range(3):
        @pl.when(i % 3 == r)
        def _():
            step(r)


def _combine_call(pos, y_slots, gates, shared, x1, g2, lng, lnb, *, alpha, tc, tiles_per_batch):
    tokens, d = x1.shape
    n_tiles = tokens // tc
    last = n_tiles - 1
    row = pl.BlockSpec((tc, d), lambda i: (i, 0))
    buf = pltpu.VMEM((TOP_K * tc * (d // 2 // LANES), LANES), y_slots.dtype)
    return pl.pallas_call(
        functools.partial(_combine_kernel, alpha=alpha),
        out_shape=jax.ShapeDtypeStruct((tokens, d), F32),
        grid=(n_tiles,),
        in_specs=[
            pl.BlockSpec((TOP_K, tc), lambda i: (0, i), memory_space=pltpu.SMEM),
            pl.BlockSpec((TOP_K, tc), lambda i: (0, jnp.minimum(i + 1, last)), memory_space=pltpu.SMEM),
            pl.BlockSpec((TOP_K, tc), lambda i: (0, jnp.minimum(i + 2, last)), memory_space=pltpu.SMEM),
            pl.BlockSpec(memory_space=pl.ANY),
            pl.BlockSpec((TOP_K, tc), lambda i: (0, i)),
            row, row,
            pl.BlockSpec((None, 1, d), lambda i: (i // tiles_per_batch, 0, 0)),
            _const_spec(lng.shape), _const_spec(lnb.shape),
        ],
        out_specs=row,
        scratch_shapes=[buf, buf, buf, pltpu.SemaphoreType.DMA((3,))],
        compiler_params=_params(("arbitrary",)),
        name="combine",
    )(pos, pos, pos, y_slots, gates, shared, x1, g2, lng, lnb)


def kernel(x, c, ctx, c_ctx, w_mod, b_mod, w_in, s5_lam_re, s5_lam_im, s5_log_dt, s5_b_re, s5_b_im, s5_c_re, s5_c_im, s5_d, w_glu, q_norm_g, k_norm_g, w_branch_ssm, w_branch_attn, w_out, ln1_g, ln1_b, w_router, router_bias, w_exp_gate, w_exp_up, w_exp_down, w_sh_gate, w_sh_up, w_sh_down, ln2_g, ln2_b):
    depth = w_mod.shape[0]
    assert depth == 1, "single-layer block: context outputs are never needed"
    bsz, seq, d = x.shape
    ctx_len = ctx.shape[1]
    kvw = N_KV_HEADS * HEAD_DIM
    n_exp = w_router.shape[2]
    alpha = (2.0 * depth) ** 0.25
    assert seq % GRID_W == 0 and seq % S5_CHUNK == 0 and ctx_len % S5_CHUNK == 0 and bsz % SUBLANES == 0
    lay = 0

    pad = (-(bsz + 1)) % SUBLANES
    c_all = jnp.concatenate([c, c_ctx[None, :], jnp.zeros((pad, d), F32)], axis=0)
    mod = _mod_call(c_all, w_mod[lay], b_mod[lay])
    mod_lat = mod[:bsz].reshape(bsz, N_MOD, 1, d)
    sh1, sc1, g1, sh2, sc2, g2 = (mod_lat[:, k] for k in range(N_MOD))
    mod_ctx = mod[bsz].reshape(N_MOD, 1, 1, d)

    w_in_l = w_in[lay].astype(BF16)
    w_ctx = jnp.concatenate([w_in_l[:, :d], w_in_l[:, 2 * d:2 * d + 2 * kvw]], axis=1)
    head_id = jnp.arange(kvw) // HEAD_DIM
    bd = jnp.where(head_id[:, None] == head_id[None, :], 1.0 / HEAD_DIM, 0.0).astype(BF16)
    tqa, tqb = _rope_tables(seq, q_norm_g[lay], HEAD_DIM ** -0.5)
    tka, tkb = _rope_tables(seq, k_norm_g[lay], 1.0)
    tca = jnp.tile(k_norm_g[lay].astype(F32), N_KV_HEADS)[None, :]
    s_lat, k_lat, v_lat, q_lat, gs_lat, ga_lat = _inproj_call(x, sh1, sc1, w_in_l, bd, tka, tkb, tqa, tqb,
                                                              tm=min(INPROJ_TOKENS, seq))
    s_ctx, k_ctx, v_ctx = _inproj_call(ctx, mod_ctx[0], mod_ctx[1], w_ctx, bd, tca, jnp.zeros_like(tca),
                                       tm=min(INPROJ_CTX_TOKENS, ctx_len))

    w1, wso, lam_tab = _s5_tables(s5_lam_re[lay], s5_lam_im[lay], s5_log_dt[lay], s5_b_re[lay], s5_b_im[lay],
                                  s5_c_re[lay], s5_c_im[lay])
    y_groups = _s5_call(_to_group_major(s_ctx), _to_group_major(s_lat), w1, wso, lam_tab, nb=bsz)
    y_s5 = _to_token_major(y_groups, bsz)

    attn = _attn_call(q_lat, k_ctx, k_lat, v_ctx, v_lat, tq=min(ATTN_TOKENS, seq))

    row = lambda v: v.astype(F32).reshape(1, -1)
    wss = jnp.concatenate([w_sh_gate[lay], w_sh_up[lay]], axis=1).astype(BF16)
    x1, hp, shared, scores_t = _mixout_call(
        y_s5, s_lat, attn, gs_lat, ga_lat, x, g1, sh2, sc2, row(s5_d[lay]),
        w_glu[lay].astype(BF16), w_branch_ssm[lay].astype(BF16), w_branch_attn[lay].astype(BF16),
        w_out[lay].astype(BF16), row(ln1_g[lay]), row(ln1_b[lay]), w_router[lay].T.astype(BF16), wss,
        w_sh_down[lay].astype(BF16), tm=min(MIXOUT_TOKENS, seq), alpha=alpha)

    tokens = bsz * seq
    bm = MOE_BLOCK
    idx, gates, rank, counts_f = _route_call(scores_t, router_bias[lay], tm=min(ROUTE_TOKENS, seq))
    counts = counts_f.reshape(n_exp).astype(jnp.int32)
    padded = (counts + bm - 1) // bm * bm
    pad_end = jnp.cumsum(padded)
    pad_start = pad_end - padded
    n_blocks = (tokens * TOP_K + n_exp * (bm - 1) + bm - 1) // bm
    block_start = jnp.arange(n_blocks, dtype=jnp.int32) * bm
    block_expert = jnp.minimum(jnp.sum((pad_end[None, :] <= block_start[:, None]).astype(jnp.int32), axis=1), n_exp - 1)
    n_active = (pad_end[-1] // bm).reshape(1)
    pos = _slot_call(idx, rank, pad_start, tm=min(ROUTE_TOKENS, tokens))
    pieces = d // 2 // LANES
    xs = _dispatch_call(pad_start, padded, pos, hp.reshape(tokens * pieces, LANES), n_blocks * bm,
                        td=min(DISPATCH_TOKENS, tokens), bm=bm, pieces=pieces)
    y_slots = _experts_call(block_expert, n_active, xs, w_exp_gate[lay], w_exp_up[lay], w_exp_down[lay])

    tc = min(COMBINE_TOKENS, seq)
    out = _combine_call(pos, y_slots, gates, shared.reshape(tokens, d), x1.reshape(tokens, d), g2,
                        row(ln2_g[lay]), row(ln2_b[lay]), alpha=alpha, tc=tc, tiles_per_batch=seq // tc)
    return out.reshape(bsz, seq, d)
```

```python
import functools
import math

import jax
import jax.numpy as jnp
from jax import lax
from jax.experimental import pallas as pl
from jax.experimental.pallas import tpu as pltpu

F32 = jnp.float32
BF16 = jnp.bfloat16

GRID_W = 64
HEAD_DIM = 64
N_KV_HEADS = 4
S5_GROUP_CH = 16
S5_MAX_RE = -1e-4
ROPE_THETA = 10000.0
TOP_K = 8
N_EXPERT_GROUPS = 8
TOPK_GROUPS = 4
ROUTED_SCALE = 2.5
LN_EPS = 1e-5
MOD_EPS = 1e-6
RMS_EPS = 1e-6
N_MOD = 6

LANES = 128
SUBLANES = 8
VMEM_LIMIT_BYTES = 56 * 1024 * 1024

INPROJ_TOKENS = 512
INPROJ_CTX_TOKENS = 256
ATTN_TOKENS = 2048
MIXOUT_TOKENS = 512
S5_CHUNK = 16
S5_COLS = S5_CHUNK * S5_GROUP_CH
MOE_BLOCK = 1024
ROUTE_TOKENS = 512
DISPATCH_TOKENS = 128
DISPATCH_BUFFERS = 3
COMBINE_TOKENS = 128


def _params(sem):
    return pltpu.CompilerParams(dimension_semantics=sem, vmem_limit_bytes=VMEM_LIMIT_BYTES)


def _const_spec(shape):
    nd = len(shape)
    return pl.BlockSpec(shape, lambda *_: (0,) * nd, pipeline_mode=pl.Buffered(1))


def _dot(a, b):
    return jnp.dot(a, b, preferred_element_type=F32)


def _norm_rows(x, eps):
    mu = jnp.mean(x, axis=-1, keepdims=True)
    xc = x - mu
    var = jnp.mean(xc * xc, axis=-1, keepdims=True)
    return xc * lax.rsqrt(var + eps)


def _silu(x):
    return x * jax.nn.sigmoid(x)


def _gelu_tanh(x):
    return 0.5 * x * (1.0 + jnp.tanh(math.sqrt(2.0 / math.pi) * (x + 0.044715 * (x * x * x))))


def _pack_bf16_pairs(v):
    n = v.shape[1] // 2
    bits = lax.bitcast_convert_type(v.astype(BF16).astype(F32), jnp.uint32)
    return bits[:, :n] | (bits[:, n:] >> 16)


def _store_row_pieces(ref, v):
    rows, width = v.shape
    n = width // LANES
    for j in range(n):
        ref[pl.ds(j, rows, stride=n), :] = v[:, j * LANES:(j + 1) * LANES]


def _load_row_pieces(ref, first_row, rows, n):
    return jnp.concatenate([ref[pl.ds(first_row * n + j, rows, stride=n), :] for j in range(n)], axis=1)


def _unpack_bf16_pairs(w):
    hi = lax.bitcast_convert_type(w & jnp.uint32(0xFFFF0000), F32)
    lo = lax.bitcast_convert_type(w << 16, F32)
    return hi, lo


def _mod_kernel(c_ref, w_ref, b_ref, o_ref):
    o_ref[...] = _dot(_silu(c_ref[...]), w_ref[...]) + b_ref[...]


def _mod_call(c_all, w_mod, b_mod):
    rows, d = c_all.shape
    n = w_mod.shape[1]
    tn = d
    return pl.pallas_call(
        _mod_kernel,
        out_shape=jax.ShapeDtypeStruct((rows, n), F32),
        grid=(n // tn,),
        in_specs=[pl.BlockSpec((rows, d), lambda j: (0, 0)),
                  pl.BlockSpec((d, tn), lambda j: (0, j)),
                  pl.BlockSpec((1, tn), lambda j: (0, j))],
        out_specs=pl.BlockSpec((rows, tn), lambda j: (0, j)),
        compiler_params=_params(("arbitrary",)),
        name="mod",
    )(c_all, w_mod, b_mod.reshape(1, n))


def _swap16(t):
    width = t.shape[1]
    lane = lax.broadcasted_iota(jnp.int32, t.shape, 1)
    first = (lane & 16) == 0
    return jnp.where(first, pltpu.roll(t, width - 16, 1), pltpu.roll(t, 16, 1))


def _rms_rope(t, bd_ref, ta, tb):
    msq = _dot((t * t).astype(BF16), bd_ref[...])
    return lax.rsqrt(msq + RMS_EPS) * (t * ta + _swap16(t) * tb)


def _store_padded_heads(t, o_ref, ones_lane=False):
    rows = t.shape[0]
    lane = lax.broadcasted_iota(jnp.int32, (rows, LANES), 1)
    lo = lane < HEAD_DIM
    fill_hi = jnp.where(lane == HEAD_DIM, 1.0, 0.0) if ones_lane else 0.0
    fill_lo = jnp.where(lane == 0, 1.0, 0.0) if ones_lane else 0.0
    for j in range(t.shape[1] // LANES):
        slab = t[:, j * LANES:(j + 1) * LANES]
        swapped = pltpu.roll(slab, HEAD_DIM, 1)
        pieces = (jnp.where(lo, slab, fill_hi), jnp.where(lo, fill_lo, swapped),
                  jnp.where(lo, swapped, fill_hi), jnp.where(lo, fill_lo, slab))
        for p, piece in enumerate(pieces):
            c0 = (4 * j + p) * LANES
            o_ref[:, c0:c0 + LANES] = piece.astype(o_ref.dtype)


def _inproj_kernel(x_ref, sh_ref, sc_ref, w_ref, bd_ref, tka_ref, tkb_ref, *rest, d, kvw, has_q):
    if has_q:
        tqa_ref, tqb_ref, s_ref, k_ref, v_ref, q_ref, gs_ref, ga_ref = rest
    else:
        s_ref, k_ref, v_ref = rest
    u = (_norm_rows(x_ref[...], MOD_EPS) * (1.0 + sc_ref[...]) + sh_ref[...]).astype(BF16)
    col = 0
    s_ref[...] = _dot(u, w_ref[:, col:col + d]).astype(s_ref.dtype)
    col += d
    if has_q:
        for c in range(d // kvw):
            q = _dot(u, w_ref[:, col + c * kvw:col + (c + 1) * kvw])
            q_ref[:, c * kvw:(c + 1) * kvw] = _rms_rope(q, bd_ref, tqa_ref[...], tqb_ref[...]).astype(q_ref.dtype)
        col += d
    k = _dot(u, w_ref[:, col:col + kvw])
    _store_padded_heads(_rms_rope(k, bd_ref, tka_ref[...], tkb_ref[...]), k_ref)
    col += kvw
    _store_padded_heads(_dot(u, w_ref[:, col:col + kvw]), v_ref, ones_lane=True)
    col += kvw
    if has_q:
        gs_ref[...] = jax.nn.sigmoid(_dot(u, w_ref[:, col:col + d])).astype(gs_ref.dtype)
        col += d
        ga_ref[...] = jax.nn.sigmoid(_dot(u, w_ref[:, col:col + d])).astype(ga_ref.dtype)


def _inproj_call(x, shift, scale, w, bd, tka, tkb, tqa=None, tqb=None, *, tm):
    bsz, length, d = x.shape
    has_q = tqa is not None
    kvw = N_KV_HEADS * HEAD_DIM
    per_batch = shift.shape[0] > 1
    tab_rows = tka.shape[0]
    tab_blk = tm if tab_rows > 1 else 1
    mod_spec = pl.BlockSpec((None, 1, d), (lambda b, i: (b, 0, 0)) if per_batch else (lambda b, i: (0, 0, 0)))
    tab_spec = pl.BlockSpec((tab_blk, kvw), (lambda b, i: (i, 0)) if tab_rows > 1 else (lambda b, i: (0, 0)))
    row_spec = lambda width: pl.BlockSpec((None, tm, width), lambda b, i: (b, i, 0))
    in_specs = [row_spec(d), mod_spec, mod_spec, _const_spec(w.shape), _const_spec(bd.shape), tab_spec, tab_spec]
    args = [x, shift, scale, w, bd, tka, tkb]
    widths = [d, 4 * kvw, 4 * kvw]
    if has_q:
        in_specs += [tab_spec, tab_spec]
        args += [tqa, tqb]
        widths += [d, d, d]
    return pl.pallas_call(
        functools.partial(_inproj_kernel, d=d, kvw=kvw, has_q=has_q),
        out_shape=[jax.ShapeDtypeStruct((bsz, length, wd), BF16) for wd in widths],
        grid=(bsz, length // tm),
        in_specs=in_specs,
        out_specs=[row_spec(wd) for wd in widths],
        compiler_params=_params(("parallel", "parallel")),
        name="inproj_lat" if has_q else "inproj_ctx",
    )(*args)


def _rope_tables(seq, gain, scale):
    half = HEAD_DIM // 2
    inv_freq = ROPE_THETA ** (-jnp.arange(0, half, 2, dtype=F32) / half)
    t = jnp.arange(seq, dtype=jnp.int32)
    pos = jnp.stack([(t // GRID_W).astype(F32), (t % GRID_W).astype(F32)], axis=1)
    dim = jnp.arange(HEAD_DIM)
    axis = dim // half
    second = ((dim % half) // (half // 2)) == 1
    freq = inv_freq[dim % (half // 2)]
    ang = pos[:, axis] * freq[None, :]
    partner = jnp.where(second, dim - half // 2, dim + half // 2)
    g = gain.astype(F32)
    ta = jnp.cos(ang) * g[None, :] * scale
    tb = jnp.sin(ang) * jnp.where(second, 1.0, -1.0)[None, :] * g[partner][None, :] * scale
    return jnp.tile(ta, (1, N_KV_HEADS)), jnp.tile(tb, (1, N_KV_HEADS))


def _attn_kernel(q_ref, kc_ref, kl_ref, vc_ref, vl_ref, o_ref):
    nt = (((1,), (1,)), ((), ()))
    rows = q_ref.shape[0]
    lane = lax.broadcasted_iota(jnp.int32, (rows, LANES), 1)
    for j in range(q_ref.shape[1] // LANES):
        qs = q_ref[:, j * LANES:(j + 1) * LANES]
        acc = []
        for half in range(2):
            cols = slice(half * LANES, (half + 1) * LANES)
            s_c = lax.dot_general(qs, kc_ref[:, cols], nt, preferred_element_type=F32)
            s_l = lax.dot_general(qs, kl_ref[:, cols], nt, preferred_element_type=F32)
            m = jnp.maximum(jnp.max(s_c, axis=-1, keepdims=True), jnp.max(s_l, axis=-1, keepdims=True))
            e_c = jnp.exp((s_c - m).astype(BF16))
            e_l = jnp.exp((s_l - m).astype(BF16))
            acc.append(_dot(e_c, vc_ref[:, cols]) + _dot(e_l, vl_ref[:, cols]))
        out = jnp.where(lane < HEAD_DIM, acc[0] / acc[0][:, HEAD_DIM:HEAD_DIM + 1], acc[1] / acc[1][:, 0:1])
        o_ref[:, j * LANES:(j + 1) * LANES] = out.astype(o_ref.dtype)


def _attn_call(q, kc, kl, vc, vl, *, tq):
    bsz, seq, d = q.shape
    ctx = kc.shape[1]
    gw = d // N_KV_HEADS
    q_spec = pl.BlockSpec((None, tq, gw), lambda b, h, i: (b, i, h))
    kv_spec = lambda length: pl.BlockSpec((None, length, 2 * LANES), lambda b, h, i: (b, 0, h))
    return pl.pallas_call(
        _attn_kernel,
        out_shape=jax.ShapeDtypeStruct((bsz, seq, d), BF16),
        grid=(bsz, N_KV_HEADS, seq // tq),
        in_specs=[q_spec, kv_spec(ctx), kv_spec(seq), kv_spec(ctx), kv_spec(seq)],
        out_specs=q_spec,
        compiler_params=_params(("parallel", "parallel", "arbitrary")),
        name="attention",
    )(q, kc, kl, vc, vl)


def _s5_tables(lam_re, lam_im, log_dt, b_re, b_im, c_re, c_im):
    n = S5_CHUNK
    lam = lax.complex(jnp.minimum(lam_re.astype(F32), S5_MAX_RE), lam_im.astype(F32))
    lam_dt = lam * jnp.exp(log_dt.astype(F32))[..., None]
    b_bar = ((jnp.exp(lam_dt) - 1.0) / lam)[..., None] * lax.complex(b_re.astype(F32), b_im.astype(F32))
    c_mat = lax.complex(c_re.astype(F32), c_im.astype(F32))
    pw = jnp.exp(lam_dt[None] * jnp.arange(n + 1, dtype=F32)[:, None, None, None])
    kern = jnp.real(jnp.einsum('dgcp,jdgp,dgpe->djgce', c_mat, pw[:n], b_bar))
    s_idx = jnp.arange(n)[:, None]
    t_idx = jnp.arange(n)[None, :]

    def toeplitz(k, lag):
        onehot = (lag[None] == jnp.arange(n)[:, None, None]).astype(F32)
        return jnp.einsum('jst,jgce->gsetc', onehot, k, precision=lax.Precision.HIGHEST)

    toep = toeplitz(kern[0], t_idx - s_idx) + toeplitz(kern[1], s_idx - t_idx)
    groups = toep.shape[0]
    toep = toep.reshape(groups, S5_COLS, S5_COLS)
    wis_f = jnp.einsum('sgp,gpe->gsep', pw[n - 1 - jnp.arange(n), 0], b_bar[0]).reshape(groups, S5_COLS, -1)
    wis_r = jnp.einsum('sgp,gpe->gsep', pw[jnp.arange(n), 1], b_bar[1]).reshape(groups, S5_COLS, -1)
    w1 = jnp.concatenate([toep, jnp.real(wis_f), jnp.real(wis_r), jnp.imag(wis_f), jnp.imag(wis_r)], axis=-1)
    m_f = jnp.einsum('gcp,tgp->gptc', c_mat[0], pw[1 + jnp.arange(n), 0]).reshape(groups, -1, S5_COLS)
    m_r = jnp.einsum('gcp,tgp->gptc', c_mat[1], pw[n - jnp.arange(n), 1]).reshape(groups, -1, S5_COLS)
    wso = jnp.concatenate([jnp.real(m_f), jnp.real(m_r), -jnp.imag(m_f), -jnp.imag(m_r)], axis=1)
    lam_n = jnp.concatenate([pw[n, 0], pw[n, 1]], axis=-1)
    lam_tab = jnp.stack([jnp.real(lam_n), jnp.imag(lam_n)], axis=1)
    return w1.astype(BF16), wso.astype(BF16), lam_tab


def _s5_kernel(uc_ref, ul_ref, w1_ref, wso_ref, lam_ref, y_ref, a_ref, xf_re, xf_im, xr_re, xr_im,
               *, nb, nc_ctx, nc):
    cols = S5_COLS
    half = lam_ref.shape[1] // 2
    a_ref[:nc_ctx * nb, :] = _dot(uc_ref[...], w1_ref[...])
    a_ref[nc_ctx * nb:, :] = _dot(ul_ref[...], w1_ref[...])
    l_re = lam_ref[0:1, :]
    l_im = lam_ref[1:2, :]
    lo = lax.broadcasted_iota(jnp.int32, (nb, 2 * half), 1) < half

    def step(i, carry):
        x_re, x_im = carry
        i_rev = jnp.where(i < nc_ctx, nc_ctx - 1 - i, nc - 1 + nc_ctx - i)
        rf = pl.ds(pl.multiple_of(i * nb, nb), nb)
        rr = pl.ds(pl.multiple_of(i_rev * nb, nb), nb)
        xf_re[rf, :] = x_re
        xf_im[rf, :] = x_im
        xr_re[rr, :] = x_re
        xr_im[rr, :] = x_im
        s_re = jnp.where(lo, a_ref[rf, cols:cols + 2 * half], a_ref[rr, cols:cols + 2 * half])
        s_im = jnp.where(lo, a_ref[rf, cols + 2 * half:cols + 4 * half], a_ref[rr, cols + 2 * half:cols + 4 * half])
        return l_re * x_re - l_im * x_im + s_re, l_re * x_im + l_im * x_re + s_im

    zero = jnp.zeros((nb, 2 * half), F32)
    lax.fori_loop(0, nc, step, (zero, zero))
    r0 = nc_ctx * nb
    rows = (nc - nc_ctx) * nb
    lo_all = lax.broadcasted_iota(jnp.int32, (rows, 2 * half), 1) < half
    y = a_ref[r0:, 0:cols]
    for k, (f_ref, r_ref) in enumerate(((xf_re, xr_re), (xf_im, xr_im))):
        st = jnp.where(lo_all, f_ref[r0:, :], r_ref[r0:, :])
        hi = st.astype(BF16)
        lo_part = (st - hi.astype(F32)).astype(BF16)
        w = wso_ref[k * 2 * half:(k + 1) * 2 * half, :]
        y = y + _dot(hi, w) + _dot(lo_part, w)
    y_ref[...] = y.astype(y_ref.dtype)


def _s5_call(u_ctx, u_lat, w1, wso, lam_tab, *, nb):
    groups, out_rows, cols = u_lat.shape
    ctx_rows = u_ctx.shape[1]
    rows = ctx_rows + out_rows
    nc = rows // nb
    nc_ctx = ctx_rows // nb
    st = lam_tab.shape[2]
    return pl.pallas_call(
        functools.partial(_s5_kernel, nb=nb, nc_ctx=nc_ctx, nc=nc),
        out_shape=jax.ShapeDtypeStruct((groups, out_rows, cols), BF16),
        grid=(groups,),
        in_specs=[pl.BlockSpec((None, ctx_rows, cols), lambda g: (g, 0, 0)),
                  pl.BlockSpec((None, out_rows, cols), lambda g: (g, 0, 0)),
                  pl.BlockSpec((None,) + w1.shape[1:], lambda g: (g, 0, 0)),
                  pl.BlockSpec((None,) + wso.shape[1:], lambda g: (g, 0, 0)),
                  pl.BlockSpec((None, 2, st), lambda g: (g, 0, 0))],
        out_specs=pl.BlockSpec((None, out_rows, cols), lambda g: (g, 0, 0)),
        scratch_shapes=[pltpu.VMEM((rows, w1.shape[2]), F32)] + [pltpu.VMEM((rows, st), F32)] * 4,
        compiler_params=_params(("parallel",)),
        name="s5",
    )(u_ctx, u_lat, w1, wso, lam_tab)


def _to_group_major(s):
    bsz, length, width = s.shape
    groups = width // S5_GROUP_CH
    nc = length // S5_CHUNK
    t = s.reshape(bsz, nc, S5_CHUNK, groups, S5_GROUP_CH).transpose(3, 1, 0, 2, 4)
    return t.reshape(groups, nc * bsz, S5_COLS)


def _to_token_major(y, bsz):
    groups, rows, _ = y.shape
    nc = rows // bsz
    t = y.reshape(groups, nc, bsz, S5_CHUNK, S5_GROUP_CH).transpose(2, 1, 3, 0, 4)
    return t.reshape(bsz, nc * S5_CHUNK, groups * S5_GROUP_CH)


def _mixout_kernel(y_ref, s_ref, at_ref, gs_ref, ga_ref, x_ref, g1_ref, sh2_ref, sc2_ref, dsk_ref,
                   wglu_ref, wbs_ref, wba_ref, wout_ref, lng_ref, lnb_ref, wrt_ref, wss_ref, wsd_ref,
                   x1_ref, h_ref, shared_ref, scores_ref, *, alpha):
    y = y_ref[...].astype(F32) + s_ref[...].astype(F32) * dsk_ref[...]
    z = _gelu_tanh(y)
    ssm = z * jax.nn.sigmoid(_dot(z.astype(BF16), wglu_ref[...]))
    merged = (gs_ref[...].astype(F32) * _dot(ssm.astype(BF16), wbs_ref[...])
              + ga_ref[...].astype(F32) * _dot(at_ref[...], wba_ref[...]))
    y_mix = _dot(merged.astype(BF16), wout_ref[...])
    x1 = _norm_rows(alpha * x_ref[...] + g1_ref[...] * y_mix, LN_EPS) * lng_ref[...] + lnb_ref[...]
    x1_ref[...] = x1
    h = _norm_rows(x1, MOD_EPS) * (1.0 + sc2_ref[...]) + sh2_ref[...]
    _store_row_pieces(h_ref, _pack_bf16_pairs(h))
    hb = h.astype(BF16)
    scores_ref[...] = jax.nn.sigmoid(
        lax.dot_general(wrt_ref[...], hb, (((1,), (1,)), ((), ())), preferred_element_type=F32))
    ss = _dot(hb, wss_ref[...])
    sh_hidden = wsd_ref.shape[0]
    hid = _silu(ss[:, :sh_hidden]) * ss[:, sh_hidden:]
    shared_ref[...] = _dot(hid.astype(BF16), wsd_ref[...])


def _mixout_call(y, s, attn, gs, ga, x, g1, sh2, sc2, dsk, wglu, wbs, wba, wout, lng, lnb, wrt, wss, wsd,
                 *, tm, alpha):
    bsz, seq, d = x.shape
    n_exp = wrt.shape[0]
    pieces = d // 2 // LANES
    row = lambda width: pl.BlockSpec((None, tm, width), lambda b, i: (b, i, 0))
    mod = pl.BlockSpec((None, 1, d), lambda b, i: (b, 0, 0))
    consts = [dsk, wglu, wbs, wba, wout, lng, lnb, wrt, wss, wsd]
    return pl.pallas_call(
        functools.partial(_mixout_kernel, alpha=alpha),
        out_shape=[jax.ShapeDtypeStruct((bsz, seq, d), F32),
                   jax.ShapeDtypeStruct((bsz, seq * pieces, LANES), jnp.uint32),
                   jax.ShapeDtypeStruct((bsz, seq, d), F32), jax.ShapeDtypeStruct((bsz, n_exp, seq), F32)],
        grid=(bsz, seq // tm),
        in_specs=[row(d)] * 6 + [mod] * 3 + [_const_spec(a.shape) for a in consts],
        out_specs=[row(d), pl.BlockSpec((None, tm * pieces, LANES), lambda b, i: (b, i, 0)),
                   row(d), pl.BlockSpec((None, n_exp, tm), lambda b, i: (b, 0, i))],
        compiler_params=_params(("parallel", "parallel")),
        name="mixout",
    )(y, s, attn, gs, ga, x, g1, sh2, sc2, *consts)


def _stack_rows(rows):
    n = rows[0].shape[1]
    row_k = lax.broadcasted_iota(jnp.int32, (len(rows), n), 0)
    out = jnp.zeros((len(rows), n), rows[0].dtype)
    for k, r in enumerate(rows):
        out = jnp.where(row_k == k, r, out)
    return out


def _route_kernel(s_ref, bias_ref, utri_ref, idx_ref, gate_ref, rank_ref, counts_ref, base_ref):
    @pl.when((pl.program_id(0) == 0) & (pl.program_id(1) == 0))
    def _():
        base_ref[...] = jnp.zeros_like(base_ref)

    s = s_ref[...]
    n_exp, tm = s.shape
    per_group = n_exp // N_EXPERT_GROUPS
    neg = -jnp.inf
    b = s + bias_ref[...]
    bg = b.reshape(N_EXPERT_GROUPS, per_group, tm)
    m1 = jnp.max(bg, axis=1, keepdims=True)
    is1 = bg == m1
    n1 = jnp.sum(jnp.where(is1, 1.0, 0.0), axis=1, keepdims=True)
    m2 = jnp.max(jnp.where(is1, neg, bg), axis=1, keepdims=True)
    gscore = m1 + jnp.where(n1 >= 2.0, m1, m2)
    gs = [gscore[g] for g in range(N_EXPERT_GROUPS)]
    kept = []
    for g in range(N_EXPERT_GROUPS):
        beaten = jnp.zeros((1, tm), F32)
        for o in range(N_EXPERT_GROUPS):
            if o != g:
                wins = (gs[o] >= gs[g]) if o < g else (gs[o] > gs[g])
                beaten = beaten + jnp.where(wins, 1.0, 0.0)
        kept.append(jnp.where(beaten < float(TOPK_GROUPS), bg[g], neg))
    masked = jnp.concatenate(kept, axis=0)
    row_f = lax.broadcasted_iota(jnp.int32, (n_exp, tm), 0).astype(F32)
    chosen = jnp.zeros((n_exp, tm), F32)
    firsts, vals = [], []
    for _ in range(TOP_K):
        m = jnp.max(masked, axis=0, keepdims=True)
        first = jnp.min(jnp.where(masked == m, row_f, float(n_exp)), axis=0, keepdims=True)
        sel = row_f == first
        firsts.append(first)
        vals.append(jnp.sum(jnp.where(sel, s, 0.0), axis=0, keepdims=True))
        chosen = jnp.where(sel, 1.0, chosen)
        masked = jnp.where(sel, neg, masked)
    before = _dot(chosen.astype(BF16), utri_ref[...]) + base_ref[...]
    ranks = [jnp.sum(jnp.where(row_f == f, before, 0.0), axis=0, keepdims=True) for f in firsts]
    idx_ref[...] = _stack_rows(firsts).astype(jnp.int32)
    rank_ref[...] = _stack_rows(ranks).astype(jnp.int32)
    val = _stack_rows(vals)
    gate_ref[...] = val / jnp.sum(val, axis=0, keepdims=True) * ROUTED_SCALE
    base_ref[...] = base_ref[...] + jnp.sum(chosen, axis=1, keepdims=True)
    counts_ref[...] = base_ref[...]


def _route_call(scores_t, bias, *, tm):
    bsz, n_exp, seq = scores_t.shape
    tiles = seq // tm
    tokens = bsz * seq
    r_id = lax.broadcasted_iota(jnp.int32, (tm, tm), 0)
    c_id = lax.broadcasted_iota(jnp.int32, (tm, tm), 1)
    utri = (r_id < c_id).astype(BF16)
    small = pl.BlockSpec((TOP_K, tm), lambda b, i: (0, b * tiles + i))
    col = pl.BlockSpec((n_exp, 1), lambda b, i: (0, 0))
    return pl.pallas_call(
        _route_kernel,
        out_shape=[jax.ShapeDtypeStruct((TOP_K, tokens), jnp.int32), jax.ShapeDtypeStruct((TOP_K, tokens), F32),
                   jax.ShapeDtypeStruct((TOP_K, tokens), jnp.int32), jax.ShapeDtypeStruct((n_exp, 1), F32)],
        grid=(bsz, tiles),
        in_specs=[pl.BlockSpec((None, n_exp, tm), lambda b, i: (b, 0, i)), col, _const_spec((tm, tm))],
        out_specs=[small, small, small, col],
        scratch_shapes=[pltpu.VMEM((n_exp, 1), F32)],
        compiler_params=_params(("arbitrary", "arbitrary")),
        name="route",
    )(scores_t, bias.astype(F32).reshape(n_exp, 1), utri)


def _slot_kernel(idx_ref, rank_ref, start_ref, pos_ref):
    idx = idx_ref[...]
    tm = idx.shape[1]
    n_exp = start_ref.shape[0]
    row = lax.broadcasted_iota(jnp.int32, (n_exp, tm), 0)
    starts = [jnp.sum(jnp.where(row == idx[k:k + 1, :], start_ref[...], 0.0), axis=0, keepdims=True)
              for k in range(TOP_K)]
    pos_ref[...] = rank_ref[...] + _stack_rows(starts).astype(jnp.int32)


def _slot_call(idx, rank, pad_start, *, tm):
    tokens = idx.shape[1]
    n_exp = pad_start.shape[0]
    small = pl.BlockSpec((TOP_K, tm), lambda i: (0, i))
    return pl.pallas_call(
        _slot_kernel,
        out_shape=jax.ShapeDtypeStruct((TOP_K, tokens), jnp.int32),
        grid=(tokens // tm,),
        in_specs=[small, small, _const_spec((n_exp, 1))],
        out_specs=small,
        compiler_params=_params(("parallel",)),
        name="slots",
    )(idx, rank, pad_start.astype(F32).reshape(n_exp, 1))


def _dispatch_kernel(start_ref, padded_ref, nact_ref, pos_ref, h_hbm, xs_hbm, hbuf, zblock, lsem, ssem, zsem,
                     *, pieces):
    i = pl.program_id(0)
    n = pl.num_programs(0)
    nbuf = hbuf.shape[0]
    td = hbuf.shape[1] // pieces
    bm = zblock.shape[0] // pieces
    n_exp = start_ref.shape[0]
    slot = i % nbuf

    def zero_copy(first_row):
        return pltpu.make_async_copy(zblock, xs_hbm.at[pl.ds(first_row * pieces, bm * pieces), :], zsem)

    def load(step, s):
        return pltpu.make_async_copy(h_hbm.at[pl.ds(step * td * pieces, td * pieces), :], hbuf.at[s], lsem.at[s])

    def wait_scatters(s):
        rows = td * TOP_K * pieces
        pltpu.make_async_copy(h_hbm.at[pl.ds(0, rows), :], xs_hbm.at[pl.ds(0, rows), :], ssem.at[s]).wait()

    @pl.when(i == 0)
    def _():
        zblock[...] = jnp.zeros_like(zblock)

        def per_expert(e, carry):
            @pl.when(padded_ref[e] > 0)
            def _():
                zero_copy(start_ref[e] + padded_ref[e] - bm).start()
            return carry
        lax.fori_loop(0, n_exp, per_expert, 0)

        def per_expert_wait(e, carry):
            @pl.when(padded_ref[e] > 0)
            def _():
                zero_copy(0).wait()
            return carry
        lax.fori_loop(0, n_exp, per_expert_wait, 0)

        n_blocks = xs_hbm.shape[0] // (bm * pieces)

        def tail(b, carry):
            zero_copy(b * bm).start()
            return carry
        lax.fori_loop(nact_ref[0], n_blocks, tail, 0)

        def tail_wait(b, carry):
            zero_copy(0).wait()
            return carry
        lax.fori_loop(nact_ref[0], n_blocks, tail_wait, 0)
        load(0, 0).start()

    @pl.when(i >= nbuf - 1)
    def _():
        wait_scatters((i + 1) % nbuf)

    @pl.when(i + 1 < n)
    def _():
        load(i + 1, (i + 1) % nbuf).start()

    load(i, slot).wait()

    for t in range(td):
        for k in range(TOP_K):
            dst = pl.multiple_of(pos_ref[k, t] * pieces, pieces)
            pltpu.make_async_copy(hbuf.at[slot, pl.ds(t * pieces, pieces), :], xs_hbm.at[pl.ds(dst, pieces), :],
                                  ssem.at[slot]).start(priority=k % 2)

    @pl.when(i == n - 1)
    def _():
        for back in range(nbuf - 1):
            @pl.when(i >= back)
            def _():
                wait_scatters((i - back) % nbuf)


def _dispatch_call(pad_start, padded, n_active, pos, hp, n_slots, *, td, bm, pieces):
    tokens = hp.shape[0] // pieces
    grid_spec = pltpu.PrefetchScalarGridSpec(
        num_scalar_prefetch=3,
        grid=(tokens // td,),
        in_specs=[pl.BlockSpec((TOP_K, td), lambda i, *_: (0, i), memory_space=pltpu.SMEM),
                  pl.BlockSpec(memory_space=pl.ANY)],
        out_specs=pl.BlockSpec(memory_space=pl.ANY),
        scratch_shapes=[pltpu.VMEM((DISPATCH_BUFFERS, td * pieces, LANES), hp.dtype),
                        pltpu.VMEM((bm * pieces, LANES), hp.dtype),
                        pltpu.SemaphoreType.DMA((DISPATCH_BUFFERS,)), pltpu.SemaphoreType.DMA((DISPATCH_BUFFERS,)),
                        pltpu.SemaphoreType.DMA],
    )
    return pl.pallas_call(
        functools.partial(_dispatch_kernel, pieces=pieces),
        out_shape=jax.ShapeDtypeStruct((n_slots * pieces, LANES), hp.dtype),
        grid_spec=grid_spec,
        compiler_params=_params(("arbitrary",)),
        name="dispatch",
    )(pad_start, padded, n_active, pos, hp)


def _experts_kernel(be_ref, nact_ref, x_ref, wg_ref, wu_ref, wd_ref, y_ref, wg_s, wu_s, wd_s):
    i = pl.program_id(0)
    last = nact_ref[0] - 1
    expert = be_ref[jnp.minimum(i, last)]
    previous = be_ref[jnp.minimum(jnp.maximum(i, 1) - 1, last)]

    @pl.when((i == 0) | (expert != previous))
    def _():
        wg_s[...] = wg_ref[...].astype(BF16)
        wu_s[...] = wu_ref[...].astype(BF16)
        wd_s[...] = wd_ref[...].astype(BF16)

    @pl.when(i <= last)
    def _():
        half = wg_s.shape[0] // 2
        pieces = half // LANES
        bm = x_ref.shape[0] // pieces
        x_hi, x_lo = (v.astype(BF16) for v in _unpack_bf16_pairs(_load_row_pieces(x_ref, 0, bm, pieces)))
        hg = _dot(x_hi, wg_s[:half, :]) + _dot(x_lo, wg_s[half:, :])
        hu = _dot(x_hi, wu_s[:half, :]) + _dot(x_lo, wu_s[half:, :])
        _store_row_pieces(y_ref, _pack_bf16_pairs(_dot((_silu(hg) * hu).astype(BF16), wd_s[...])))

    @pl.when(i > last)
    def _():
        y_ref[...] = jnp.zeros_like(y_ref)


def _experts_call(block_expert, n_active, xs, wg, wu, wd):
    n_blocks = block_expert.shape[0]
    block_rows = xs.shape[0] // n_blocks
    d, hidden = wg.shape[1:]
    blk = lambda i, be, na: (jnp.minimum(i, na[0] - 1), 0)
    wsel = lambda i, be, na: (be[jnp.minimum(i, na[0] - 1)], 0, 0)
    grid_spec = pltpu.PrefetchScalarGridSpec(
        num_scalar_prefetch=2,
        grid=(n_blocks,),
        in_specs=[pl.BlockSpec((block_rows, LANES), blk),
                  pl.BlockSpec((None, d, hidden), wsel),
                  pl.BlockSpec((None, d, hidden), wsel),
                  pl.BlockSpec((None, hidden, d), wsel)],
        out_specs=pl.BlockSpec((block_rows, LANES), lambda i, be, na: (i, 0)),
        scratch_shapes=[pltpu.VMEM((d, hidden), BF16), pltpu.VMEM((d, hidden), BF16), pltpu.VMEM((hidden, d), BF16)],
    )
    return pl.pallas_call(
        _experts_kernel,
        out_shape=jax.ShapeDtypeStruct(xs.shape, xs.dtype),
        grid_spec=grid_spec,
        compiler_params=_params(("arbitrary",)),
        name="experts",
    )(block_expert, n_active, xs, wg, wu, wd)


def _row_gather_start(idx_ref, n_rows, pieces, src_hbm, dst_ref, sem):
    tc = idx_ref.shape[1]
    for r in range(n_rows):
        src = pl.multiple_of(idx_ref[r // tc, r % tc] * pieces, pieces)
        pltpu.make_async_copy(src_hbm.at[pl.ds(src, pieces), :], dst_ref.at[pl.ds(r * pieces, pieces), :],
                              sem).start(priority=r % 2)


def _row_gather_wait(src_hbm, dst_ref, sem):
    pltpu.make_async_copy(src_hbm.at[pl.ds(0, dst_ref.shape[0]), :], dst_ref, sem).wait()


def _combine_kernel(pos_ref, pos1_ref, pos2_ref, y_hbm, gate_ref, shared_ref, x1_ref, g2_ref, lng_ref, lnb_ref,
                    o_ref, buf_a, buf_b, buf_c, sems, *, alpha):
    i = pl.program_id(0)
    n = pl.num_programs(0)
    tc, d = o_ref.shape
    half = d // 2
    pieces = half // LANES
    rows = buf_a.shape[0] // pieces
    bufs = (buf_a, buf_b, buf_c)

    @pl.when(i == 0)
    def _():
        _row_gather_start(pos_ref, rows, pieces, y_hbm, buf_a, sems.at[0])
        _row_gather_start(pos1_ref, rows, pieces, y_hbm, buf_b, sems.at[1])

    def step(r):
        cur, nxt, far = bufs[r], bufs[(r + 1) % 3], bufs[(r + 2) % 3]
        _row_gather_wait(y_hbm, cur, sems.at[r])
        _row_gather_start(pos2_ref, rows, pieces, y_hbm, far, sems.at[(r + 2) % 3])
        r_id = lax.broadcasted_iota(jnp.int32, (tc, tc), 0)
        c_id = lax.broadcasted_iota(jnp.int32, (tc, tc), 1)
        f_hi = shared_ref[:, :half]
        f_lo = shared_ref[:, half:]
        for k in range(rows // tc):
            y_hi, y_lo = _unpack_bf16_pairs(_load_row_pieces(cur, k * tc, tc, pieces))
            g = jnp.sum(jnp.where(r_id == c_id, gate_ref[k:k + 1, :], 0.0), axis=1, keepdims=True)
            f_hi = f_hi + g * y_hi
            f_lo = f_lo + g * y_lo
        f = jnp.concatenate([f_hi, f_lo], axis=1)
        o_ref[...] = _norm_rows(alpha * x1_ref[...] + g2_ref[...] * f, LN_EPS) * lng_ref[...] + lnb_ref[...]

        @pl.when(i == n - 1)
        def _():
            _row_gather_wait(y_hbm, nxt, sems.at[(r + 1) % 3])
            _row_gather_wait(y_hbm, far, sems.at[(r + 2) % 3])

    for r in range(3):
        @pl.when(i % 3 == r)
        def _():
            step(r)


def _combine_call(pos, y_slots, gates, shared, x1, g2, lng, lnb, *, alpha, tc, tiles_per_batch):
    tokens, d = x1.shape
    n_tiles = tokens // tc
    last = n_tiles - 1
    row = pl.BlockSpec((tc, d), lambda i: (i, 0))
    buf = pltpu.VMEM((TOP_K * tc * (d // 2 // LANES), LANES), y_slots.dtype)
    return pl.pallas_call(
        functools.partial(_combine_kernel, alpha=alpha),
        out_shape=jax.ShapeDtypeStruct((tokens, d), F32),
        grid=(n_tiles,),
        in_specs=[
            pl.BlockSpec((TOP_K, tc), lambda i: (0, i), memory_space=pltpu.SMEM),
            pl.BlockSpec((TOP_K, tc), lambda i: (0, jnp.minimum(i + 1, last)), memory_space=pltpu.SMEM),
            pl.BlockSpec((TOP_K, tc), lambda i: (0, jnp.minimum(i + 2, last)), memory_space=pltpu.SMEM),
            pl.BlockSpec(memory_space=pl.ANY),
            pl.BlockSpec((TOP_K, tc), lambda i: (0, i)),
            row, row,
            pl.BlockSpec((None, 1, d), lambda i: (i // tiles_per_batch, 0, 0)),
            _const_spec(lng.shape), _const_spec(lnb.shape),
        ],
        out_specs=row,
        scratch_shapes=[buf, buf, buf, pltpu.SemaphoreType.DMA((3,))],
        compiler_params=_params(("arbitrary",)),
        name="combine",
    )(pos, pos, pos, y_slots, gates, shared, x1, g2, lng, lnb)


def kernel(x, c, ctx, c_ctx, w_mod, b_mod, w_in, s5_lam_re, s5_lam_im, s5_log_dt, s5_b_re, s5_b_im, s5_c_re, s5_c_im, s5_d, w_glu, q_norm_g, k_norm_g, w_branch_ssm, w_branch_attn, w_out, ln1_g, ln1_b, w_router, router_bias, w_exp_gate, w_exp_up, w_exp_down, w_sh_gate, w_sh_up, w_sh_down, ln2_g, ln2_b):
    depth = w_mod.shape[0]
    assert depth == 1, "single-layer block: context outputs are never needed"
    bsz, seq, d = x.shape
    ctx_len = ctx.shape[1]
    kvw = N_KV_HEADS * HEAD_DIM
    n_exp = w_router.shape[2]
    alpha = (2.0 * depth) ** 0.25
    assert seq % GRID_W == 0 and seq % S5_CHUNK == 0 and ctx_len % S5_CHUNK == 0 and bsz % SUBLANES == 0
    lay = 0

    pad = (-(bsz + 1)) % SUBLANES
    c_all = jnp.concatenate([c, c_ctx[None, :], jnp.zeros((pad, d), F32)], axis=0)
    mod = _mod_call(c_all, w_mod[lay], b_mod[lay])
    mod_lat = mod[:bsz].reshape(bsz, N_MOD, 1, d)
    sh1, sc1, g1, sh2, sc2, g2 = (mod_lat[:, k] for k in range(N_MOD))
    mod_ctx = mod[bsz].reshape(N_MOD, 1, 1, d)

    w_in_l = w_in[lay].astype(BF16)
    w_ctx = jnp.concatenate([w_in_l[:, :d], w_in_l[:, 2 * d:2 * d + 2 * kvw]], axis=1)
    head_id = jnp.arange(kvw) // HEAD_DIM
    bd = jnp.where(head_id[:, None] == head_id[None, :], 1.0 / HEAD_DIM, 0.0).astype(BF16)
    tqa, tqb = _rope_tables(seq, q_norm_g[lay], HEAD_DIM ** -0.5)
    tka, tkb = _rope_tables(seq, k_norm_g[lay], 1.0)
    tca = jnp.tile(k_norm_g[lay].astype(F32), N_KV_HEADS)[None, :]
    s_lat, k_lat, v_lat, q_lat, gs_lat, ga_lat = _inproj_call(x, sh1, sc1, w_in_l, bd, tka, tkb, tqa, tqb,
                                                              tm=min(INPROJ_TOKENS, seq))
    s_ctx, k_ctx, v_ctx = _inproj_call(ctx, mod_ctx[0], mod_ctx[1], w_ctx, bd, tca, jnp.zeros_like(tca),
                                       tm=min(INPROJ_CTX_TOKENS, ctx_len))

    w1, wso, lam_tab = _s5_tables(s5_lam_re[lay], s5_lam_im[lay], s5_log_dt[lay], s5_b_re[lay], s5_b_im[lay],
                                  s5_c_re[lay], s5_c_im[lay])
    y_groups = _s5_call(_to_group_major(s_ctx), _to_group_major(s_lat), w1, wso, lam_tab, nb=bsz)
    y_s5 = _to_token_major(y_groups, bsz)

    attn = _attn_call(q_lat, k_ctx, k_lat, v_ctx, v_lat, tq=min(ATTN_TOKENS, seq))

    row = lambda v: v.astype(F32).reshape(1, -1)
    wss = jnp.concatenate([w_sh_gate[lay], w_sh_up[lay]], axis=1).astype(BF16)
    x1, hp, shared, scores_t = _mixout_call(
        y_s5, s_lat, attn, gs_lat, ga_lat, x, g1, sh2, sc2, row(s5_d[lay]),
        w_glu[lay].astype(BF16), w_branch_ssm[lay].astype(BF16), w_branch_attn[lay].astype(BF16),
        w_out[lay].astype(BF16), row(ln1_g[lay]), row(ln1_b[lay]), w_router[lay].T.astype(BF16), wss,
        w_sh_down[lay].astype(BF16), tm=min(MIXOUT_TOKENS, seq), alpha=alpha)

    tokens = bsz * seq
    bm = MOE_BLOCK
    idx, gates, rank, counts_f = _route_call(scores_t, router_bias[lay], tm=min(ROUTE_TOKENS, seq))
    counts = counts_f.reshape(n_exp).astype(jnp.int32)
    padded = (counts + bm - 1) // bm * bm
    pad_end = jnp.cumsum(padded)
    pad_start = pad_end - padded
    n_blocks = (tokens * TOP_K + n_exp * (bm - 1) + bm - 1) // bm
    block_start = jnp.arange(n_blocks, dtype=jnp.int32) * bm
    block_expert = jnp.minimum(jnp.sum((pad_end[None, :] <= block_start[:, None]).astype(jnp.int32), axis=1), n_exp - 1)
    n_active = (pad_end[-1] // bm).reshape(1)
    pos = _slot_call(idx, rank, pad_start, tm=min(ROUTE_TOKENS, tokens))
    pieces = d // 2 // LANES
    xs = _dispatch_call(pad_start, padded, n_active, pos, hp.reshape(tokens * pieces, LANES), n_blocks * bm,
                        td=min(DISPATCH_TOKENS, tokens), bm=bm, pieces=pieces)
    y_slots = _experts_call(block_expert, n_active, xs, w_exp_gate[lay], w_exp_up[lay], w_exp_down[lay])

    tc = min(COMBINE_TOKENS, seq)
    out = _combine_call(pos, y_slots, gates, shared.reshape(tokens, d), x1.reshape(tokens, d), g2,
                        row(ln2_g[lay]), row(ln2_b[lay]), alpha=alpha, tc=tc, tiles_per_batch=seq // tc)
    return out.reshape(bsz, seq, d)
```
